```python
import jax, jax.numpy as jnp
from jax import lax
import numpy as np

D_MODEL = 1024
BATCH = 8
SEQ = 2048
DEPTH = 2

GRID_W = 64
CTX_LEN = 256
HEAD_DIM = 64
A_HEADS = 8
A_KV_HEADS = 2
A_WIDTH = A_HEADS * HEAD_DIM
A_KV_WIDTH = A_KV_HEADS * HEAD_DIM
Q_BLOCK = 128
ROPE_BASE = 10000.0
B_HEADS = 4
B_WIDTH = B_HEADS * HEAD_DIM
DECAY_LORA = 64
AAA_LORA = 64
GATE_LORA = 128
LNX_EPS = 64e-5
C_HEADS = 4
C_WIDTH = C_HEADS * HEAD_DIM
WIN_ROWS = 8
WIN_COLS = 16
D_MIX = A_WIDTH + B_WIDTH + C_WIDTH
IN_SIZES = (A_WIDTH, A_KV_WIDTH, A_KV_WIDTH,
            B_WIDTH, B_WIDTH, B_WIDTH, 2 * DECAY_LORA, 2 * AAA_LORA, GATE_LORA,
            C_WIDTH, C_WIDTH, C_WIDTH)
N_IN = A_WIDTH + 2 * A_KV_WIDTH + 3 * B_WIDTH + 2 * DECAY_LORA + 2 * AAA_LORA + GATE_LORA + 3 * C_WIDTH
N_EXPERTS = 16
CAPACITY_FACTOR = 2
D_EXPERT = 1024
NORM_EPS = 1e-6

kernel_name = "hybrid_flow_gqa_rwkv7_natten_ecmoe"


def rms_norm(x, g, eps=NORM_EPS):
    xf = x.astype(jnp.float32)
    y = xf * lax.rsqrt(jnp.mean(xf * xf, axis=-1, keepdims=True) + eps)
    return (y * g.astype(jnp.float32)).astype(x.dtype)


def modulate(h, shift, scale):
    return h * (1 + scale) + shift


def rope_1d(x, pos):
    half = x.shape[-1] // 2
    inv = ROPE_BASE ** (-jnp.arange(half, dtype=jnp.float32) / half)
    ang = pos.astype(jnp.float32)[:, None] * inv[None, :]
    cos, sin = jnp.cos(ang)[:, None, :], jnp.sin(ang)[:, None, :]
    xf = x.astype(jnp.float32)
    x1, x2 = xf[..., :half], xf[..., half:]
    return jnp.concatenate([x1 * cos - x2 * sin, x2 * cos + x1 * sin], axis=-1).astype(x.dtype)


def rope_2d(x, row, col):
    h = x.shape[-1] // 2
    return jnp.concatenate([rope_1d(x[..., :h], row), rope_1d(x[..., h:], col)], axis=-1)


def gqa_attend(q, k, v):
    s = jnp.einsum('bmkgd,bskd->bkgms', q, k).astype(jnp.float32) * (HEAD_DIM ** -0.5)
    p = jax.nn.softmax(s, axis=-1).astype(v.dtype)
    return jnp.einsum('bkgms,bskd->bmkgd', p, v)


def mixer_grid_attention(q, k, v, q_gain, k_gain, n_ctx, need_ctx):
    B_, T, _ = q.shape
    n = T - n_ctx
    G = A_HEADS // A_KV_HEADS
    q = rms_norm(q.reshape(B_, T, A_HEADS, HEAD_DIM), q_gain)
    k = rms_norm(k.reshape(B_, T, A_KV_HEADS, HEAD_DIM), k_gain)
    v = v.reshape(B_, T, A_KV_HEADS, HEAD_DIM)
    pos = jnp.arange(n)
    row, col = pos // GRID_W, pos % GRID_W
    q_l = rope_2d(q[:, n_ctx:], row, col)
    k_l = rope_2d(k[:, n_ctx:], row, col)
    q_c, k_c, v_c, v_l = q[:, :n_ctx], k[:, :n_ctx], v[:, :n_ctx], v[:, n_ctx:]
    keys = jnp.concatenate([k_l, k_c], axis=1)
    vals = jnp.concatenate([v_l, v_c], axis=1)
    nb = n // Q_BLOCK
    qb = q_l.reshape(B_, nb, Q_BLOCK, A_KV_HEADS, G, HEAD_DIM).swapaxes(0, 1)
    o_l = lax.map(lambda qq: gqa_attend(qq, keys, vals), qb)
    o_l = o_l.swapaxes(0, 1).reshape(B_, n, A_WIDTH)
    o_c = None
    if need_ctx:
        o_c = gqa_attend(q_c.reshape(B_, n_ctx, A_KV_HEADS, G, HEAD_DIM), k_c, v_c).reshape(B_, n_ctx, A_WIDTH)
    return o_c, o_l


def mixer_neighbourhood(q, k, v, rpb, n_ctx, need_ctx):
    B_, T, _ = q.shape
    n = T - n_ctx
    rows = n // GRID_W
    wr = min(WIN_ROWS, rows)
    heads = lambda t: t.reshape(B_, T, C_HEADS, HEAD_DIM)
    q, k, v = heads(q), heads(k), heads(v)
    q_c, k_c, v_c = q[:, :n_ctx], k[:, :n_ctx], v[:, :n_ctx]
    q_l, k_l, v_l = q[:, n_ctx:], k[:, n_ctx:], v[:, n_ctx:]
    r = jnp.arange(rows)
    col = jnp.arange(GRID_W)
    row_start = jnp.clip(r - wr // 2, 0, rows - wr)
    col_start = jnp.clip(col - WIN_COLS // 2, 0, GRID_W - WIN_COLS)
    w_idx = jnp.arange(wr * WIN_COLS)
    key_row = row_start[:, None] + (w_idx // WIN_COLS)[None, :]
    key_col = col_start[:, None] + (w_idx % WIN_COLS)[None, :]
    idx = key_row[:, None, :] * GRID_W + key_col[None, :, :]
    d_row = (key_row - r[:, None] + WIN_ROWS - 1)[:, None, :]
    d_col = (key_col - col[:, None] + WIN_COLS - 1)[None, :, :]
    bias = rpb[:, d_row, d_col].astype(jnp.float32).transpose(1, 0, 2, 3)
    q_rows = q_l.reshape(B_, rows, GRID_W, C_HEADS, HEAD_DIM).swapaxes(0, 1)
    scale = HEAD_DIM ** -0.5

    def row_block(args):
        q_r, idx_r, bias_r = args
        k_w, v_w = k_l[:, idx_r], v_l[:, idx_r]
        s_w = jnp.einsum('bqhd,bqwhd->bhqw', q_r, k_w).astype(jnp.float32) * scale + bias_r
        s_c = jnp.einsum('bqhd,bchd->bhqc', q_r, k_c).astype(jnp.float32) * scale
        p = jax.nn.softmax(jnp.concatenate([s_w, s_c], axis=-1), axis=-1).astype(v_l.dtype)
        W = idx_r.shape[-1]
        return (jnp.einsum('bhqw,bqwhd->bqhd', p[..., :W], v_w)
                + jnp.einsum('bhqc,bchd->bqhd', p[..., W:], v_c))

    o_l = lax.map(row_block, (q_rows, idx, bias)).swapaxes(0, 1).reshape(B_, n, C_WIDTH)
    o_c = None
    if need_ctx:
        o_c = gqa_attend(q_c[:, :, :, None, :], k_c, v_c).reshape(B_, n_ctx, C_WIDTH)
    return o_c, o_l


def flip_segments(t, n_ctx):
    return jnp.concatenate([t[:, :n_ctx][:, ::-1], t[:, n_ctx:][:, ::-1]], axis=1)


def wkv_scan(r, w, k, v, a, b):
    B_, _, H, N = r.shape
    xs = tuple(t.swapaxes(0, 1) for t in (r, w, k, v, a, b))

    def step(S, inp):
        r_t, w_t, k_t, v_t, a_t, b_t = inp
        sa = jnp.einsum('bhij,bhj->bhi', S, a_t)
        S = S * w_t[:, :, None, :] + sa[..., None] * b_t[:, :, None, :] + v_t[..., None] * k_t[:, :, None, :]
        return S, jnp.einsum('bhij,bhj->bhi', S, r_t)

    _, y = lax.scan(step, jnp.zeros((B_, H, N, N), jnp.float32), xs)
    return y.swapaxes(0, 1)


def mixer_rwkv7(r, k, v, w_low, a_low, g_low, w0, w_up, a0, a_up, g_up, k_k, k_a, r_k, lnx_w, lnx_b, n_ctx):
    f32 = jnp.float32
    B_, T, _ = r.shape
    heads = lambda t: t.reshape(B_, T, B_HEADS, HEAD_DIM)
    r, k, v = r.astype(f32), k.astype(f32), v.astype(f32)
    kk = heads(k * k_k.astype(f32))
    kk = kk / jnp.maximum(jnp.sqrt(jnp.sum(kk * kk, axis=-1, keepdims=True)), 1e-12)
    g = jax.nn.sigmoid(g_low.astype(f32)) @ g_up.astype(f32)
    ys, bonuses = [], []
    for d in range(2):
        wl = w_low[..., d * DECAY_LORA:(d + 1) * DECAY_LORA].astype(f32)
        al = a_low[..., d * AAA_LORA:(d + 1) * AAA_LORA].astype(f32)
        w = -jax.nn.softplus(-(w0[d].astype(f32) + jnp.tanh(wl) @ w_up[d].astype(f32))) - 0.5
        decay = jnp.exp(-jnp.exp(w))
        a = jax.nn.sigmoid(a0[d].astype(f32) + al @ a_up[d].astype(f32))
        kd = k * (1 + (a - 1) * k_a.astype(f32))
        seq = [heads(t) for t in (r, decay, kd, v)] + [-kk, kk * heads(a)]
        if d == 1:
            seq = [flip_segments(t, n_ctx) for t in seq]
        y = wkv_scan(*seq)
        if d == 1:
            y = flip_segments(y, n_ctx)
        ys.append(y)
        bonuses.append(jnp.sum(heads(r * kd * r_k.astype(f32)), axis=-1, keepdims=True) * heads(v))
    y = ys[0] + ys[1]
    mu = jnp.mean(y, axis=-1, keepdims=True)
    var = jnp.mean(jnp.square(y - mu), axis=-1, keepdims=True)
    y = ((y - mu) * lax.rsqrt(var + LNX_EPS)).reshape(B_, T, B_WIDTH) * lnx_w.astype(f32) + lnx_b.astype(f32)
    return (y + (bonuses[0] + bonuses[1]).reshape(B_, T, B_WIDTH)) * g


def ec_moe(h, w_router, w_gate, w_up, w_down):
    B_, n, _ = h.shape
    cap = CAPACITY_FACTOR * n // N_EXPERTS
    aff = jax.nn.softmax((h @ w_router).astype(jnp.float32), axis=-1)
    gate, idx = lax.top_k(aff.swapaxes(1, 2), cap)
    bidx = jnp.arange(B_)[:, None, None]
    xe = h[bidx, idx]
    hid = jax.nn.silu(jnp.einsum('becd,edf->becf', xe, w_gate)) * jnp.einsum('becd,edf->becf', xe, w_up)
    ye = jnp.einsum('becf,efd->becd', hid, w_down) * gate[..., None].astype(h.dtype)
    return jnp.zeros_like(h).at[bidx, idx].add(ye)


def setup_inputs(seed: int = 0) -> dict:
    key = jax.random.key(seed)
    ks = iter(jax.random.split(key, 40))
    nrm = lambda shape, s: jax.random.normal(next(ks), shape, jnp.float32) * s
    L, D = DEPTH, D_MODEL
    return {
        "x": nrm((BATCH, SEQ, D), 1.0),
        "c": nrm((BATCH, D), 1.0),
        "ctx": nrm((BATCH, CTX_LEN, D), 1.0),
        "c_ctx": nrm((D,), 1.0),
        "w_mod": nrm((L, D, 6 * D), 0.5 * D ** -0.5),
        "b_mod": nrm((L, 6 * D), 0.01),
        "g_pre_mix": 1.0 + nrm((L, D), 0.05),
        "g_post_mix": 1.0 + nrm((L, D), 0.05),
        "g_pre_ffn": 1.0 + nrm((L, D), 0.05),
        "g_post_ffn": 1.0 + nrm((L, D), 0.05),
        "w_in": nrm((L, D, N_IN), D ** -0.5),
        "w_out": nrm((L, D_MIX, D), D_MIX ** -0.5),
        "q_gain": 1.0 + nrm((L, HEAD_DIM), 0.05),
        "k_gain": 1.0 + nrm((L, HEAD_DIM), 0.05),
        "rpb": nrm((L, C_HEADS, 2 * WIN_ROWS - 1, 2 * WIN_COLS - 1), 0.1),
        "rk_w0": jax.random.uniform(next(ks), (L, 2, B_WIDTH), jnp.float32, -5.0, 0.0),
        "rk_w_up": nrm((L, 2, DECAY_LORA, B_WIDTH), 0.5 * DECAY_LORA ** -0.5),
        "rk_a0": nrm((L, 2, B_WIDTH), 0.1),
        "rk_a_up": nrm((L, 2, AAA_LORA, B_WIDTH), 0.5 * AAA_LORA ** -0.5),
        "rk_g_up": nrm((L, GATE_LORA, B_WIDTH), GATE_LORA ** -0.5),
        "rk_k_k": 0.85 + nrm((L, B_WIDTH), 0.05),
        "rk_k_a": 1.0 + nrm((L, B_WIDTH), 0.05),
        "rk_r_k": nrm((L, B_WIDTH), 0.1),
        "rk_lnx_w": 1.0 + nrm((L, B_WIDTH), 0.05),
        "rk_lnx_b": nrm((L, B_WIDTH), 0.01),
        "w_router": nrm((L, D, N_EXPERTS), D ** -0.5),
        "w_e_gate": nrm((L, N_EXPERTS, D, D_EXPERT), D ** -0.5),
        "w_e_up": nrm((L, N_EXPERTS, D, D_EXPERT), D ** -0.5),
        "w_e_down": nrm((L, N_EXPERTS, D_EXPERT, D), D_EXPERT ** -0.5),
    }


def reference(x, c, ctx, c_ctx, w_mod, b_mod, g_pre_mix, g_post_mix, g_pre_ffn, g_post_ffn,
              w_in, w_out, q_gain, k_gain, rpb, rk_w0, rk_w_up, rk_a0, rk_a_up, rk_g_up,
              rk_k_k, rk_k_a, rk_r_k, rk_lnx_w, rk_lnx_b, w_router, w_e_gate, w_e_up, w_e_down):
    x_l, x_c = x, ctx
    n_ctx = ctx.shape[1]
    in_splits = np.cumsum(IN_SIZES)[:-1].tolist()
    s_l, s_c = jax.nn.silu(c), jax.nn.silu(c_ctx)
    for i in range(DEPTH):
        need_ctx = i < DEPTH - 1
        m_l = jnp.split((s_l @ w_mod[i] + b_mod[i])[:, None, :], 6, axis=-1)
        m_c = jnp.split(s_c @ w_mod[i] + b_mod[i], 6, axis=-1)
        h = jnp.concatenate([modulate(rms_norm(x_c, g_pre_mix[i]), m_c[0], m_c[1]),
                             modulate(rms_norm(x_l, g_pre_mix[i]), m_l[0], m_l[1])], axis=1)
        proj = h @ w_in[i]
        (qa, ka, va, rb, kb, vb, wlb, alb, glb, qn, kn, vn) = jnp.split(proj, in_splits, axis=-1)
        oa_c, oa_l = mixer_grid_attention(qa, ka, va, q_gain[i], k_gain[i], n_ctx, need_ctx)
        ob = mixer_rwkv7(rb, kb, vb, wlb, alb, glb, rk_w0[i], rk_w_up[i], rk_a0[i], rk_a_up[i],
                         rk_g_up[i], rk_k_k[i], rk_k_a[i], rk_r_k[i], rk_lnx_w[i], rk_lnx_b[i],
                         n_ctx).astype(x.dtype)
        on_c, on_l = mixer_neighbourhood(qn, kn, vn, rpb[i], n_ctx, need_ctx)
        y_l = jnp.concatenate([oa_l, ob[:, n_ctx:], on_l], axis=-1) @ w_out[i]
        x_l = x_l + m_l[2] * rms_norm(y_l, g_post_mix[i])
        if need_ctx:
            y_c = jnp.concatenate([oa_c, ob[:, :n_ctx], on_c], axis=-1) @ w_out[i]
            x_c = x_c + m_c[2] * rms_norm(y_c, g_post_mix[i])
        h_l = modulate(rms_norm(x_l, g_pre_ffn[i]), m_l[3], m_l[4])
        f_l = ec_moe(h_l, w_router[i], w_e_gate[i], w_e_up[i], w_e_down[i])
        x_l = x_l + m_l[5] * rms_norm(f_l, g_post_ffn[i])
        if need_ctx:
            h_c = modulate(rms_norm(x_c, g_pre_ffn[i]), m_c[3], m_c[4])
            f_c = ec_moe(h_c, w_router[i], w_e_gate[i], w_e_up[i], w_e_down[i])
            x_c = x_c + m_c[5] * rms_norm(f_c, g_post_ffn[i])
    return x_l
```

```python
import functools

import numpy as np
import jax
import jax.numpy as jnp
from jax import lax
from jax.experimental import pallas as pl
from jax.experimental.pallas import tpu as pltpu

F32 = jnp.float32
BF16 = jnp.bfloat16

HEAD_DIM = 64
GRID_W = 64
A_HEADS = 8
A_KV_HEADS = 2
B_HEADS = 4
C_HEADS = 4
A_WIDTH = A_HEADS * HEAD_DIM
A_KV_WIDTH = A_KV_HEADS * HEAD_DIM
B_WIDTH = B_HEADS * HEAD_DIM
C_WIDTH = C_HEADS * HEAD_DIM
DECAY_LORA = 64
AAA_LORA = 64
GATE_LORA = 128
RW_WIDTH = 3 * B_WIDTH + 2 * DECAY_LORA + 2 * AAA_LORA + GATE_LORA
ROPE_BASE = 10000.0
ROPE_HALF = HEAD_DIM // 4
WIN_ROWS = 8
WIN_COLS = 16
N_EXPERTS = 16
CAPACITY_FACTOR = 2
NORM_EPS = 1e-6
LNX_EPS = 64e-5
ATTN_SCALE = HEAD_DIM ** -0.5
MASK_VALUE = -1e30

LANES = 128
ROW_TILE = 256
CHUNK = 64
MOD_ROWS = 16
VMEM_LIMIT = 56 * 1024 * 1024


def _cparams(*sem):
    return pltpu.CompilerParams(dimension_semantics=sem, vmem_limit_bytes=VMEM_LIMIT)


def _dot(a, b):
    return jnp.dot(a, b, preferred_element_type=F32)


def _dot_nt(a, b):
    return lax.dot_general(a, b, (((1,), (1,)), ((), ())), preferred_element_type=F32)


def _dot_tn(a, b):
    return lax.dot_general(a, b, (((0,), (0,)), ((), ())), preferred_element_type=F32)


def _split2(x):
    hi = x.astype(BF16)
    lo = (x - hi.astype(F32)).astype(BF16)
    return hi, lo


def _split3(x):
    hi = x.astype(BF16)
    r1 = x - hi.astype(F32)
    mid = r1.astype(BF16)
    lo = (r1 - mid.astype(F32)).astype(BF16)
    return hi, mid, lo


def _head_sum(x, bd):
    outs = []
    for c in range(x.shape[1] // LANES):
        hi, lo = _split2(x[:, c * LANES:(c + 1) * LANES])
        outs.append(_dot(hi, bd) + _dot(lo, bd))
    return outs[0] if len(outs) == 1 else jnp.concatenate(outs, axis=1)


def _tile_lanes(t, width):
    reps = width // t.shape[1]
    return t if reps == 1 else jnp.concatenate([t] * reps, axis=1)


def _rms(x, g):
    return x * lax.rsqrt(jnp.mean(x * x, axis=-1, keepdims=True) + NORM_EPS) * g


def _sigmoid(x):
    return 1.0 / (1.0 + jnp.exp(-x))


def _mod_kernel(s_ref, w_ref, b_ref, o_ref):
    s = s_ref[...]
    s = s * _sigmoid(s)
    o_ref[...] = _dot(s, w_ref[...]) + b_ref[...]


def _modulation(cond, w_mod, b_mod):
    L, D, N = w_mod.shape
    tn = 512
    return pl.pallas_call(
        _mod_kernel,
        grid=(L, N // tn),
        in_specs=[pl.BlockSpec((MOD_ROWS, D), lambda l, j: (0, 0)),
                  pl.BlockSpec((None, D, tn), lambda l, j: (l, 0, j)),
                  pl.BlockSpec((None, 1, tn), lambda l, j: (l, 0, j))],
        out_specs=pl.BlockSpec((None, MOD_ROWS, tn), lambda l, j: (l, 0, j)),
        out_shape=jax.ShapeDtypeStruct((L, MOD_ROWS, N), F32),
        compiler_params=_cparams("parallel", "parallel"),
        name="modulation",
    )(cond, w_mod, b_mod.reshape(L, 1, N))


def _rope(x, cos, sin):
    w = x.shape[1]
    lane = lax.broadcasted_iota(jnp.int32, x.shape, 1)
    upper = (lane // ROPE_HALF) % 2 == 1
    partner = jnp.where(upper, pltpu.roll(x, ROPE_HALF, 1), pltpu.roll(x, w - ROPE_HALF, 1))
    return x * _tile_lanes(cos, w) + partner * _tile_lanes(sin, w)


def _inproj_kernel(x_ref, g_ref, shift_ref, scale_ref, wqa_ref, wka_ref, wva_ref, wrw_ref, wn_ref,
                   qg_ref, kg_ref, cos_ref, sin_ref, bd_ref,
                   qa_ref, ka_ref, va_ref, rw_ref, qn_ref, kn_ref, vn_ref):
    h = _rms(x_ref[...], g_ref[...]) * (1.0 + scale_ref[...]) + shift_ref[...]
    hb = h.astype(BF16)
    bd = bd_ref[...]
    cos, sin = cos_ref[...], sin_ref[...]

    def normed(w_ref, gain_ref):
        y = _dot(hb, w_ref[...])
        ms = _head_sum(y * y, bd) * (1.0 / HEAD_DIM)
        return _rope(y * lax.rsqrt(ms + NORM_EPS) * gain_ref[...], cos, sin)

    def put_heads(o_ref, y):
        for hd in range(y.shape[1] // HEAD_DIM):
            o_ref[hd] = y[:, hd * HEAD_DIM:(hd + 1) * HEAD_DIM].astype(o_ref.dtype)

    put_heads(qa_ref, normed(wqa_ref, qg_ref))
    put_heads(ka_ref, normed(wka_ref, kg_ref))
    put_heads(va_ref, _dot(hb, wva_ref[...]))
    rw_ref[...] = _dot(hb, wrw_ref[...])
    yn = _dot(hb, wn_ref[...])
    put_heads(qn_ref, yn[:, :C_WIDTH])
    put_heads(kn_ref, yn[:, C_WIDTH:2 * C_WIDTH])
    put_heads(vn_ref, yn[:, 2 * C_WIDTH:])


def _in_projection(xs, mod4, g_pre, w_in, q_gain, k_gain, cos, sin, bd, n_lat_tiles):
    B, T, D = xs.shape
    tm = ROW_TILE
    o = np.cumsum([0, A_WIDTH, A_KV_WIDTH, A_KV_WIDTH, RW_WIDTH, 3 * C_WIDTH])
    wb = w_in.astype(BF16)
    ws = [wb[:, o[i]:o[i + 1]] for i in range(5)]
    const = lambda shape: pl.BlockSpec(shape, lambda b, t: (0,) * len(shape))
    mod_spec = lambda k: pl.BlockSpec(
        (None, None, 1, D), lambda b, t: (jnp.where(t >= n_lat_tiles, B, b), k, 0, 0))
    heads = lambda nh: pl.BlockSpec((None, nh, tm, HEAD_DIM), lambda b, t: (b, 0, t, 0))
    hshape = lambda nh: jax.ShapeDtypeStruct((B, nh, T, HEAD_DIM), BF16)
    return pl.pallas_call(
        _inproj_kernel,
        grid=(B, T // tm),
        in_specs=[pl.BlockSpec((None, tm, D), lambda b, t: (b, t, 0)),
                  const((1, D)), mod_spec(0), mod_spec(1)]
                 + [const(w.shape) for w in ws]
                 + [const((1, A_WIDTH)), const((1, A_KV_WIDTH)),
                    pl.BlockSpec((tm, LANES), lambda b, t: (t, 0)),
                    pl.BlockSpec((tm, LANES), lambda b, t: (t, 0)),
                    const((LANES, LANES))],
        out_specs=[heads(A_HEADS), heads(A_KV_HEADS), heads(A_KV_HEADS),
                   pl.BlockSpec((None, tm, RW_WIDTH), lambda b, t: (b, t, 0)),
                   heads(C_HEADS), heads(C_HEADS), heads(C_HEADS)],
        out_shape=[hshape(A_HEADS), hshape(A_KV_HEADS), hshape(A_KV_HEADS),
                   jax.ShapeDtypeStruct((B, T, RW_WIDTH), F32),
                   hshape(C_HEADS), hshape(C_HEADS), hshape(C_HEADS)],
        compiler_params=_cparams("parallel", "parallel"),
        name="in_projection",
    )(xs, g_pre.reshape(1, D), mod4, mod4, *ws,
      jnp.tile(q_gain, A_HEADS).reshape(1, A_WIDTH), jnp.tile(k_gain, A_KV_HEADS).reshape(1, A_KV_WIDTH),
      cos, sin, bd)


def _gqa_kernel(q_ref, k_ref, v_ref, o_ref, *, n_lat, n_lat_tiles):
    group = q_ref.shape[0]

    def attend(k, v):
        for hd in range(group):
            s = _dot_nt(q_ref[hd], k) * ATTN_SCALE
            p = jnp.exp(s - jnp.max(s, axis=-1, keepdims=True))
            den = jnp.sum(p, axis=-1, keepdims=True)
            o = _dot(p.astype(BF16), v) / den
            o_ref[:, hd * HEAD_DIM:(hd + 1) * HEAD_DIM] = o.astype(o_ref.dtype)

    @pl.when(pl.program_id(2) < n_lat_tiles)
    def _():
        attend(k_ref[...], v_ref[...])

    @pl.when(pl.program_id(2) >= n_lat_tiles)
    def _():
        attend(k_ref[n_lat:, :], v_ref[n_lat:, :])


def _gqa_attention(q, k, v, n_lat, n_rows):
    B, _, T, _ = q.shape
    tq = ROW_TILE
    group = A_HEADS // A_KV_HEADS
    kv_spec = pl.BlockSpec((None, None, T, HEAD_DIM), lambda b, g, t: (b, g, 0, 0))
    return pl.pallas_call(
        functools.partial(_gqa_kernel, n_lat=n_lat, n_lat_tiles=n_lat // tq),
        grid=(B, A_KV_HEADS, n_rows // tq),
        in_specs=[pl.BlockSpec((None, group, tq, HEAD_DIM), lambda b, g, t: (b, g, t, 0)),
                  kv_spec, kv_spec],
        out_specs=pl.BlockSpec((None, tq, group * HEAD_DIM), lambda b, g, t: (b, t, g)),
        out_shape=jax.ShapeDtypeStruct((B, n_rows, A_WIDTH), BF16),
        compiler_params=_cparams("parallel", "parallel", "parallel"),
        name="gqa_attention",
    )(q, k, v)


def _natten_kernel(q_ref, k_ref, v_ref, bias_ref, o_ref, *, n_lat, rows, need_ctx):
    win = WIN_ROWS * GRID_W
    for hd in range(C_HEADS):
        kc = k_ref[hd, n_lat:, :]
        vc = v_ref[hd, n_lat:, :]

        def row_block(r, carry, hd=hd, kc=kc, vc=vc):
            rs = jnp.clip(r - WIN_ROWS // 2, 0, rows - WIN_ROWS)
            q = q_ref[hd, pl.ds(pl.multiple_of(r * GRID_W, GRID_W), GRID_W), :]
            kw = k_ref[hd, pl.ds(pl.multiple_of(rs * GRID_W, GRID_W), win), :]
            vw = v_ref[hd, pl.ds(pl.multiple_of(rs * GRID_W, GRID_W), win), :]
            s_w = _dot_nt(q, kw) * ATTN_SCALE + bias_ref[r - rs, hd]
            s_c = _dot_nt(q, kc) * ATTN_SCALE
            m = jnp.maximum(jnp.max(s_w, axis=-1, keepdims=True), jnp.max(s_c, axis=-1, keepdims=True))
            p_w = jnp.exp(s_w - m)
            p_c = jnp.exp(s_c - m)
            den = jnp.sum(p_w, axis=-1, keepdims=True) + jnp.sum(p_c, axis=-1, keepdims=True)
            o = (_dot(p_w.astype(BF16), vw) + _dot(p_c.astype(BF16), vc)) / den
            o_ref[pl.ds(pl.multiple_of(r * GRID_W, GRID_W), GRID_W),
                  hd * HEAD_DIM:(hd + 1) * HEAD_DIM] = o.astype(o_ref.dtype)
            return carry

        lax.fori_loop(0, rows, row_block, 0)
        if need_ctx:
            s = _dot_nt(q_ref[hd, n_lat:, :], kc) * ATTN_SCALE
            p = jnp.exp(s - jnp.max(s, axis=-1, keepdims=True))
            o = _dot(p.astype(BF16), vc) / jnp.sum(p, axis=-1, keepdims=True)
            o_ref[n_lat:, hd * HEAD_DIM:(hd + 1) * HEAD_DIM] = o.astype(o_ref.dtype)


def _natten_bias(rpb, rows):
    off = np.arange(WIN_ROWS)[:, None, None, None]
    qc = np.arange(GRID_W)[None, :, None, None]
    jr = np.arange(WIN_ROWS)[None, None, :, None]
    kc = np.arange(GRID_W)[None, None, None, :]
    col_start = np.clip(qc - WIN_COLS // 2, 0, GRID_W - WIN_COLS)
    valid = (kc >= col_start) & (kc < col_start + WIN_COLS)
    d_row = np.broadcast_to(jr - off + WIN_ROWS - 1, (WIN_ROWS, GRID_W, WIN_ROWS, GRID_W))
    d_col = np.clip(np.broadcast_to(kc - qc + WIN_COLS - 1, d_row.shape), 0, 2 * WIN_COLS - 2)
    valid = np.broadcast_to(valid, d_row.shape) & (d_row >= 0) & (d_row <= 2 * WIN_ROWS - 2)
    d_row = np.clip(d_row, 0, 2 * WIN_ROWS - 2)
    bias = rpb[:, d_row, d_col].astype(F32)
    bias = jnp.where(valid[None], bias, MASK_VALUE)
    return bias.transpose(1, 0, 2, 3, 4).reshape(WIN_ROWS, C_HEADS, GRID_W, WIN_ROWS * GRID_W)


def _neighbourhood_attention(q, k, v, bias, n_lat, need_ctx):
    B, _, T, _ = q.shape
    n_rows = T if need_ctx else n_lat
    full = pl.BlockSpec((None, C_HEADS, T, HEAD_DIM), lambda b: (b, 0, 0, 0))
    return pl.pallas_call(
        functools.partial(_natten_kernel, n_lat=n_lat, rows=n_lat // GRID_W, need_ctx=need_ctx),
        grid=(B,),
        in_specs=[full, full, full, pl.BlockSpec(bias.shape, lambda b: (0, 0, 0, 0))],
        out_specs=pl.BlockSpec((None, n_rows, C_WIDTH), lambda b: (b, 0, 0)),
        out_shape=jax.ShapeDtypeStruct((B, n_rows, C_WIDTH), BF16),
        compiler_params=_cparams("parallel"),
        name="neighbourhood_attention",
    )(q, k, v, bias)


def _rwkv_chunk_kernel(rw_ref, w0_ref, wup_ref, a0_ref, aup_ref, gup_ref, kk_ref, ka_ref, rk_ref, bd_ref,
                       rhat_ref, yhat_ref, gt_ref, ht_ref, gate_ref, bonus_ref):
    C = CHUNK
    W = B_WIDTH
    rw = rw_ref[...]
    r, k, v = rw[:, :W], rw[:, W:2 * W], rw[:, 2 * W:3 * W]
    wl = rw[:, 3 * W:3 * W + 2 * DECAY_LORA]
    al = rw[:, 3 * W + 2 * DECAY_LORA:3 * W + 2 * DECAY_LORA + 2 * AAA_LORA]
    gl = rw[:, 3 * W + 2 * DECAY_LORA + 2 * AAA_LORA:]
    bd = bd_ref[...]

    kkv = k * kk_ref[...]
    kkv = kkv / jnp.maximum(jnp.sqrt(_head_sum(kkv * kkv, bd)), 1e-12)
    gate_ref[...] = _dot(_sigmoid(gl), gup_ref[...])

    row = lax.broadcasted_iota(jnp.int32, (C, C), 0)
    col = lax.broadcasted_iota(jnp.int32, (C, C), 1)
    eye = row == col
    kd_sum = jnp.zeros_like(k)
    for d in range(2):
        incl = (row >= col) if d == 0 else (row <= col)
        strict = (row > col) if d == 0 else (row < col)
        w = w0_ref[d] + _dot(jnp.tanh(wl[:, d * DECAY_LORA:(d + 1) * DECAY_LORA]), wup_ref[d])
        w = -(jnp.maximum(-w, 0.0) + jnp.log(1.0 + jnp.exp(-jnp.abs(w)))) - 0.5
        lw = -jnp.exp(w)
        asig = _sigmoid(a0_ref[d] + _dot(al[:, d * AAA_LORA:(d + 1) * AAA_LORA], aup_ref[d]))
        kd = k * (1.0 + (asig - 1.0) * ka_ref[...])
        kd_sum = kd_sum + kd
        b = kkv * asig
        a = -kkv

        tri = incl.astype(BF16)
        lw3 = _split3(lw)
        cum_incl = _dot(tri, lw3[0]) + _dot(tri, lw3[1]) + _dot(tri, lw3[2])
        cum_excl = cum_incl - lw
        tot = cum_incl[C - 1:C, :] if d == 0 else cum_incl[0:1, :]
        rho = 0.5 * tot
        e_in = jnp.exp(rho - cum_incl)
        e_end = jnp.exp(tot - cum_incl)
        e_rho = jnp.exp(rho)
        a_rho = a * jnp.exp(cum_excl - rho)
        r_rho = r * jnp.exp(cum_incl - rho)
        b_in, k_in = b * e_in, kd * e_in
        b_end, k_end = b * e_end, kd * e_end
        g_end = jnp.exp(tot)

        for hd in range(B_HEADS):
            sl = slice(hd * HEAD_DIM, (hd + 1) * HEAD_DIM)
            vh = v[:, sl].astype(BF16)
            x = jnp.concatenate([a_rho[:, sl], r_rho[:, sl]], axis=0).astype(BF16)
            y = jnp.concatenate([b_in[:, sl], k_in[:, sl]], axis=0).astype(BF16)
            m = _dot_nt(x, y)
            l_ab = jnp.where(strict, m[:C, :C], 0.0)
            l_ak = jnp.where(strict, m[:C, C:], 0.0)
            a_rb = jnp.where(incl, m[C:, :C], 0.0).astype(BF16)
            a_rk = jnp.where(incl, m[C:, C:], 0.0).astype(BF16)
            tinv = jnp.where(eye, 1.0, l_ab)
            lp = l_ab
            for _ in range(int(np.log2(C)) - 1):
                lpb = lp.astype(BF16)
                lp = _dot(lpb, lpb)
                tinv = tinv + _dot(tinv.astype(BF16), lp.astype(BF16))
            w1 = _dot(l_ak.astype(BF16), vh)
            z = _dot(tinv.astype(BF16), jnp.concatenate([a_rho[:, sl], w1], axis=1).astype(BF16))
            ahat_rho, vhat = z[:, :HEAD_DIM], z[:, HEAD_DIM:]
            zb = z.astype(BF16)
            rz = _dot(a_rb, zb)
            rhat = (r_rho[:, sl] + rz[:, :HEAD_DIM]) * e_rho[:, sl]
            yhat = rz[:, HEAD_DIM:] + _dot(a_rk, vh)
            ahat = ahat_rho * e_rho[:, sl]
            av = jnp.concatenate([ahat, vhat], axis=1).astype(BF16)
            gh = _dot_tn(b_end[:, sl].astype(BF16), av)
            gt = gh[:, :HEAD_DIM] + jnp.where(eye, g_end[:, sl], 0.0)
            ht = gh[:, HEAD_DIM:] + _dot_tn(k_end[:, sl].astype(BF16), vh)
            rhat_ref[d, :, sl] = rhat
            yhat_ref[d, :, sl] = yhat
            gt_ref[d, :, sl] = gt
            ht_ref[d, :, sl] = ht
    bonus_ref[...] = _head_sum(r * kd_sum * rk_ref[...], bd) * v


def _rwkv_chunks(rw, p, bd):
    B, T, _ = rw.shape
    C, W = CHUNK, B_WIDTH
    nc = T // C
    const = lambda shape: pl.BlockSpec(shape, lambda b, c: (0,) * len(shape))
    rows = pl.BlockSpec((None, 2, C, W), lambda b, c: (b, 0, c, 0))
    mats = pl.BlockSpec((None, 2, None, C, W), lambda b, c: (b, 0, c, 0, 0))
    flat = pl.BlockSpec((None, C, W), lambda b, c: (b, c, 0))
    vec = lambda a: a.reshape(1, W)
    return pl.pallas_call(
        _rwkv_chunk_kernel,
        grid=(B, nc),
        in_specs=[pl.BlockSpec((None, C, RW_WIDTH), lambda b, c: (b, c, 0)),
                  const((2, 1, W)), const((2, DECAY_LORA, W)), const((2, 1, W)), const((2, AAA_LORA, W)),
                  const((GATE_LORA, W)), const((1, W)), const((1, W)), const((1, W)), const((LANES, LANES))],
        out_specs=[rows, rows, mats, mats, flat, flat],
        out_shape=[jax.ShapeDtypeStruct((B, 2, T, W), F32), jax.ShapeDtypeStruct((B, 2, T, W), F32),
                   jax.ShapeDtypeStruct((B, 2, nc, C, W), F32), jax.ShapeDtypeStruct((B, 2, nc, C, W), F32),
                   jax.ShapeDtypeStruct((B, T, W), F32), jax.ShapeDtypeStruct((B, T, W), F32)],
        compiler_params=_cparams("parallel", "parallel"),
        name="rwkv_chunks",
    )(rw, p["w0"].reshape(2, 1, W), p["w_up"], p["a0"].reshape(2, 1, W), p["a_up"], p["g_up"],
      vec(p["k_k"]), vec(p["k_a"]), vec(p["r_k"]), bd)


def _rwkv_scan_kernel(rhat_ref, yhat_ref, gt_ref, ht_ref, y_ref, st_ref):
    @pl.when(pl.program_id(1) == 0)
    def _():
        st_ref[...] = jnp.zeros_like(st_ref)

    for b in range(st_ref.shape[0]):
        for hd in range(B_HEADS):
            sl = slice(hd * HEAD_DIM, (hd + 1) * HEAD_DIM)
            st = st_ref[b, :, sl]
            stb = st.astype(BF16)
            y_ref[b, :, sl] = _dot(rhat_ref[b, :, sl].astype(BF16), stb) + yhat_ref[b, :, sl]
            st_ref[b, :, sl] = _dot(gt_ref[b, :, sl].astype(BF16), stb) + ht_ref[b, :, sl]


def _rwkv_scan(rhat, yhat, gt, ht, n_lat_chunks):
    B, _, T, W = rhat.shape
    C = CHUNK
    nc = T // C
    chunk = lambda d, s: jnp.where(d == 0, (s + n_lat_chunks) % nc, nc - 1 - s)
    rows = pl.BlockSpec((B, None, C, W), lambda d, s: (0, d, chunk(d, s), 0))
    mats = pl.BlockSpec((B, None, None, C, W), lambda d, s: (0, d, chunk(d, s), 0, 0))
    return pl.pallas_call(
        _rwkv_scan_kernel,
        grid=(2, nc),
        in_specs=[rows, rows, mats, mats],
        out_specs=rows,
        out_shape=jax.ShapeDtypeStruct((B, 2, T, W), F32),
        scratch_shapes=[pltpu.VMEM((B, HEAD_DIM, W), F32)],
        compiler_params=_cparams("arbitrary", "arbitrary"),
        name="rwkv_scan",
    )(rhat, yhat, gt, ht)


def _outproj_kernel(x_ref, oa_ref, y_ref, bonus_ref, gate_ref, on_ref, lnw_ref, lnb_ref, bd_ref,
                    wa_ref, wb_ref, wn_ref, g_ref, mgate_ref, o_ref):
    y = y_ref[0] + y_ref[1]
    bd = bd_ref[...]
    mu = _head_sum(y, bd) * (1.0 / HEAD_DIM)
    yc = y - mu
    var = _head_sum(yc * yc, bd) * (1.0 / HEAD_DIM)
    ob = (yc * lax.rsqrt(var + LNX_EPS) * lnw_ref[...] + lnb_ref[...] + bonus_ref[...]) * gate_ref[...]
    out = _dot(oa_ref[...], wa_ref[...]) + _dot(ob.astype(BF16), wb_ref[...]) + _dot(on_ref[...], wn_ref[...])
    o_ref[...] = x_ref[...] + mgate_ref[...] * _rms(out, g_ref[...])


def _out_projection(xs, oa, y, bonus, gate, on, p, bd, w_out, g_post, mod4, n_lat_tiles, n_rows):
    B, T, D = xs.shape
    tm = ROW_TILE
    W = B_WIDTH
    wb = w_out.astype(BF16)
    const = lambda shape: pl.BlockSpec(shape, lambda b, t: (0,) * len(shape))
    rows = lambda width: pl.BlockSpec((None, tm, width), lambda b, t: (b, t, 0))
    return pl.pallas_call(
        _outproj_kernel,
        grid=(B, n_rows // tm),
        in_specs=[rows(D), rows(A_WIDTH), pl.BlockSpec((None, 2, tm, W), lambda b, t: (b, 0, t, 0)),
                  rows(W), rows(W), rows(C_WIDTH), const((1, W)), const((1, W)), const((LANES, LANES)),
                  const((A_WIDTH, D)), const((W, D)), const((C_WIDTH, D)), const((1, D)),
                  pl.BlockSpec((None, None, 1, D), lambda b, t: (jnp.where(t >= n_lat_tiles, B, b), 2, 0, 0))],
        out_specs=rows(D),
        out_shape=jax.ShapeDtypeStruct((B, n_rows, D), F32),
        compiler_params=_cparams("parallel", "parallel"),
        name="out_projection",
    )(xs, oa, y, bonus, gate, on, p["lnx_w"].reshape(1, W), p["lnx_b"].reshape(1, W), bd,
      wb[:A_WIDTH], wb[A_WIDTH:A_WIDTH + W], wb[A_WIDTH + W:], g_post.reshape(1, D), mod4)


def _router_kernel(x_ref, g_ref, shift_ref, scale_ref, wr_ref, h_ref, aff_ref):
    h = _rms(x_ref[...], g_ref[...]) * (1.0 + scale_ref[...]) + shift_ref[...]
    h_ref[...] = h.astype(h_ref.dtype)
    logits = _dot_nt(wr_ref[...], h)
    p = jnp.exp(logits - jnp.max(logits, axis=0, keepdims=True))
    aff_ref[...] = p / jnp.sum(p, axis=0, keepdims=True)


def _router(xs, g_pre, mod4, w_router, n_lat_tiles):
    B, R, D = xs.shape
    tm = ROW_TILE
    E = w_router.shape[1]
    mod_spec = lambda k: pl.BlockSpec(
        (None, None, 1, D), lambda b, t: (jnp.where(t >= n_lat_tiles, B, b), k, 0, 0))
    return pl.pallas_call(
        _router_kernel,
        grid=(B, R // tm),
        in_specs=[pl.BlockSpec((None, tm, D), lambda b, t: (b, t, 0)),
                  pl.BlockSpec((1, D), lambda b, t: (0, 0)), mod_spec(3), mod_spec(4),
                  pl.BlockSpec((E, D), lambda b, t: (0, 0))],
        out_specs=[pl.BlockSpec((None, tm, D), lambda b, t: (b, t, 0)),
                   pl.BlockSpec((None, E, tm), lambda b, t: (b, 0, t))],
        out_shape=[jax.ShapeDtypeStruct((B, R, D), BF16), jax.ShapeDtypeStruct((B, E, R), F32)],
        compiler_params=_cparams("parallel", "parallel"),
        name="moe_router",
    )(xs, g_pre.reshape(1, D), mod4, mod4, w_router.T)


def _rank_kernel(arow_ref, acol_ref, rank_ref):
    n = arow_ref.shape[-1]
    e = pl.program_id(1)
    a_row = arow_ref[...]
    lane = lax.broadcasted_iota(jnp.int32, acol_ref.shape, 1)
    a_col = jnp.sum(jnp.where(lane == e, acol_ref[...], 0.0), axis=1, keepdims=True)
    tj = min(n, 256)
    ones = jnp.ones((8, tj), BF16)
    i_idx = lax.broadcasted_iota(jnp.int32, (tj, n), 1)
    j_loc = lax.broadcasted_iota(jnp.int32, (tj, n), 0)
    cnt = jnp.zeros((8, n), F32)
    for jc in range(n // tj):
        aj = a_col[jc * tj:(jc + 1) * tj, :]
        before = (aj > a_row) | ((aj == a_row) & (j_loc + jc * tj < i_idx))
        cnt = cnt + _dot(ones, before.astype(F32).astype(BF16))
    rank_ref[...] = cnt[0:1, :]


def _ranks(aff_t, aff_c):
    B, E, n = aff_t.shape
    return pl.pallas_call(
        _rank_kernel,
        grid=(B, E),
        in_specs=[pl.BlockSpec((None, None, 1, n), lambda b, e: (b, e, 0, 0)),
                  pl.BlockSpec((None, n, E), lambda b, e: (b, 0, 0))],
        out_specs=pl.BlockSpec((None, None, 1, n), lambda b, e: (b, e, 0, 0)),
        out_shape=jax.ShapeDtypeStruct((B, E, 1, n), F32),
        compiler_params=_cparams("parallel", "parallel"),
        name="moe_ranks",
    )(aff_t.reshape(B, E, 1, n), aff_c)


def _expert_kernel(h_ref, rrow_ref, rcol_ref, acol_ref, wg_ref, wu_ref, wd_ref, f_ref, *, cap):
    n = h_ref.shape[0]
    e = pl.program_id(1)
    slot = lax.broadcasted_iota(jnp.int32, (cap, n), 0).astype(F32)
    pick = (rrow_ref[...] == slot).astype(F32).astype(BF16)
    xe = _dot(pick, h_ref[...]).astype(BF16)
    gte = _dot(xe, wg_ref[...])
    hid = (gte * _sigmoid(gte) * _dot(xe, wu_ref[...])).astype(BF16)
    ye = _dot(hid, wd_ref[...]).astype(BF16)
    lane = lax.broadcasted_iota(jnp.int32, rcol_ref.shape, 1)
    r_col = jnp.sum(jnp.where(lane == e, rcol_ref[...], 0.0), axis=1, keepdims=True)
    a_col = jnp.sum(jnp.where(lane == e, acol_ref[...], 0.0), axis=1, keepdims=True)
    tn = min(n, 512)
    slot_l = lax.broadcasted_iota(jnp.int32, (tn, cap), 1).astype(F32)
    for c in range(n // tn):
        rows = slice(c * tn, (c + 1) * tn)
        put = jnp.where(r_col[rows] == slot_l, a_col[rows], 0.0).astype(BF16)
        upd = _dot(put, ye)

        @pl.when(e == 0)
        def _():
            f_ref[rows, :] = upd

        @pl.when(e > 0)
        def _():
            f_ref[rows, :] += upd


def _experts(h, rank_t, rank_c, aff_c, wg, wu, wd, row0_blocks):
    B, E, _, n = rank_t.shape
    D = h.shape[-1]
    F = wg.shape[-1]
    cap = CAPACITY_FACTOR * n // E
    col = pl.BlockSpec((None, n, E), lambda b, e: (b, 0, 0))
    return pl.pallas_call(
        functools.partial(_expert_kernel, cap=cap),
        grid=(B, E),
        in_specs=[pl.BlockSpec((None, n, D), lambda b, e: (b, row0_blocks, 0)),
                  pl.BlockSpec((None, None, 1, n), lambda b, e: (b, e, 0, 0)), col, col,
                  pl.BlockSpec((None, D, F), lambda b, e: (e, 0, 0)),
                  pl.BlockSpec((None, D, F), lambda b, e: (e, 0, 0)),
                  pl.BlockSpec((None, F, D), lambda b, e: (e, 0, 0))],
        out_specs=pl.BlockSpec((None, n, D), lambda b, e: (b, 0, 0)),
        out_shape=jax.ShapeDtypeStruct((B, n, D), F32),
        compiler_params=_cparams("parallel", "arbitrary"),
        name="moe_experts",
    )(h, rank_t, rank_c, aff_c, wg, wu, wd)


def _ffn_residual_kernel(x_ref, f_ref, g_ref, mgate_ref, o_ref):
    o_ref[...] = x_ref[...] + mgate_ref[...] * _rms(f_ref[...], g_ref[...])


def _ffn_residual(xs, f, g_post, mod4, n_lat_tiles):
    B, R, D = xs.shape
    tm = ROW_TILE
    rows = pl.BlockSpec((None, tm, D), lambda b, t: (b, t, 0))
    return pl.pallas_call(
        _ffn_residual_kernel,
        grid=(B, R // tm),
        in_specs=[rows, rows, pl.BlockSpec((1, D), lambda b, t: (0, 0)),
                  pl.BlockSpec((None, None, 1, D), lambda b, t: (jnp.where(t >= n_lat_tiles, B, b), 5, 0, 0))],
        out_specs=rows,
        out_shape=jax.ShapeDtypeStruct((B, R, D), F32),
        compiler_params=_cparams("parallel", "parallel"),
        name="ffn_residual",
    )(xs, f, g_post.reshape(1, D), mod4)


def _moe_set(h, aff_t, wg, wu, wd, row0, n):
    a_t = aff_t[:, :, row0:row0 + n]
    a_c = jnp.swapaxes(a_t, 1, 2)
    rank_t = _ranks(a_t, a_c)
    rank_c = jnp.swapaxes(rank_t[:, :, 0, :], 1, 2)
    return _experts(h, rank_t, rank_c, a_c, wg, wu, wd, row0 // n)


def _rope_tables(n_lat, n_ctx):
    inv = ROPE_BASE ** (-jnp.arange(ROPE_HALF, dtype=F32) / ROPE_HALF)
    pos = jnp.arange(n_lat)
    ang_r = (pos // GRID_W).astype(F32)[:, None] * inv[None, :]
    ang_c = (pos % GRID_W).astype(F32)[:, None] * inv[None, :]
    cos = jnp.concatenate([jnp.cos(ang_r)] * 2 + [jnp.cos(ang_c)] * 2, axis=1)
    sin = jnp.concatenate([-jnp.sin(ang_r), jnp.sin(ang_r), -jnp.sin(ang_c), jnp.sin(ang_c)], axis=1)
    cos = jnp.concatenate([jnp.tile(cos, (1, LANES // HEAD_DIM)), jnp.ones((n_ctx, LANES), F32)], axis=0)
    sin = jnp.concatenate([jnp.tile(sin, (1, LANES // HEAD_DIM)), jnp.zeros((n_ctx, LANES), F32)], axis=0)
    return cos, sin


def kernel(x, c, ctx, c_ctx, w_mod, b_mod, g_pre_mix, g_post_mix, g_pre_ffn, g_post_ffn, w_in, w_out, q_gain, k_gain, rpb, rk_w0, rk_w_up, rk_a0, rk_a_up, rk_g_up, rk_k_k, rk_k_a, rk_r_k, rk_lnx_w, rk_lnx_b, w_router, w_e_gate, w_e_up, w_e_down):
    B, n_lat, D = x.shape
    n_ctx = ctx.shape[1]
    T = n_lat + n_ctx
    depth = w_mod.shape[0]
    assert n_lat % ROW_TILE == 0 and n_ctx == ROW_TILE and n_lat % n_ctx == 0 and B < MOD_ROWS
    n_lat_tiles = n_lat // ROW_TILE

    cond = jnp.zeros((MOD_ROWS, D), F32).at[:B].set(c).at[B].set(c_ctx)
    mod = _modulation(cond, w_mod, b_mod)
    cos, sin = _rope_tables(n_lat, n_ctx)
    lane = np.arange(LANES)
    bd = jnp.asarray(lane[:, None] // HEAD_DIM == lane[None, :] // HEAD_DIM, BF16)

    xs = jnp.concatenate([x, ctx], axis=1)
    for i in range(depth):
        need_ctx = i < depth - 1
        n_rows = T if need_ctx else n_lat
        mod4 = mod[i].reshape(MOD_ROWS, 6, 1, D)
        qa, ka, va, rw, qn, kn, vn = _in_projection(
            xs, mod4, g_pre_mix[i], w_in[i], q_gain[i], k_gain[i], cos, sin, bd, n_lat_tiles)
        oa = _gqa_attention(qa, ka, va, n_lat, n_rows)
        on = _neighbourhood_attention(qn, kn, vn, _natten_bias(rpb[i], n_lat // GRID_W), n_lat, need_ctx)
        rk = dict(w0=rk_w0[i], w_up=rk_w_up[i], a0=rk_a0[i], a_up=rk_a_up[i], g_up=rk_g_up[i],
                  k_k=rk_k_k[i], k_a=rk_k_a[i], r_k=rk_r_k[i], lnx_w=rk_lnx_w[i], lnx_b=rk_lnx_b[i])
        rhat, yhat, gt, ht, gate, bonus = _rwkv_chunks(rw, rk, bd)
        y = _rwkv_scan(rhat, yhat, gt, ht, n_lat // CHUNK)
        xs = _out_projection(xs, oa, y, bonus, gate, on, rk, bd, w_out[i], g_post_mix[i], mod4,
                             n_lat_tiles, n_rows)
        h, aff_t = _router(xs, g_pre_ffn[i], mod4, w_router[i], n_lat_tiles)
        wg, wu, wd = (w.astype(BF16) for w in (w_e_gate[i], w_e_up[i], w_e_down[i]))
        f = _moe_set(h, aff_t, wg, wu, wd, 0, n_lat)
        if need_ctx:
            f = jnp.concatenate([f, _moe_set(h, aff_t, wg, wu, wd, n_lat, n_ctx)], axis=1)
        xs = _ffn_residual(xs, f, g_post_ffn[i], mod4, n_lat_tiles)
    return xs
```

```python
import functools

import numpy as np
import jax
import jax.numpy as jnp
from jax import lax
from jax.experimental import pallas as pl
from jax.experimental.pallas import tpu as pltpu

F32 = jnp.float32
BF16 = jnp.bfloat16

HEAD_DIM = 64
GRID_W = 64
A_HEADS = 8
A_KV_HEADS = 2
B_HEADS = 4
C_HEADS = 4
A_WIDTH = A_HEADS * HEAD_DIM
A_KV_WIDTH = A_KV_HEADS * HEAD_DIM
B_WIDTH = B_HEADS * HEAD_DIM
C_WIDTH = C_HEADS * HEAD_DIM
DECAY_LORA = 64
AAA_LORA = 64
GATE_LORA = 128
RW_WIDTH = 3 * B_WIDTH + 2 * DECAY_LORA + 2 * AAA_LORA + GATE_LORA
ROPE_BASE = 10000.0
ROPE_HALF = HEAD_DIM // 4
WIN_ROWS = 8
WIN_COLS = 16
N_EXPERTS = 16
CAPACITY_FACTOR = 2
NORM_EPS = 1e-6
LNX_EPS = 64e-5
ATTN_SCALE = HEAD_DIM ** -0.5
MASK_VALUE = -1e30

LANES = 128
ROW_TILE = 256
CHUNK = 64
MOD_ROWS = 16
VMEM_LIMIT = 56 * 1024 * 1024


def _cparams(*sem):
    return pltpu.CompilerParams(dimension_semantics=sem, vmem_limit_bytes=VMEM_LIMIT)


def _dot(a, b):
    return jnp.dot(a, b, preferred_element_type=F32)


def _dot_nt(a, b):
    return lax.dot_general(a, b, (((1,), (1,)), ((), ())), preferred_element_type=F32)


def _dot_tn(a, b):
    return lax.dot_general(a, b, (((0,), (0,)), ((), ())), preferred_element_type=F32)


def _split2(x):
    hi = x.astype(BF16)
    lo = (x - hi.astype(F32)).astype(BF16)
    return hi, lo


def _split3(x):
    hi = x.astype(BF16)
    r1 = x - hi.astype(F32)
    mid = r1.astype(BF16)
    lo = (r1 - mid.astype(F32)).astype(BF16)
    return hi, mid, lo


def _head_sum(x, bd):
    outs = []
    for c in range(x.shape[1] // LANES):
        hi, lo = _split2(x[:, c * LANES:(c + 1) * LANES])
        outs.append(_dot(hi, bd) + _dot(lo, bd))
    return outs[0] if len(outs) == 1 else jnp.concatenate(outs, axis=1)


def _tile_lanes(t, width):
    reps = width // t.shape[1]
    return t if reps == 1 else jnp.concatenate([t] * reps, axis=1)


def _rms(x, g):
    return x * lax.rsqrt(jnp.mean(x * x, axis=-1, keepdims=True) + NORM_EPS) * g


def _sigmoid(x):
    return 1.0 / (1.0 + jnp.exp(-x))


def _mod_kernel(s_ref, w_ref, b_ref, o_ref):
    s = s_ref[...]
    s = s * _sigmoid(s)
    o_ref[...] = _dot(s, w_ref[...]) + b_ref[...]


def _modulation(cond, w_mod, b_mod):
    L, D, N = w_mod.shape
    tn = 512
    return pl.pallas_call(
        _mod_kernel,
        grid=(L, N // tn),
        in_specs=[pl.BlockSpec((MOD_ROWS, D), lambda l, j: (0, 0)),
                  pl.BlockSpec((None, D, tn), lambda l, j: (l, 0, j)),
                  pl.BlockSpec((None, 1, tn), lambda l, j: (l, 0, j))],
        out_specs=pl.BlockSpec((None, MOD_ROWS, tn), lambda l, j: (l, 0, j)),
        out_shape=jax.ShapeDtypeStruct((L, MOD_ROWS, N), F32),
        compiler_params=_cparams("parallel", "parallel"),
        name="modulation",
    )(cond, w_mod, b_mod.reshape(L, 1, N))


def _rope(x, cos, sin):
    w = x.shape[1]
    lane = lax.broadcasted_iota(jnp.int32, x.shape, 1)
    upper = (lane // ROPE_HALF) % 2 == 1
    partner = jnp.where(upper, pltpu.roll(x, ROPE_HALF, 1), pltpu.roll(x, w - ROPE_HALF, 1))
    return x * _tile_lanes(cos, w) + partner * _tile_lanes(sin, w)


def _inproj_kernel(x_ref, g_ref, shift_ref, scale_ref, wqa_ref, wka_ref, wva_ref, wrw_ref, wn_ref,
                   qg_ref, kg_ref, cos_ref, sin_ref, bd_ref,
                   qa_ref, ka_ref, va_ref, rw_ref, qn_ref, kn_ref, vn_ref):
    h = _rms(x_ref[...], g_ref[...]) * (1.0 + scale_ref[...]) + shift_ref[...]
    hb = h.astype(BF16)
    bd = bd_ref[...]
    cos, sin = cos_ref[...], sin_ref[...]

    def normed(w_ref, gain_ref):
        y = _dot(hb, w_ref[...])
        ms = _head_sum(y * y, bd) * (1.0 / HEAD_DIM)
        return _rope(y * lax.rsqrt(ms + NORM_EPS) * gain_ref[...], cos, sin)

    def put_heads(o_ref, y):
        for hd in range(y.shape[1] // HEAD_DIM):
            o_ref[hd] = y[:, hd * HEAD_DIM:(hd + 1) * HEAD_DIM].astype(o_ref.dtype)

    put_heads(qa_ref, normed(wqa_ref, qg_ref))
    put_heads(ka_ref, normed(wka_ref, kg_ref))
    put_heads(va_ref, _dot(hb, wva_ref[...]))
    rw_ref[...] = _dot(hb, wrw_ref[...])
    yn = _dot(hb, wn_ref[...])
    put_heads(qn_ref, yn[:, :C_WIDTH])
    put_heads(kn_ref, yn[:, C_WIDTH:2 * C_WIDTH])
    put_heads(vn_ref, yn[:, 2 * C_WIDTH:])


def _in_projection(xs, mod4, g_pre, w_in, q_gain, k_gain, cos, sin, bd, n_lat_tiles):
    B, T, D = xs.shape
    tm = ROW_TILE
    o = np.cumsum([0, A_WIDTH, A_KV_WIDTH, A_KV_WIDTH, RW_WIDTH, 3 * C_WIDTH])
    wb = w_in.astype(BF16)
    ws = [wb[:, o[i]:o[i + 1]] for i in range(5)]
    const = lambda shape: pl.BlockSpec(shape, lambda b, t: (0,) * len(shape))
    mod_spec = lambda k: pl.BlockSpec(
        (None, None, 1, D), lambda b, t: (jnp.where(t >= n_lat_tiles, B, b), k, 0, 0))
    heads = lambda nh: pl.BlockSpec((None, nh, tm, HEAD_DIM), lambda b, t: (b, 0, t, 0))
    hshape = lambda nh: jax.ShapeDtypeStruct((B, nh, T, HEAD_DIM), BF16)
    return pl.pallas_call(
        _inproj_kernel,
        grid=(B, T // tm),
        in_specs=[pl.BlockSpec((None, tm, D), lambda b, t: (b, t, 0)),
                  const((1, D)), mod_spec(0), mod_spec(1)]
                 + [const(w.shape) for w in ws]
                 + [const((1, A_WIDTH)), const((1, A_KV_WIDTH)),
                    pl.BlockSpec((tm, LANES), lambda b, t: (t, 0)),
                    pl.BlockSpec((tm, LANES), lambda b, t: (t, 0)),
                    const((LANES, LANES))],
        out_specs=[heads(A_HEADS), heads(A_KV_HEADS), heads(A_KV_HEADS),
                   pl.BlockSpec((None, tm, RW_WIDTH), lambda b, t: (b, t, 0)),
                   heads(C_HEADS), heads(C_HEADS), heads(C_HEADS)],
        out_shape=[hshape(A_HEADS), hshape(A_KV_HEADS), hshape(A_KV_HEADS),
                   jax.ShapeDtypeStruct((B, T, RW_WIDTH), F32),
                   hshape(C_HEADS), hshape(C_HEADS), hshape(C_HEADS)],
        compiler_params=_cparams("parallel", "parallel"),
        name="in_projection",
    )(xs, g_pre.reshape(1, D), mod4, mod4, *ws,
      jnp.tile(q_gain, A_HEADS).reshape(1, A_WIDTH), jnp.tile(k_gain, A_KV_HEADS).reshape(1, A_KV_WIDTH),
      cos, sin, bd)


def _gqa_kernel(q_ref, k_ref, v_ref, o_ref, *, n_lat, n_lat_tiles):
    group = q_ref.shape[0]

    def attend(k, v):
        for hd in range(group):
            s = _dot_nt(q_ref[hd], k) * ATTN_SCALE
            p = jnp.exp(s - jnp.max(s, axis=-1, keepdims=True))
            den = jnp.sum(p, axis=-1, keepdims=True)
            o = _dot(p.astype(BF16), v) / den
            o_ref[:, hd * HEAD_DIM:(hd + 1) * HEAD_DIM] = o.astype(o_ref.dtype)

    @pl.when(pl.program_id(2) < n_lat_tiles)
    def _():
        attend(k_ref[...], v_ref[...])

    @pl.when(pl.program_id(2) >= n_lat_tiles)
    def _():
        attend(k_ref[n_lat:, :], v_ref[n_lat:, :])


def _gqa_attention(q, k, v, n_lat, n_rows):
    B, _, T, _ = q.shape
    tq = ROW_TILE
    group = A_HEADS // A_KV_HEADS
    kv_spec = pl.BlockSpec((None, None, T, HEAD_DIM), lambda b, g, t: (b, g, 0, 0))
    return pl.pallas_call(
        functools.partial(_gqa_kernel, n_lat=n_lat, n_lat_tiles=n_lat // tq),
        grid=(B, A_KV_HEADS, n_rows // tq),
        in_specs=[pl.BlockSpec((None, group, tq, HEAD_DIM), lambda b, g, t: (b, g, t, 0)),
                  kv_spec, kv_spec],
        out_specs=pl.BlockSpec((None, tq, group * HEAD_DIM), lambda b, g, t: (b, t, g)),
        out_shape=jax.ShapeDtypeStruct((B, n_rows, A_WIDTH), BF16),
        compiler_params=_cparams("parallel", "parallel", "parallel"),
        name="gqa_attention",
    )(q, k, v)


def _natten_kernel(q_ref, k_ref, v_ref, bias_ref, o_ref, *, n_lat, rows, need_ctx):
    win = WIN_ROWS * GRID_W
    for hd in range(C_HEADS):
        kc = k_ref[hd, n_lat:, :]
        vc = v_ref[hd, n_lat:, :]

        def row_block(r, carry, hd=hd, kc=kc, vc=vc):
            rs = jnp.clip(r - WIN_ROWS // 2, 0, rows - WIN_ROWS)
            q = q_ref[hd, pl.ds(pl.multiple_of(r * GRID_W, GRID_W), GRID_W), :]
            kw = k_ref[hd, pl.ds(pl.multiple_of(rs * GRID_W, GRID_W), win), :]
            vw = v_ref[hd, pl.ds(pl.multiple_of(rs * GRID_W, GRID_W), win), :]
            s_w = _dot_nt(q, kw) * ATTN_SCALE + bias_ref[r - rs, hd]
            s_c = _dot_nt(q, kc) * ATTN_SCALE
            m = jnp.maximum(jnp.max(s_w, axis=-1, keepdims=True), jnp.max(s_c, axis=-1, keepdims=True))
            p_w = jnp.exp(s_w - m)
            p_c = jnp.exp(s_c - m)
            den = jnp.sum(p_w, axis=-1, keepdims=True) + jnp.sum(p_c, axis=-1, keepdims=True)
            o = (_dot(p_w.astype(BF16), vw) + _dot(p_c.astype(BF16), vc)) / den
            o_ref[pl.ds(pl.multiple_of(r * GRID_W, GRID_W), GRID_W),
                  hd * HEAD_DIM:(hd + 1) * HEAD_DIM] = o.astype(o_ref.dtype)
            return carry

        lax.fori_loop(0, rows, row_block, 0)
        if need_ctx:
            s = _dot_nt(q_ref[hd, n_lat:, :], kc) * ATTN_SCALE
            p = jnp.exp(s - jnp.max(s, axis=-1, keepdims=True))
            o = _dot(p.astype(BF16), vc) / jnp.sum(p, axis=-1, keepdims=True)
            o_ref[n_lat:, hd * HEAD_DIM:(hd + 1) * HEAD_DIM] = o.astype(o_ref.dtype)


def _natten_bias(rpb, rows):
    off = np.arange(WIN_ROWS)[:, None, None]
    jr = np.arange(WIN_ROWS)[None, :, None]
    row_sel = (np.arange(2 * WIN_ROWS - 1)[None, None, :] == jr - off + WIN_ROWS - 1)
    qc = np.arange(GRID_W)[:, None, None]
    kc = np.arange(GRID_W)[None, :, None]
    col_start = np.clip(qc - WIN_COLS // 2, 0, GRID_W - WIN_COLS)
    valid = (kc >= col_start) & (kc < col_start + WIN_COLS)
    col_sel = (np.arange(2 * WIN_COLS - 1)[None, None, :] == kc - qc + WIN_COLS - 1) & valid
    bias = jnp.einsum("hrc,ojr,qkc->ohqjk", rpb.astype(F32), jnp.asarray(row_sel, F32), jnp.asarray(col_sel, F32),
                      precision=lax.Precision.HIGHEST)
    bias = bias + jnp.asarray(np.where(valid[None, None, :, None, :, 0], 0.0, MASK_VALUE), F32)
    return bias.reshape(WIN_ROWS, C_HEADS, GRID_W, WIN_ROWS * GRID_W)


def _neighbourhood_attention(q, k, v, bias, n_lat, need_ctx):
    B, _, T, _ = q.shape
    n_rows = T if need_ctx else n_lat
    full = pl.BlockSpec((None, C_HEADS, T, HEAD_DIM), lambda b: (b, 0, 0, 0))
    return pl.pallas_call(
        functools.partial(_natten_kernel, n_lat=n_lat, rows=n_lat // GRID_W, need_ctx=need_ctx),
        grid=(B,),
        in_specs=[full, full, full, pl.BlockSpec(bias.shape, lambda b: (0, 0, 0, 0))],
        out_specs=pl.BlockSpec((None, n_rows, C_WIDTH), lambda b: (b, 0, 0)),
        out_shape=jax.ShapeDtypeStruct((B, n_rows, C_WIDTH), BF16),
        compiler_params=_cparams("parallel"),
        name="neighbourhood_attention",
    )(q, k, v, bias)


def _rwkv_chunk_kernel(rw_ref, w0_ref, a0_ref, lora_ref, kk_ref, ka_ref, rk_ref, bdw_ref,
                       rhat_ref, yhat_ref, gt_ref, ht_ref, gate_ref, bonus_ref):
    C, W, NH = CHUNK, B_WIDTH, B_HEADS
    S = NH * C
    rw = rw_ref[...]
    r, k, v = rw[:, :W], rw[:, W:2 * W], rw[:, 2 * W:3 * W]
    wl = rw[:, 3 * W:3 * W + 2 * DECAY_LORA]
    al = rw[:, 3 * W + 2 * DECAY_LORA:3 * W + 2 * DECAY_LORA + 2 * AAA_LORA]
    gl = rw[:, 3 * W + 2 * DECAY_LORA + 2 * AAA_LORA:]
    bdw = bdw_ref[...]

    def head_sum(x):
        hi, lo = _split2(x)
        p = _dot(jnp.concatenate([hi, lo], axis=0), bdw)
        return p[:C] + p[C:]

    kkv = k * kk_ref[...]
    kkv = kkv / jnp.maximum(jnp.sqrt(head_sum(kkv * kkv)), 1e-12)
    lora = _dot(jnp.concatenate([jnp.tanh(wl), al, _sigmoid(gl)], axis=1), lora_ref[...])
    gate_ref[...] = lora[:, 4 * W:]
    lws, asigs = [], []
    for d in range(2):
        w = w0_ref[d] + lora[:, d * W:(d + 1) * W]
        w = -(jnp.maximum(-w, 0.0) + jnp.log(1.0 + jnp.exp(-jnp.abs(w)))) - 0.5
        lws.append(-jnp.exp(w))
        asigs.append(_sigmoid(a0_ref[d] + lora[:, (2 + d) * W:(3 + d) * W]))

    tri = (lax.broadcasted_iota(jnp.int32, (C, C), 0) >= lax.broadcasted_iota(jnp.int32, (C, C), 1)).astype(BF16)
    cs = _dot(tri, jnp.concatenate([p for lw in lws for p in _split3(lw)], axis=1))
    prefix = [cs[:, 3 * d * W:(3 * d + 1) * W] + cs[:, (3 * d + 1) * W:(3 * d + 2) * W]
              + cs[:, (3 * d + 2) * W:(3 * d + 3) * W] for d in range(2)]

    srow = lax.broadcasted_iota(jnp.int32, (S, S), 0)
    scol = lax.broadcasted_iota(jnp.int32, (S, S), 1)
    same = (srow // C) == (scol // HEAD_DIM)
    trow, tcol = srow % C, scol % C
    eye_s = srow == scol
    rrow = lax.broadcasted_iota(jnp.int32, (S, C), 0) % C
    rcol = lax.broadcasted_iota(jnp.int32, (S, C), 1)
    eye_c = lax.broadcasted_iota(jnp.int32, (C, W), 0) == lax.broadcasted_iota(jnp.int32, (C, W), 1) % HEAD_DIM

    def stack(x):
        return jnp.where(same, jnp.concatenate([x] * NH, axis=0), 0.0)

    def unstack(x):
        out = x[:C]
        for hd in range(1, NH):
            out = out + x[hd * C:(hd + 1) * C]
        return out

    v_b = v.astype(BF16)
    v_s = stack(v)
    kd_sum = jnp.zeros_like(k)
    for d in range(2):
        lw, asig = lws[d], asigs[d]
        if d == 0:
            before_s, upto_s, before_r, upto_r = trow > tcol, trow >= tcol, rrow > rcol, rrow >= rcol
        else:
            before_s, upto_s, before_r, upto_r = trow < tcol, trow <= tcol, rrow < rcol, rrow <= rcol
        kd = k * (1.0 + (asig - 1.0) * ka_ref[...])
        kd_sum = kd_sum + kd
        b = kkv * asig
        a = -kkv
        tot = prefix[d][C - 1:C, :]
        cum_incl = prefix[d] if d == 0 else tot - prefix[d] + lw
        cum_excl = cum_incl - lw
        rho = 0.5 * tot
        e_in = jnp.exp(rho - cum_incl)
        e_end = jnp.exp(tot - cum_incl)
        e_rho = jnp.exp(rho)
        xa = stack(a * jnp.exp(cum_excl - rho))
        xr = stack(r * jnp.exp(cum_incl - rho))
        b_in, k_in = (b * e_in).astype(BF16), (kd * e_in).astype(BF16)

        m = _dot_nt(jnp.concatenate([xa, xr], axis=0).astype(BF16),
                    jnp.concatenate([b_in] * NH + [k_in], axis=0))
        l_ab = jnp.where(same & before_s, m[:S, :S], 0.0)
        a_rb = jnp.where(same & upto_s, m[S:, :S], 0.0).astype(BF16)
        l_ak = jnp.where(before_r, m[:S, S:], 0.0)
        a_rk = jnp.where(upto_r, m[S:, S:], 0.0)
        lb = l_ab.astype(BF16)
        lp = _dot(lb, lb)
        tinv = jnp.where(eye_s, 1.0, l_ab)
        for _ in range(int(np.log2(C)) - 2):
            p = _dot(jnp.concatenate([tinv, lp], axis=0).astype(BF16), lp.astype(BF16))
            tinv = tinv + p[:S]
            lp = p[S:]
        tinv = tinv + _dot(tinv.astype(BF16), lp.astype(BF16))

        wv = _dot(jnp.concatenate([l_ak, a_rk], axis=0).astype(BF16), v_b)
        w1 = jnp.where(same, wv[:S], 0.0)
        z = _dot(tinv.astype(BF16), jnp.concatenate([xa, w1], axis=1).astype(BF16))
        rz = _dot(a_rb, z.astype(BF16))
        rhat_ref[d] = unstack((xr + rz[:, :W]) * e_rho)
        yhat_ref[d] = unstack(rz[:, W:] + jnp.where(same, wv[S:], 0.0))
        ends = jnp.concatenate([stack(b * e_end), stack(kd * e_end)], axis=0).astype(BF16)
        vals = jnp.concatenate([jnp.concatenate([z[:, :W] * e_rho, z[:, W:]], axis=1),
                                jnp.concatenate([jnp.zeros((S, W), F32), v_s], axis=1)], axis=0)
        gh = _dot_tn(ends, vals.astype(BF16))
        gt_ref[d] = unstack(gh[:, :W]) + jnp.where(eye_c, jnp.exp(tot), 0.0)
        ht_ref[d] = unstack(gh[:, W:])
    bonus_ref[...] = head_sum(r * kd_sum * rk_ref[...]) * v


def _rwkv_chunks(rw, p):
    B, T, _ = rw.shape
    C, W = CHUNK, B_WIDTH
    nc = T // C
    assert C == HEAD_DIM
    const = lambda shape: pl.BlockSpec(shape, lambda b, c: (0,) * len(shape))
    rows = pl.BlockSpec((None, 2, C, W), lambda b, c: (b, 0, c, 0))
    mats = pl.BlockSpec((None, 2, None, C, W), lambda b, c: (b, 0, c, 0, 0))
    flat = pl.BlockSpec((None, C, W), lambda b, c: (b, c, 0))
    vec = lambda a: a.reshape(1, W)
    lora_w = jnp.zeros((2 * DECAY_LORA + 2 * AAA_LORA + GATE_LORA, 5 * W), F32)
    for d in range(2):
        lora_w = lora_w.at[d * DECAY_LORA:(d + 1) * DECAY_LORA, d * W:(d + 1) * W].set(p["w_up"][d])
        lora_w = lora_w.at[2 * DECAY_LORA + d * AAA_LORA:2 * DECAY_LORA + (d + 1) * AAA_LORA,
                           (2 + d) * W:(3 + d) * W].set(p["a_up"][d])
    lora_w = lora_w.at[2 * DECAY_LORA + 2 * AAA_LORA:, 4 * W:].set(p["g_up"])
    lane = np.arange(W)
    bdw = jnp.asarray(lane[:, None] // HEAD_DIM == lane[None, :] // HEAD_DIM, BF16)
    return pl.pallas_call(
        _rwkv_chunk_kernel,
        grid=(B, nc),
        in_specs=[pl.BlockSpec((None, C, RW_WIDTH), lambda b, c: (b, c, 0)),
                  const((2, 1, W)), const((2, 1, W)), const(lora_w.shape),
                  const((1, W)), const((1, W)), const((1, W)), const((W, W))],
        out_specs=[rows, rows, mats, mats, flat, flat],
        out_shape=[jax.ShapeDtypeStruct((B, 2, T, W), F32), jax.ShapeDtypeStruct((B, 2, T, W), F32),
                   jax.ShapeDtypeStruct((B, 2, nc, C, W), F32), jax.ShapeDtypeStruct((B, 2, nc, C, W), F32),
                   jax.ShapeDtypeStruct((B, T, W), F32), jax.ShapeDtypeStruct((B, T, W), F32)],
        compiler_params=_cparams("parallel", "parallel"),
        name="rwkv_chunks",
    )(rw, p["w0"].reshape(2, 1, W), p["a0"].reshape(2, 1, W), lora_w,
      vec(p["k_k"]), vec(p["k_a"]), vec(p["r_k"]), bdw)


def _rwkv_scan_kernel(rhat0_ref, yhat0_ref, gt0_ref, ht0_ref, rhat1_ref, yhat1_ref, gt1_ref, ht1_ref,
                      y0_ref, y1_ref, st_ref):
    @pl.when(pl.program_id(0) == 0)
    def _():
        st_ref[...] = jnp.zeros_like(st_ref)

    W = B_WIDTH
    same = (lax.broadcasted_iota(jnp.int32, (W, W), 0) // HEAD_DIM
            == lax.broadcasted_iota(jnp.int32, (W, W), 1) // HEAD_DIM)
    expand = lambda x: jnp.where(same, jnp.concatenate([x] * B_HEADS, axis=0), 0.0)
    dirs = ((rhat0_ref, yhat0_ref, gt0_ref, ht0_ref, y0_ref), (rhat1_ref, yhat1_ref, gt1_ref, ht1_ref, y1_ref))
    for b in range(st_ref.shape[1]):
        for d, (rhat_ref, yhat_ref, gt_ref, ht_ref, y_ref) in enumerate(dirs):
            stb = st_ref[d, b].astype(BF16)
            y_ref[b] = _dot(rhat_ref[b].astype(BF16), stb) + yhat_ref[b]
            st_ref[d, b] = _dot(expand(gt_ref[b]).astype(BF16), stb) + expand(ht_ref[b])


def _rwkv_scan(rhat, yhat, gt, ht, n_lat_chunks):
    B, _, T, W = rhat.shape
    C = CHUNK
    nc = T // C
    chunk = (lambda s: (s + n_lat_chunks) % nc, lambda s: nc - 1 - s)
    rows = lambda d: pl.BlockSpec((B, None, C, W), lambda s: (0, d, chunk[d](s), 0))
    mats = lambda d: pl.BlockSpec((B, None, None, C, W), lambda s: (0, d, chunk[d](s), 0, 0))
    outs = lambda d: pl.BlockSpec((B, C, W), lambda s: (0, chunk[d](s), 0))
    return pl.pallas_call(
        _rwkv_scan_kernel,
        grid=(nc,),
        in_specs=[rows(0), rows(0), mats(0), mats(0), rows(1), rows(1), mats(1), mats(1)],
        out_specs=[outs(0), outs(1)],
        out_shape=[jax.ShapeDtypeStruct((B, T, W), F32)] * 2,
        scratch_shapes=[pltpu.VMEM((2, B, W, W), F32)],
        compiler_params=_cparams("arbitrary"),
        name="rwkv_scan",
    )(rhat, yhat, gt, ht, rhat, yhat, gt, ht)


def _outproj_kernel(x_ref, oa_ref, y0_ref, y1_ref, bonus_ref, gate_ref, on_ref, lnw_ref, lnb_ref, bd_ref,
                    wa_ref, wb_ref, wn_ref, g_ref, mgate_ref, o_ref):
    y = y0_ref[...] + y1_ref[...]
    bd = bd_ref[...]
    mu = _head_sum(y, bd) * (1.0 / HEAD_DIM)
    yc = y - mu
    var = _head_sum(yc * yc, bd) * (1.0 / HEAD_DIM)
    ob = (yc * lax.rsqrt(var + LNX_EPS) * lnw_ref[...] + lnb_ref[...] + bonus_ref[...]) * gate_ref[...]
    out = _dot(oa_ref[...], wa_ref[...]) + _dot(ob.astype(BF16), wb_ref[...]) + _dot(on_ref[...], wn_ref[...])
    o_ref[...] = x_ref[...] + mgate_ref[...] * _rms(out, g_ref[...])


def _out_projection(xs, oa, y0, y1, bonus, gate, on, p, bd, w_out, g_post, mod4, n_lat_tiles, n_rows):
    B, T, D = xs.shape
    tm = ROW_TILE
    W = B_WIDTH
    wb = w_out.astype(BF16)
    const = lambda shape: pl.BlockSpec(shape, lambda b, t: (0,) * len(shape))
    rows = lambda width: pl.BlockSpec((None, tm, width), lambda b, t: (b, t, 0))
    return pl.pallas_call(
        _outproj_kernel,
        grid=(B, n_rows // tm),
        in_specs=[rows(D), rows(A_WIDTH), rows(W), rows(W), rows(W), rows(W), rows(C_WIDTH),
                  const((1, W)), const((1, W)), const((LANES, LANES)),
                  const((A_WIDTH, D)), const((W, D)), const((C_WIDTH, D)), const((1, D)),
                  pl.BlockSpec((None, None, 1, D), lambda b, t: (jnp.where(t >= n_lat_tiles, B, b), 2, 0, 0))],
        out_specs=rows(D),
        out_shape=jax.ShapeDtypeStruct((B, n_rows, D), F32),
        compiler_params=_cparams("parallel", "parallel"),
        name="out_projection",
    )(xs, oa, y0, y1, bonus, gate, on, p["lnx_w"].reshape(1, W), p["lnx_b"].reshape(1, W), bd,
      wb[:A_WIDTH], wb[A_WIDTH:A_WIDTH + W], wb[A_WIDTH + W:], g_post.reshape(1, D), mod4)


def _router_kernel(x_ref, g_ref, shift_ref, scale_ref, wr_ref, h_ref, aff_ref):
    h = _rms(x_ref[...], g_ref[...]) * (1.0 + scale_ref[...]) + shift_ref[...]
    h_ref[...] = h.astype(h_ref.dtype)
    logits = _dot_nt(wr_ref[...], h)
    p = jnp.exp(logits - jnp.max(logits, axis=0, keepdims=True))
    aff_ref[...] = p / jnp.sum(p, axis=0, keepdims=True)


def _router(xs, g_pre, mod4, w_router, n_lat_tiles):
    B, R, D = xs.shape
    tm = ROW_TILE
    E = w_router.shape[1]
    mod_spec = lambda k: pl.BlockSpec(
        (None, None, 1, D), lambda b, t: (jnp.where(t >= n_lat_tiles, B, b), k, 0, 0))
    return pl.pallas_call(
        _router_kernel,
        grid=(B, R // tm),
        in_specs=[pl.BlockSpec((None, tm, D), lambda b, t: (b, t, 0)),
                  pl.BlockSpec((1, D), lambda b, t: (0, 0)), mod_spec(3), mod_spec(4),
                  pl.BlockSpec((E, D), lambda b, t: (0, 0))],
        out_specs=[pl.BlockSpec((None, tm, D), lambda b, t: (b, t, 0)),
                   pl.BlockSpec((None, E, tm), lambda b, t: (b, 0, t))],
        out_shape=[jax.ShapeDtypeStruct((B, R, D), BF16), jax.ShapeDtypeStruct((B, E, R), F32)],
        compiler_params=_cparams("parallel", "parallel"),
        name="moe_router",
    )(xs, g_pre.reshape(1, D), mod4, mod4, w_router.T)


def _rank_kernel(arow_ref, acol_ref, rank_ref):
    n = arow_ref.shape[-1]
    e = pl.program_id(1)
    a_row = arow_ref[...]
    lane = lax.broadcasted_iota(jnp.int32, acol_ref.shape, 1)
    a_col = jnp.sum(jnp.where(lane == e, acol_ref[...], 0.0), axis=1, keepdims=True)
    tj = min(n, 256)
    ones = jnp.ones((8, tj), BF16)
    i_idx = lax.broadcasted_iota(jnp.int32, (tj, n), 1)
    j_loc = lax.broadcasted_iota(jnp.int32, (tj, n), 0)
    cnt = jnp.zeros((8, n), F32)
    for jc in range(n // tj):
        aj = a_col[jc * tj:(jc + 1) * tj, :]
        before = (aj > a_row) | ((aj == a_row) & (j_loc + jc * tj < i_idx))
        cnt = cnt + _dot(ones, before.astype(F32).astype(BF16))
    rank_ref[...] = cnt[0:1, :]


def _ranks(aff_t, aff_c):
    B, E, n = aff_t.shape
    return pl.pallas_call(
        _rank_kernel,
        grid=(B, E),
        in_specs=[pl.BlockSpec((None, None, 1, n), lambda b, e: (b, e, 0, 0)),
                  pl.BlockSpec((None, n, E), lambda b, e: (b, 0, 0))],
        out_specs=pl.BlockSpec((None, None, 1, n), lambda b, e: (b, e, 0, 0)),
        out_shape=jax.ShapeDtypeStruct((B, E, 1, n), F32),
        compiler_params=_cparams("parallel", "parallel"),
        name="moe_ranks",
    )(aff_t.reshape(B, E, 1, n), aff_c)


def _expert_kernel(h_ref, rrow_ref, rcol_ref, acol_ref, wg_ref, wu_ref, wd_ref, f_ref, *, cap):
    n = h_ref.shape[0]
    e = pl.program_id(1)
    slot = lax.broadcasted_iota(jnp.int32, (cap, n), 0).astype(F32)
    pick = (rrow_ref[...] == slot).astype(F32).astype(BF16)
    xe = _dot(pick, h_ref[...]).astype(BF16)
    gte = _dot(xe, wg_ref[...])
    hid = (gte * _sigmoid(gte) * _dot(xe, wu_ref[...])).astype(BF16)
    ye = _dot(hid, wd_ref[...]).astype(BF16)
    lane = lax.broadcasted_iota(jnp.int32, rcol_ref.shape, 1)
    r_col = jnp.sum(jnp.where(lane == e, rcol_ref[...], 0.0), axis=1, keepdims=True)
    a_col = jnp.sum(jnp.where(lane == e, acol_ref[...], 0.0), axis=1, keepdims=True)
    tn = min(n, 512)
    slot_l = lax.broadcasted_iota(jnp.int32, (tn, cap), 1).astype(F32)
    for c in range(n // tn):
        rows = slice(c * tn, (c + 1) * tn)
        put = jnp.where(r_col[rows] == slot_l, a_col[rows], 0.0).astype(BF16)
        upd = _dot(put, ye)

        @pl.when(e == 0)
        def _():
            f_ref[rows, :] = upd

        @pl.when(e > 0)
        def _():
            f_ref[rows, :] += upd


def _experts(h, rank_t, rank_c, aff_c, wg, wu, wd, row0_blocks):
    B, E, _, n = rank_t.shape
    D = h.shape[-1]
    F = wg.shape[-1]
    cap = CAPACITY_FACTOR * n // E
    col = pl.BlockSpec((None, n, E), lambda b, e: (b, 0, 0))
    return pl.pallas_call(
        functools.partial(_expert_kernel, cap=cap),
        grid=(B, E),
        in_specs=[pl.BlockSpec((None, n, D), lambda b, e: (b, row0_blocks, 0)),
                  pl.BlockSpec((None, None, 1, n), lambda b, e: (b, e, 0, 0)), col, col,
                  pl.BlockSpec((None, D, F), lambda b, e: (e, 0, 0)),
                  pl.BlockSpec((None, D, F), lambda b, e: (e, 0, 0)),
                  pl.BlockSpec((None, F, D), lambda b, e: (e, 0, 0))],
        out_specs=pl.BlockSpec((None, n, D), lambda b, e: (b, 0, 0)),
        out_shape=jax.ShapeDtypeStruct((B, n, D), F32),
        compiler_params=_cparams("parallel", "arbitrary"),
        name="moe_experts",
    )(h, rank_t, rank_c, aff_c, wg, wu, wd)


def _ffn_residual_kernel(x_ref, f_ref, g_ref, mgate_ref, o_ref):
    o_ref[...] = x_ref[...] + mgate_ref[...] * _rms(f_ref[...], g_ref[...])


def _ffn_residual(xs, f, g_post, mod4, n_lat_tiles):
    B, R, D = xs.shape
    tm = ROW_TILE
    rows = pl.BlockSpec((None, tm, D), lambda b, t: (b, t, 0))
    return pl.pallas_call(
        _ffn_residual_kernel,
        grid=(B, R // tm),
        in_specs=[rows, rows, pl.BlockSpec((1, D), lambda b, t: (0, 0)),
                  pl.BlockSpec((None, None, 1, D), lambda b, t: (jnp.where(t >= n_lat_tiles, B, b), 5, 0, 0))],
        out_specs=rows,
        out_shape=jax.ShapeDtypeStruct((B, R, D), F32),
        compiler_params=_cparams("parallel", "parallel"),
        name="ffn_residual",
    )(xs, f, g_post.reshape(1, D), mod4)


def _moe_set(h, aff_t, wg, wu, wd, row0, n):
    a_t = aff_t[:, :, row0:row0 + n]
    a_c = jnp.swapaxes(a_t, 1, 2)
    rank_t = _ranks(a_t, a_c)
    rank_c = jnp.swapaxes(rank_t[:, :, 0, :], 1, 2)
    return _experts(h, rank_t, rank_c, a_c, wg, wu, wd, row0 // n)


def _rope_tables(n_lat, n_ctx):
    inv = ROPE_BASE ** (-jnp.arange(ROPE_HALF, dtype=F32) / ROPE_HALF)
    pos = jnp.arange(n_lat)
    ang_r = (pos // GRID_W).astype(F32)[:, None] * inv[None, :]
    ang_c = (pos % GRID_W).astype(F32)[:, None] * inv[None, :]
    cos = jnp.concatenate([jnp.cos(ang_r)] * 2 + [jnp.cos(ang_c)] * 2, axis=1)
    sin = jnp.concatenate([-jnp.sin(ang_r), jnp.sin(ang_r), -jnp.sin(ang_c), jnp.sin(ang_c)], axis=1)
    cos = jnp.concatenate([jnp.tile(cos, (1, LANES // HEAD_DIM)), jnp.ones((n_ctx, LANES), F32)], axis=0)
    sin = jnp.concatenate([jnp.tile(sin, (1, LANES // HEAD_DIM)), jnp.zeros((n_ctx, LANES), F32)], axis=0)
    return cos, sin


def kernel(x, c, ctx, c_ctx, w_mod, b_mod, g_pre_mix, g_post_mix, g_pre_ffn, g_post_ffn, w_in, w_out, q_gain, k_gain, rpb, rk_w0, rk_w_up, rk_a0, rk_a_up, rk_g_up, rk_k_k, rk_k_a, rk_r_k, rk_lnx_w, rk_lnx_b, w_router, w_e_gate, w_e_up, w_e_down):
    B, n_lat, D = x.shape
    n_ctx = ctx.shape[1]
    T = n_lat + n_ctx
    depth = w_mod.shape[0]
    assert n_lat % ROW_TILE == 0 and n_ctx == ROW_TILE and n_lat % n_ctx == 0 and B < MOD_ROWS
    n_lat_tiles = n_lat // ROW_TILE

    cond = jnp.zeros((MOD_ROWS, D), F32).at[:B].set(c).at[B].set(c_ctx)
    mod = _modulation(cond, w_mod, b_mod)
    cos, sin = _rope_tables(n_lat, n_ctx)
    lane = np.arange(LANES)
    bd = jnp.asarray(lane[:, None] // HEAD_DIM == lane[None, :] // HEAD_DIM, BF16)

    xs = jnp.concatenate([x, ctx], axis=1)
    for i in range(depth):
        need_ctx = i < depth - 1
        n_rows = T if need_ctx else n_lat
        mod4 = mod[i].reshape(MOD_ROWS, 6, 1, D)
        qa, ka, va, rw, qn, kn, vn = _in_projection(
            xs, mod4, g_pre_mix[i], w_in[i], q_gain[i], k_gain[i], cos, sin, bd, n_lat_tiles)
        oa = _gqa_attention(qa, ka, va, n_lat, n_rows)
        on = _neighbourhood_attention(qn, kn, vn, _natten_bias(rpb[i], n_lat // GRID_W), n_lat, need_ctx)
        rk = dict(w0=rk_w0[i], w_up=rk_w_up[i], a0=rk_a0[i], a_up=rk_a_up[i], g_up=rk_g_up[i],
                  k_k=rk_k_k[i], k_a=rk_k_a[i], r_k=rk_r_k[i], lnx_w=rk_lnx_w[i], lnx_b=rk_lnx_b[i])
        rhat, yhat, gt, ht, gate, bonus = _rwkv_chunks(rw, rk)
        y0, y1 = _rwkv_scan(rhat, yhat, gt, ht, n_lat // CHUNK)
        xs = _out_projection(xs, oa, y0, y1, bonus, gate, on, rk, bd, w_out[i], g_post_mix[i], mod4,
                             n_lat_tiles, n_rows)
        h, aff_t = _router(xs, g_pre_ffn[i], mod4, w_router[i], n_lat_tiles)
        wg, wu, wd = (w.astype(BF16) for w in (w_e_gate[i], w_e_up[i], w_e_down[i]))
        f = _moe_set(h, aff_t, wg, wu, wd, 0, n_lat)
        if need_ctx:
            f = jnp.concatenate([f, _moe_set(h, aff_t, wg, wu, wd, n_lat, n_ctx)], axis=1)
        xs = _ffn_residual(xs, f, g_post_ffn[i], mod4, n_lat_tiles)
    return xs
```

```python
import functools

import numpy as np
import jax
import jax.numpy as jnp
from jax import lax
from jax.experimental import pallas as pl
from jax.experimental.pallas import tpu as pltpu

F32 = jnp.float32
BF16 = jnp.bfloat16

HEAD_DIM = 64
GRID_W = 64
A_HEADS = 8
A_KV_HEADS = 2
B_HEADS = 4
C_HEADS = 4
A_WIDTH = A_HEADS * HEAD_DIM
A_KV_WIDTH = A_KV_HEADS * HEAD_DIM
B_WIDTH = B_HEADS * HEAD_DIM
C_WIDTH = C_HEADS * HEAD_DIM
DECAY_LORA = 64
AAA_LORA = 64
GATE_LORA = 128
RW_WIDTH = 3 * B_WIDTH + 2 * DECAY_LORA + 2 * AAA_LORA + GATE_LORA
ROPE_BASE = 10000.0
ROPE_HALF = HEAD_DIM // 4
WIN_ROWS = 8
WIN_COLS = 16
N_EXPERTS = 16
CAPACITY_FACTOR = 2
NORM_EPS = 1e-6
LNX_EPS = 64e-5
ATTN_SCALE = HEAD_DIM ** -0.5
MASK_VALUE = -1e30

LANES = 128
ROW_TILE = 256
CHUNK = 64
RWKV_CHUNKS_PER_STEP = 4
NATTEN_ROWS_PER_ITER = 8
MOD_ROWS = 16
VMEM_LIMIT = 56 * 1024 * 1024


def _cparams(*sem):
    return pltpu.CompilerParams(dimension_semantics=sem, vmem_limit_bytes=VMEM_LIMIT)


def _dot(a, b):
    return jnp.dot(a, b, preferred_element_type=F32)


def _dot_nt(a, b):
    return lax.dot_general(a, b, (((1,), (1,)), ((), ())), preferred_element_type=F32)


def _dot_tn(a, b):
    return lax.dot_general(a, b, (((0,), (0,)), ((), ())), preferred_element_type=F32)


def _split2(x):
    hi = x.astype(BF16)
    lo = (x - hi.astype(F32)).astype(BF16)
    return hi, lo


def _split3(x):
    hi = x.astype(BF16)
    r1 = x - hi.astype(F32)
    mid = r1.astype(BF16)
    lo = (r1 - mid.astype(F32)).astype(BF16)
    return hi, mid, lo


def _head_sum(x, bd):
    outs = []
    for c in range(x.shape[1] // LANES):
        hi, lo = _split2(x[:, c * LANES:(c + 1) * LANES])
        outs.append(_dot(hi, bd) + _dot(lo, bd))
    return outs[0] if len(outs) == 1 else jnp.concatenate(outs, axis=1)


def _tile_lanes(t, width):
    reps = width // t.shape[1]
    return t if reps == 1 else jnp.concatenate([t] * reps, axis=1)


def _rms(x, g):
    return x * lax.rsqrt(jnp.mean(x * x, axis=-1, keepdims=True) + NORM_EPS) * g


def _sigmoid(x):
    return 1.0 / (1.0 + jnp.exp(-x))


def _mod_kernel(s_ref, w_ref, b_ref, o_ref):
    s = s_ref[...]
    s = s * _sigmoid(s)
    o_ref[...] = _dot(s, w_ref[...]) + b_ref[...]


def _modulation(cond, w_mod, b_mod):
    L, D, N = w_mod.shape
    tn = 512
    return pl.pallas_call(
        _mod_kernel,
        grid=(L, N // tn),
        in_specs=[pl.BlockSpec((MOD_ROWS, D), lambda l, j: (0, 0)),
                  pl.BlockSpec((None, D, tn), lambda l, j: (l, 0, j)),
                  pl.BlockSpec((None, 1, tn), lambda l, j: (l, 0, j))],
        out_specs=pl.BlockSpec((None, MOD_ROWS, tn), lambda l, j: (l, 0, j)),
        out_shape=jax.ShapeDtypeStruct((L, MOD_ROWS, N), F32),
        compiler_params=_cparams("parallel", "parallel"),
        name="modulation",
    )(cond, w_mod, b_mod.reshape(L, 1, N))


def _rope(x, cos, sin):
    w = x.shape[1]
    lane = lax.broadcasted_iota(jnp.int32, x.shape, 1)
    upper = (lane // ROPE_HALF) % 2 == 1
    partner = jnp.where(upper, pltpu.roll(x, ROPE_HALF, 1), pltpu.roll(x, w - ROPE_HALF, 1))
    return x * _tile_lanes(cos, w) + partner * _tile_lanes(sin, w)


def _inproj_kernel(x_ref, g_ref, shift_ref, scale_ref, wqa_ref, wka_ref, wva_ref, wrw_ref, wn_ref,
                   qg_ref, kg_ref, cos_ref, sin_ref, bd_ref,
                   qa_ref, ka_ref, va_ref, rw_ref, qn_ref, kn_ref, vn_ref):
    h = _rms(x_ref[...], g_ref[...]) * (1.0 + scale_ref[...]) + shift_ref[...]
    hb = h.astype(BF16)
    bd = bd_ref[...]
    cos, sin = cos_ref[...], sin_ref[...]

    def normed(w_ref, gain_ref):
        y = _dot(hb, w_ref[...])
        ms = _head_sum(y * y, bd) * (1.0 / HEAD_DIM)
        return _rope(y * lax.rsqrt(ms + NORM_EPS) * gain_ref[...], cos, sin)

    def put_heads(o_ref, y):
        for hd in range(y.shape[1] // HEAD_DIM):
            o_ref[hd] = y[:, hd * HEAD_DIM:(hd + 1) * HEAD_DIM].astype(o_ref.dtype)

    put_heads(qa_ref, normed(wqa_ref, qg_ref) * ATTN_SCALE)
    put_heads(ka_ref, normed(wka_ref, kg_ref))
    put_heads(va_ref, _dot(hb, wva_ref[...]))
    rw_ref[...] = _dot(hb, wrw_ref[...])
    yn = _dot(hb, wn_ref[...])
    put_heads(qn_ref, yn[:, :C_WIDTH] * ATTN_SCALE)
    put_heads(kn_ref, yn[:, C_WIDTH:2 * C_WIDTH])
    put_heads(vn_ref, yn[:, 2 * C_WIDTH:])


def _in_projection(xs, mod4, g_pre, w_in, q_gain, k_gain, cos, sin, bd, n_lat_tiles):
    B, T, D = xs.shape
    tm = ROW_TILE
    o = np.cumsum([0, A_WIDTH, A_KV_WIDTH, A_KV_WIDTH, RW_WIDTH, 3 * C_WIDTH])
    wb = w_in.astype(BF16)
    ws = [wb[:, o[i]:o[i + 1]] for i in range(5)]
    const = lambda shape: pl.BlockSpec(shape, lambda b, t: (0,) * len(shape))
    mod_spec = lambda k: pl.BlockSpec(
        (None, None, 1, D), lambda b, t: (jnp.where(t >= n_lat_tiles, B, b), k, 0, 0))
    heads = lambda nh: pl.BlockSpec((None, nh, tm, HEAD_DIM), lambda b, t: (b, 0, t, 0))
    hshape = lambda nh: jax.ShapeDtypeStruct((B, nh, T, HEAD_DIM), BF16)
    return pl.pallas_call(
        _inproj_kernel,
        grid=(B, T // tm),
        in_specs=[pl.BlockSpec((None, tm, D), lambda b, t: (b, t, 0)),
                  const((1, D)), mod_spec(0), mod_spec(1)]
                 + [const(w.shape) for w in ws]
                 + [const((1, A_WIDTH)), const((1, A_KV_WIDTH)),
                    pl.BlockSpec((tm, LANES), lambda b, t: (t, 0)),
                    pl.BlockSpec((tm, LANES), lambda b, t: (t, 0)),
                    const((LANES, LANES))],
        out_specs=[heads(A_HEADS), heads(A_KV_HEADS), heads(A_KV_HEADS),
                   pl.BlockSpec((None, tm, RW_WIDTH), lambda b, t: (b, t, 0)),
                   heads(C_HEADS), heads(C_HEADS), heads(C_HEADS)],
        out_shape=[hshape(A_HEADS), hshape(A_KV_HEADS), hshape(A_KV_HEADS),
                   jax.ShapeDtypeStruct((B, T, RW_WIDTH), F32),
                   hshape(C_HEADS), hshape(C_HEADS), hshape(C_HEADS)],
        compiler_params=_cparams("parallel", "parallel"),
        name="in_projection",
    )(xs, g_pre.reshape(1, D), mod4, mod4, *ws,
      jnp.tile(q_gain, A_HEADS).reshape(1, A_WIDTH), jnp.tile(k_gain, A_KV_HEADS).reshape(1, A_KV_WIDTH),
      cos, sin, bd)


def _gqa_kernel(q_ref, k_ref, v_ref, o_ref, *, n_lat, n_lat_tiles):
    group = q_ref.shape[0]

    def attend(k, v):
        scores = [_dot_nt(q_ref[hd], k) for hd in range(group)]
        probs = []
        for s in scores:
            p = jnp.exp(s - jnp.max(s, axis=-1, keepdims=True))
            probs.append((p.astype(BF16), jnp.sum(p, axis=-1, keepdims=True)))
        for hd, (p, den) in enumerate(probs):
            o_ref[:, hd * HEAD_DIM:(hd + 1) * HEAD_DIM] = (_dot(p, v) / den).astype(o_ref.dtype)

    @pl.when(pl.program_id(2) < n_lat_tiles)
    def _():
        attend(k_ref[...], v_ref[...])

    @pl.when(pl.program_id(2) >= n_lat_tiles)
    def _():
        attend(k_ref[n_lat:, :], v_ref[n_lat:, :])


def _gqa_attention(q, k, v, n_lat, n_rows):
    B, _, T, _ = q.shape
    tq = ROW_TILE
    group = A_HEADS // A_KV_HEADS
    kv_spec = pl.BlockSpec((None, None, T, HEAD_DIM), lambda b, g, t: (b, g, 0, 0))
    return pl.pallas_call(
        functools.partial(_gqa_kernel, n_lat=n_lat, n_lat_tiles=n_lat // tq),
        grid=(B, A_KV_HEADS, n_rows // tq),
        in_specs=[pl.BlockSpec((None, group, tq, HEAD_DIM), lambda b, g, t: (b, g, t, 0)),
                  kv_spec, kv_spec],
        out_specs=pl.BlockSpec((None, tq, group * HEAD_DIM), lambda b, g, t: (b, t, g)),
        out_shape=jax.ShapeDtypeStruct((B, n_rows, A_WIDTH), BF16),
        compiler_params=_cparams("parallel", "parallel", "parallel"),
        name="gqa_attention",
    )(q, k, v)


def _natten_kernel(q_ref, k_ref, v_ref, bias_ref, o_ref, *, n_lat, rows, need_ctx):
    win = WIN_ROWS * GRID_W
    for hd in range(C_HEADS):
        kc = k_ref[hd, n_lat:, :]
        vc = v_ref[hd, n_lat:, :]

        def row_group(g, carry, hd=hd, kc=kc, vc=vc):
            scores = []
            for u in range(NATTEN_ROWS_PER_ITER):
                r = g * NATTEN_ROWS_PER_ITER + u
                rs = jnp.clip(r - WIN_ROWS // 2, 0, rows - WIN_ROWS)
                q = q_ref[hd, pl.ds(pl.multiple_of(r * GRID_W, GRID_W), GRID_W), :]
                kw = k_ref[hd, pl.ds(pl.multiple_of(rs * GRID_W, GRID_W), win), :]
                scores.append((r, rs, _dot_nt(q, kw) + bias_ref[r - rs, hd], _dot_nt(q, kc)))
            probs = []
            for r, rs, s_w, s_c in scores:
                m = jnp.maximum(jnp.max(s_w, axis=-1, keepdims=True), jnp.max(s_c, axis=-1, keepdims=True))
                p_w = jnp.exp(s_w - m)
                p_c = jnp.exp(s_c - m)
                den = jnp.sum(p_w, axis=-1, keepdims=True) + jnp.sum(p_c, axis=-1, keepdims=True)
                probs.append((r, rs, p_w.astype(BF16), p_c.astype(BF16), den))
            for r, rs, p_w, p_c, den in probs:
                vw = v_ref[hd, pl.ds(pl.multiple_of(rs * GRID_W, GRID_W), win), :]
                o = (_dot(p_w, vw) + _dot(p_c, vc)) / den
                o_ref[pl.ds(pl.multiple_of(r * GRID_W, GRID_W), GRID_W),
                      hd * HEAD_DIM:(hd + 1) * HEAD_DIM] = o.astype(o_ref.dtype)
            return carry

        lax.fori_loop(0, rows // NATTEN_ROWS_PER_ITER, row_group, 0)
        if need_ctx:
            s = _dot_nt(q_ref[hd, n_lat:, :], kc)
            p = jnp.exp(s - jnp.max(s, axis=-1, keepdims=True))
            o = _dot(p.astype(BF16), vc) / jnp.sum(p, axis=-1, keepdims=True)
            o_ref[n_lat:, hd * HEAD_DIM:(hd + 1) * HEAD_DIM] = o.astype(o_ref.dtype)


def _natten_bias(rpb, rows):
    off = np.arange(WIN_ROWS)[:, None, None]
    jr = np.arange(WIN_ROWS)[None, :, None]
    row_sel = (np.arange(2 * WIN_ROWS - 1)[None, None, :] == jr - off + WIN_ROWS - 1)
    qc = np.arange(GRID_W)[:, None, None]
    kc = np.arange(GRID_W)[None, :, None]
    col_start = np.clip(qc - WIN_COLS // 2, 0, GRID_W - WIN_COLS)
    valid = (kc >= col_start) & (kc < col_start + WIN_COLS)
    col_sel = (np.arange(2 * WIN_COLS - 1)[None, None, :] == kc - qc + WIN_COLS - 1) & valid
    bias = jnp.einsum("hrc,ojr,qkc->ohqjk", rpb.astype(F32), jnp.asarray(row_sel, F32), jnp.asarray(col_sel, F32),
                      precision=lax.Precision.HIGHEST)
    bias = bias + jnp.asarray(np.where(valid[None, None, :, None, :, 0], 0.0, MASK_VALUE), F32)
    return bias.reshape(WIN_ROWS, C_HEADS, GRID_W, WIN_ROWS * GRID_W)


def _neighbourhood_attention(q, k, v, bias, n_lat, need_ctx):
    B, _, T, _ = q.shape
    n_rows = T if need_ctx else n_lat
    full = pl.BlockSpec((None, C_HEADS, T, HEAD_DIM), lambda b: (b, 0, 0, 0))
    return pl.pallas_call(
        functools.partial(_natten_kernel, n_lat=n_lat, rows=n_lat // GRID_W, need_ctx=need_ctx),
        grid=(B,),
        in_specs=[full, full, full, pl.BlockSpec(bias.shape, lambda b: (0, 0, 0, 0))],
        out_specs=pl.BlockSpec((None, n_rows, C_WIDTH), lambda b: (b, 0, 0)),
        out_shape=jax.ShapeDtypeStruct((B, n_rows, C_WIDTH), BF16),
        compiler_params=_cparams("parallel"),
        name="neighbourhood_attention",
    )(q, k, v, bias)


def _rwkv_chunk_kernel(rw_ref, w0_ref, a0_ref, lora_ref, kk_ref, ka_ref, rk_ref, bdw_ref,
                       rhat_ref, yhat_ref, gt_ref, ht_ref, gate_ref, bonus_ref):
    C, W, NH = CHUNK, B_WIDTH, B_HEADS
    S = NH * C
    R = rw_ref.shape[0]
    rw = rw_ref[...]
    r, k, v = rw[:, :W], rw[:, W:2 * W], rw[:, 2 * W:3 * W]
    wl = rw[:, 3 * W:3 * W + 2 * DECAY_LORA]
    al = rw[:, 3 * W + 2 * DECAY_LORA:3 * W + 2 * DECAY_LORA + 2 * AAA_LORA]
    gl = rw[:, 3 * W + 2 * DECAY_LORA + 2 * AAA_LORA:]
    bdw = bdw_ref[...]

    def head_sum(x):
        hi, lo = _split2(x)
        p = _dot(jnp.concatenate([hi, lo], axis=0), bdw)
        return p[:R] + p[R:]

    kkv = k * kk_ref[...]
    kkv = kkv / jnp.maximum(jnp.sqrt(head_sum(kkv * kkv)), 1e-12)
    lora = _dot(jnp.concatenate([jnp.tanh(wl), al, _sigmoid(gl)], axis=1), lora_ref[...])
    gate_ref[...] = lora[:, 4 * W:]
    lws, asigs = [], []
    for d in range(2):
        w = w0_ref[d] + lora[:, d * W:(d + 1) * W]
        w = -(jnp.maximum(-w, 0.0) + jnp.log(1.0 + jnp.exp(-jnp.abs(w)))) - 0.5
        lws.append(-jnp.exp(w))
        asigs.append(_sigmoid(a0_ref[d] + lora[:, (2 + d) * W:(3 + d) * W]))

    prow = lax.broadcasted_iota(jnp.int32, (R, R), 0)
    pcol = lax.broadcasted_iota(jnp.int32, (R, R), 1)
    tri = ((prow >= pcol) & (prow // C == pcol // C)).astype(BF16)
    cs = _dot(tri, jnp.concatenate([p for lw in lws for p in _split3(lw)], axis=1))
    prefix = [cs[:, 3 * d * W:(3 * d + 1) * W] + cs[:, (3 * d + 1) * W:(3 * d + 2) * W]
              + cs[:, (3 * d + 2) * W:(3 * d + 3) * W] for d in range(2)]

    srow = lax.broadcasted_iota(jnp.int32, (S, S), 0)
    scol = lax.broadcasted_iota(jnp.int32, (S, S), 1)
    same = (srow // C) == (scol // HEAD_DIM)
    trow, tcol = srow % C, scol % C
    eye_s = srow == scol
    rrow = lax.broadcasted_iota(jnp.int32, (S, C), 0) % C
    rcol = lax.broadcasted_iota(jnp.int32, (S, C), 1)
    eye_c = lax.broadcasted_iota(jnp.int32, (C, W), 0) == lax.broadcasted_iota(jnp.int32, (C, W), 1) % HEAD_DIM

    def stack(x):
        return jnp.where(same, jnp.concatenate([x] * NH, axis=0), 0.0)

    def unstack(x):
        out = x[:C]
        for hd in range(1, NH):
            out = out + x[hd * C:(hd + 1) * C]
        return out

    kds = [k * (1.0 + (asig - 1.0) * ka_ref[...]) for asig in asigs]
    bonus_ref[...] = head_sum(r * (kds[0] + kds[1]) * rk_ref[...]) * v
    class Chain:
        pass

    chains = []
    for ci in range(R // C):
        for d in range(2):
            ch = Chain()
            ch.d, ch.ci, ch.rows = d, ci, slice(ci * C, (ci + 1) * C)
            chains.append(ch)

    for ch in chains:
        d, rows = ch.d, ch.rows
        lw, kd, b, a = lws[d][rows], kds[d][rows], (kkv * asigs[d])[rows], -kkv[rows]
        ch.v = v[rows]
        pre = prefix[d][rows]
        ch.tot = tot = pre[C - 1:C, :]
        cum_incl = pre if d == 0 else tot - pre + lw
        cum_excl = cum_incl - lw
        rho = 0.5 * tot
        e_in = jnp.exp(rho - cum_incl)
        e_end = jnp.exp(tot - cum_incl)
        ch.e_rho = jnp.exp(rho)
        ch.xa = stack(a * jnp.exp(cum_excl - rho))
        ch.xr = stack(r[rows] * jnp.exp(cum_incl - rho))
        ch.ends = jnp.concatenate([stack(b * e_end), stack(kd * e_end)], axis=0).astype(BF16)
        ch.m = _dot_nt(jnp.concatenate([ch.xa, ch.xr], axis=0).astype(BF16),
                       jnp.concatenate([(b * e_in).astype(BF16)] * NH + [(kd * e_in).astype(BF16)], axis=0))

    for ch in chains:
        if ch.d == 0:
            before_s, upto_s, before_r, upto_r = trow > tcol, trow >= tcol, rrow > rcol, rrow >= rcol
        else:
            before_s, upto_s, before_r, upto_r = trow < tcol, trow <= tcol, rrow < rcol, rrow <= rcol
        m = ch.m
        l_ab = jnp.where(same & before_s, m[:S, :S], 0.0)
        ch.a_rb = jnp.where(same & upto_s, m[S:, :S], 0.0).astype(BF16)
        ch.kv = jnp.concatenate([jnp.where(before_r, m[:S, S:], 0.0), jnp.where(upto_r, m[S:, S:], 0.0)],
                                axis=0).astype(BF16)
        lb = l_ab.astype(BF16)
        ch.lp = _dot(lb, lb)
        ch.tinv = jnp.where(eye_s, 1.0, l_ab)
    for _ in range(int(np.log2(C)) - 2):
        for ch in chains:
            p = _dot(jnp.concatenate([ch.tinv, ch.lp], axis=0).astype(BF16), ch.lp.astype(BF16))
            ch.tinv = ch.tinv + p[:S]
            ch.lp = p[S:]
    for ch in chains:
        ch.tinv = ch.tinv + _dot(ch.tinv.astype(BF16), ch.lp.astype(BF16))
    for ch in chains:
        ch.wv = _dot(ch.kv, ch.v.astype(BF16))
    for ch in chains:
        w1 = jnp.where(same, ch.wv[:S], 0.0)
        ch.z = _dot(ch.tinv.astype(BF16), jnp.concatenate([ch.xa, w1], axis=1).astype(BF16))
    for ch in chains:
        rz = _dot(ch.a_rb, ch.z.astype(BF16))
        rhat_ref[ch.d, ch.rows] = unstack((ch.xr + rz[:, :W]) * ch.e_rho)
        yhat_ref[ch.d, ch.rows] = unstack(rz[:, W:] + jnp.where(same, ch.wv[S:], 0.0))
    for ch in chains:
        z = ch.z
        vals = jnp.concatenate([jnp.concatenate([z[:, :W] * ch.e_rho, z[:, W:]], axis=1),
                                jnp.concatenate([jnp.zeros((S, W), F32), stack(ch.v)], axis=1)], axis=0)
        gh = _dot_tn(ch.ends, vals.astype(BF16))
        gt_ref[ch.d, ch.ci] = unstack(gh[:, :W]) + jnp.where(eye_c, jnp.exp(ch.tot), 0.0)
        ht_ref[ch.d, ch.ci] = unstack(gh[:, W:])


def _rwkv_chunks(rw, p):
    B, T, _ = rw.shape
    C, W = CHUNK, B_WIDTH
    nc = T // C
    per = RWKV_CHUNKS_PER_STEP
    assert C == HEAD_DIM and nc % per == 0
    const = lambda shape: pl.BlockSpec(shape, lambda b, c: (0,) * len(shape))
    rows = pl.BlockSpec((None, 2, per * C, W), lambda b, c: (b, 0, c, 0))
    mats = pl.BlockSpec((None, 2, per, C, W), lambda b, c: (b, 0, c, 0, 0))
    flat = pl.BlockSpec((None, per * C, W), lambda b, c: (b, c, 0))
    vec = lambda a: a.reshape(1, W)
    lora_w = jnp.zeros((2 * DECAY_LORA + 2 * AAA_LORA + GATE_LORA, 5 * W), F32)
    for d in range(2):
        lora_w = lora_w.at[d * DECAY_LORA:(d + 1) * DECAY_LORA, d * W:(d + 1) * W].set(p["w_up"][d])
        lora_w = lora_w.at[2 * DECAY_LORA + d * AAA_LORA:2 * DECAY_LORA + (d + 1) * AAA_LORA,
                           (2 + d) * W:(3 + d) * W].set(p["a_up"][d])
    lora_w = lora_w.at[2 * DECAY_LORA + 2 * AAA_LORA:, 4 * W:].set(p["g_up"])
    lane = np.arange(W)
    bdw = jnp.asarray(lane[:, None] // HEAD_DIM == lane[None, :] // HEAD_DIM, BF16)
    return pl.pallas_call(
        _rwkv_chunk_kernel,
        grid=(B, nc // per),
        in_specs=[pl.BlockSpec((None, per * C, RW_WIDTH), lambda b, c: (b, c, 0)),
                  const((2, 1, W)), const((2, 1, W)), const(lora_w.shape),
                  const((1, W)), const((1, W)), const((1, W)), const((W, W))],
        out_specs=[rows, rows, mats, mats, flat, flat],
        out_shape=[jax.ShapeDtypeStruct((B, 2, T, W), F32), jax.ShapeDtypeStruct((B, 2, T, W), F32),
                   jax.ShapeDtypeStruct((B, 2, nc, C, W), F32), jax.ShapeDtypeStruct((B, 2, nc, C, W), F32),
                   jax.ShapeDtypeStruct((B, T, W), F32), jax.ShapeDtypeStruct((B, T, W), F32)],
        compiler_params=_cparams("parallel", "parallel"),
        name="rwkv_chunks",
    )(rw, p["w0"].reshape(2, 1, W), p["a0"].reshape(2, 1, W), lora_w,
      vec(p["k_k"]), vec(p["k_a"]), vec(p["r_k"]), bdw)


def _rwkv_scan_kernel(rhat0_ref, yhat0_ref, gt0_ref, ht0_ref, rhat1_ref, yhat1_ref, gt1_ref, ht1_ref,
                      y0_ref, y1_ref, st_ref):
    @pl.when(pl.program_id(0) == 0)
    def _():
        st_ref[...] = jnp.zeros_like(st_ref)

    W = B_WIDTH
    same = (lax.broadcasted_iota(jnp.int32, (W, W), 0) // HEAD_DIM
            == lax.broadcasted_iota(jnp.int32, (W, W), 1) // HEAD_DIM)
    expand = lambda x: jnp.where(same, jnp.concatenate([x] * B_HEADS, axis=0), 0.0)
    dirs = ((rhat0_ref, yhat0_ref, gt0_ref, ht0_ref, y0_ref), (rhat1_ref, yhat1_ref, gt1_ref, ht1_ref, y1_ref))
    for b in range(st_ref.shape[1]):
        for d, (rhat_ref, yhat_ref, gt_ref, ht_ref, y_ref) in enumerate(dirs):
            stb = st_ref[d, b].astype(BF16)
            y_ref[b] = _dot(rhat_ref[b].astype(BF16), stb) + yhat_ref[b]
            st_ref[d, b] = _dot(expand(gt_ref[b]).astype(BF16), stb) + expand(ht_ref[b])


def _rwkv_scan(rhat, yhat, gt, ht, n_lat_chunks):
    B, _, T, W = rhat.shape
    C = CHUNK
    nc = T // C
    chunk = (lambda s: (s + n_lat_chunks) % nc, lambda s: nc - 1 - s)
    rows = lambda d: pl.BlockSpec((B, None, C, W), lambda s: (0, d, chunk[d](s), 0))
    mats = lambda d: pl.BlockSpec((B, None, None, C, W), lambda s: (0, d, chunk[d](s), 0, 0))
    outs = lambda d: pl.BlockSpec((B, C, W), lambda s: (0, chunk[d](s), 0))
    return pl.pallas_call(
        _rwkv_scan_kernel,
        grid=(nc,),
        in_specs=[rows(0), rows(0), mats(0), mats(0), rows(1), rows(1), mats(1), mats(1)],
        out_specs=[outs(0), outs(1)],
        out_shape=[jax.ShapeDtypeStruct((B, T, W), F32)] * 2,
        scratch_shapes=[pltpu.VMEM((2, B, W, W), F32)],
        compiler_params=_cparams("arbitrary"),
        name="rwkv_scan",
    )(rhat, yhat, gt, ht, rhat, yhat, gt, ht)


def _outproj_kernel(x_ref, oa_ref, y0_ref, y1_ref, bonus_ref, gate_ref, on_ref, lnw_ref, lnb_ref, bd_ref,
                    wa_ref, wb_ref, wn_ref, g_ref, mgate_ref, o_ref):
    y = y0_ref[...] + y1_ref[...]
    bd = bd_ref[...]
    mu = _head_sum(y, bd) * (1.0 / HEAD_DIM)
    yc = y - mu
    var = _head_sum(yc * yc, bd) * (1.0 / HEAD_DIM)
    ob = (yc * lax.rsqrt(var + LNX_EPS) * lnw_ref[...] + lnb_ref[...] + bonus_ref[...]) * gate_ref[...]
    out = _dot(oa_ref[...], wa_ref[...]) + _dot(ob.astype(BF16), wb_ref[...]) + _dot(on_ref[...], wn_ref[...])
    o_ref[...] = x_ref[...] + mgate_ref[...] * _rms(out, g_ref[...])


def _out_projection(xs, oa, y0, y1, bonus, gate, on, p, bd, w_out, g_post, mod4, n_lat_tiles, n_rows):
    B, T, D = xs.shape
    tm = ROW_TILE
    W = B_WIDTH
    wb = w_out.astype(BF16)
    const = lambda shape: pl.BlockSpec(shape, lambda b, t: (0,) * len(shape))
    rows = lambda width: pl.BlockSpec((None, tm, width), lambda b, t: (b, t, 0))
    return pl.pallas_call(
        _outproj_kernel,
        grid=(B, n_rows // tm),
        in_specs=[rows(D), rows(A_WIDTH), rows(W), rows(W), rows(W), rows(W), rows(C_WIDTH),
                  const((1, W)), const((1, W)), const((LANES, LANES)),
                  const((A_WIDTH, D)), const((W, D)), const((C_WIDTH, D)), const((1, D)),
                  pl.BlockSpec((None, None, 1, D), lambda b, t: (jnp.where(t >= n_lat_tiles, B, b), 2, 0, 0))],
        out_specs=rows(D),
        out_shape=jax.ShapeDtypeStruct((B, n_rows, D), F32),
        compiler_params=_cparams("parallel", "parallel"),
        name="out_projection",
    )(xs, oa, y0, y1, bonus, gate, on, p["lnx_w"].reshape(1, W), p["lnx_b"].reshape(1, W), bd,
      wb[:A_WIDTH], wb[A_WIDTH:A_WIDTH + W], wb[A_WIDTH + W:], g_post.reshape(1, D), mod4)


def _router_kernel(x_ref, g_ref, shift_ref, scale_ref, wr_ref, h_ref, aff_ref):
    h = _rms(x_ref[...], g_ref[...]) * (1.0 + scale_ref[...]) + shift_ref[...]
    h_ref[...] = h.astype(h_ref.dtype)
    logits = _dot_nt(wr_ref[...], h)
    p = jnp.exp(logits - jnp.max(logits, axis=0, keepdims=True))
    aff_ref[...] = p / jnp.sum(p, axis=0, keepdims=True)


def _router(xs, g_pre, mod4, w_router, n_lat_tiles):
    B, R, D = xs.shape
    tm = ROW_TILE
    E = w_router.shape[1]
    mod_spec = lambda k: pl.BlockSpec(
        (None, None, 1, D), lambda b, t: (jnp.where(t >= n_lat_tiles, B, b), k, 0, 0))
    return pl.pallas_call(
        _router_kernel,
        grid=(B, R // tm),
        in_specs=[pl.BlockSpec((None, tm, D), lambda b, t: (b, t, 0)),
                  pl.BlockSpec((1, D), lambda b, t: (0, 0)), mod_spec(3), mod_spec(4),
                  pl.BlockSpec((E, D), lambda b, t: (0, 0))],
        out_specs=[pl.BlockSpec((None, tm, D), lambda b, t: (b, t, 0)),
                   pl.BlockSpec((None, E, tm), lambda b, t: (b, 0, t))],
        out_shape=[jax.ShapeDtypeStruct((B, R, D), BF16), jax.ShapeDtypeStruct((B, E, R), F32)],
        compiler_params=_cparams("parallel", "parallel"),
        name="moe_router",
    )(xs, g_pre.reshape(1, D), mod4, mod4, w_router.T)


def _rank_kernel(arow_ref, acol_ref, rank_ref):
    n = arow_ref.shape[-1]
    e = pl.program_id(1)
    a_row = arow_ref[...]
    lane = lax.broadcasted_iota(jnp.int32, acol_ref.shape, 1)
    a_col = jnp.sum(jnp.where(lane == e, acol_ref[...], 0.0), axis=1, keepdims=True)
    tj = min(n, 256)
    ones = jnp.ones((8, tj), BF16)
    j_first = (lax.broadcasted_iota(jnp.int32, (tj, tj), 0) < lax.broadcasted_iota(jnp.int32, (tj, tj), 1))
    as_count = lambda mask: mask.astype(F32).astype(BF16)
    cnt = jnp.zeros((8, n), F32)
    for jc in range(n // tj):
        lo, hi = jc * tj, (jc + 1) * tj
        aj = a_col[lo:hi, :]
        ai = a_row[:, lo:hi]
        parts = [as_count(aj > a_row[:, :lo])] if lo else []
        parts.append(as_count((aj > ai) | ((aj == ai) & j_first)))
        if hi < n:
            parts.append(as_count(aj >= a_row[:, hi:]))
        cnt = cnt + _dot(ones, parts[0] if len(parts) == 1 else jnp.concatenate(parts, axis=1))
    rank_ref[...] = cnt[0:1, :]


def _ranks(aff_t, aff_c):
    B, E, n = aff_t.shape
    return pl.pallas_call(
        _rank_kernel,
        grid=(B, E),
        in_specs=[pl.BlockSpec((None, None, 1, n), lambda b, e: (b, e, 0, 0)),
                  pl.BlockSpec((None, n, E), lambda b, e: (b, 0, 0))],
        out_specs=pl.BlockSpec((None, None, 1, n), lambda b, e: (b, e, 0, 0)),
        out_shape=jax.ShapeDtypeStruct((B, E, 1, n), F32),
        compiler_params=_cparams("parallel", "parallel"),
        name="moe_ranks",
    )(aff_t.reshape(B, E, 1, n), aff_c)


def _expert_kernel(h_ref, rank_ref, aff_ref, wg_ref, wu_ref, wdt_ref, f_ref, yet_ref, ft_ref, *, cap):
    n = h_ref.shape[0]
    E = rank_ref.shape[0]
    e = pl.program_id(1)
    slot = lax.broadcasted_iota(jnp.int32, (cap, n), 0).astype(F32)
    pick = (rank_ref[pl.ds(e, 1), :] == slot).astype(F32).astype(BF16)
    xe = _dot(pick, h_ref[...]).astype(BF16)
    gte = _dot(xe, wg_ref[...])
    hid = (gte * _sigmoid(gte) * _dot(xe, wu_ref[...])).astype(BF16)
    yet_ref[:, pl.ds(pl.multiple_of(e * cap, cap), cap)] = _dot_nt(wdt_ref[...], hid).astype(BF16)

    @pl.when(e == E - 1)
    def _():
        tn = min(n, ROW_TILE)
        slot_c = lax.broadcasted_iota(jnp.int32, (cap, tn), 0).astype(F32)
        for c in range(n // tn):
            cols = slice(c * tn, (c + 1) * tn)
            put = jnp.concatenate(
                [jnp.where(rank_ref[x:x + 1, cols] == slot_c, aff_ref[x:x + 1, cols], 0.0).astype(BF16)
                 for x in range(E)], axis=0)
            ft_ref[...] = _dot(yet_ref[...], put)
            f_ref[cols, :] = ft_ref[...].T


def _experts(h, rank, aff, wg, wu, wdt, cap):
    S, E, n = rank.shape
    D = h.shape[-1]
    F = wg.shape[-1]
    assert cap % LANES == 0
    rows = pl.BlockSpec((None, E, n), lambda s, e: (s, 0, 0))
    return pl.pallas_call(
        functools.partial(_expert_kernel, cap=cap),
        grid=(S, E),
        in_specs=[pl.BlockSpec((None, n, D), lambda s, e: (s, 0, 0)), rows, rows,
                  pl.BlockSpec((None, D, F), lambda s, e: (e, 0, 0)),
                  pl.BlockSpec((None, D, F), lambda s, e: (e, 0, 0)),
                  pl.BlockSpec((None, D, F), lambda s, e: (e, 0, 0))],
        out_specs=pl.BlockSpec((None, n, D), lambda s, e: (s, 0, 0)),
        out_shape=jax.ShapeDtypeStruct((S, n, D), F32),
        scratch_shapes=[pltpu.VMEM((D, E * cap), BF16), pltpu.VMEM((D, min(n, ROW_TILE)), F32)],
        compiler_params=_cparams("parallel", "arbitrary"),
        name="moe_experts",
    )(h, rank, aff, wg, wu, wdt)


def _ffn_residual_kernel(x_ref, f_ref, g_ref, mgate_ref, o_ref):
    o_ref[...] = x_ref[...] + mgate_ref[...] * _rms(f_ref[...], g_ref[...])


def _ffn_residual(xs, f, g_post, mod4, n_lat_tiles):
    B, R, D = xs.shape
    tm = ROW_TILE
    rows = pl.BlockSpec((None, tm, D), lambda b, t: (b, t, 0))
    return pl.pallas_call(
        _ffn_residual_kernel,
        grid=(B, R // tm),
        in_specs=[rows, rows, pl.BlockSpec((1, D), lambda b, t: (0, 0)),
                  pl.BlockSpec((None, None, 1, D), lambda b, t: (jnp.where(t >= n_lat_tiles, B, b), 5, 0, 0))],
        out_specs=rows,
        out_shape=jax.ShapeDtypeStruct((B, R, D), F32),
        compiler_params=_cparams("parallel", "parallel"),
        name="ffn_residual",
    )(xs, f, g_post.reshape(1, D), mod4)


def _set_ranks(aff_t, row0, n):
    a_t = aff_t[:, :, row0:row0 + n]
    return a_t, _ranks(a_t, jnp.swapaxes(a_t, 1, 2))[:, :, 0, :]


def _moe(h, aff_t, wg, wu, wdt, n_lat, n_ctx):
    B, _, D = h.shape
    E = aff_t.shape[1]
    a_l, rank_l = _set_ranks(aff_t, 0, n_lat)
    f = _experts(h, rank_l, a_l, wg, wu, wdt, CAPACITY_FACTOR * n_lat // E)
    if n_ctx:
        a_c, rank_c = _set_ranks(aff_t, n_lat, n_ctx)
        cap = CAPACITY_FACTOR * n_ctx // E
        slots = jnp.where(rank_c < cap, rank_c + cap * jnp.arange(B, dtype=F32)[:, None, None], -1.0)
        merge = lambda t: jnp.swapaxes(t, 0, 1).reshape(1, E, B * n_ctx)
        f_c = _experts(h[:, n_lat:].reshape(1, B * n_ctx, D), merge(slots), merge(a_c), wg, wu, wdt, B * cap)
        f = jnp.concatenate([f, f_c.reshape(B, n_ctx, D)], axis=1)
    return f


def _rope_tables(n_lat, n_ctx):
    inv = ROPE_BASE ** (-jnp.arange(ROPE_HALF, dtype=F32) / ROPE_HALF)
    pos = jnp.arange(n_lat)
    ang_r = (pos // GRID_W).astype(F32)[:, None] * inv[None, :]
    ang_c = (pos % GRID_W).astype(F32)[:, None] * inv[None, :]
    cos = jnp.concatenate([jnp.cos(ang_r)] * 2 + [jnp.cos(ang_c)] * 2, axis=1)
    sin = jnp.concatenate([-jnp.sin(ang_r), jnp.sin(ang_r), -jnp.sin(ang_c), jnp.sin(ang_c)], axis=1)
    cos = jnp.concatenate([jnp.tile(cos, (1, LANES // HEAD_DIM)), jnp.ones((n_ctx, LANES), F32)], axis=0)
    sin = jnp.concatenate([jnp.tile(sin, (1, LANES // HEAD_DIM)), jnp.zeros((n_ctx, LANES), F32)], axis=0)
    return cos, sin


def kernel(x, c, ctx, c_ctx, w_mod, b_mod, g_pre_mix, g_post_mix, g_pre_ffn, g_post_ffn, w_in, w_out, q_gain, k_gain, rpb, rk_w0, rk_w_up, rk_a0, rk_a_up, rk_g_up, rk_k_k, rk_k_a, rk_r_k, rk_lnx_w, rk_lnx_b, w_router, w_e_gate, w_e_up, w_e_down):
    B, n_lat, D = x.shape
    n_ctx = ctx.shape[1]
    T = n_lat + n_ctx
    depth = w_mod.shape[0]
    assert n_lat % ROW_TILE == 0 and n_ctx == ROW_TILE and n_lat % n_ctx == 0 and B < MOD_ROWS
    n_lat_tiles = n_lat // ROW_TILE

    cond = jnp.zeros((MOD_ROWS, D), F32).at[:B].set(c).at[B].set(c_ctx)
    mod = _modulation(cond, w_mod, b_mod)
    cos, sin = _rope_tables(n_lat, n_ctx)
    lane = np.arange(LANES)
    bd = jnp.asarray(lane[:, None] // HEAD_DIM == lane[None, :] // HEAD_DIM, BF16)

    xs = jnp.concatenate([x, ctx], axis=1)
    for i in range(depth):
        need_ctx = i < depth - 1
        n_rows = T if need_ctx else n_lat
        mod4 = mod[i].reshape(MOD_ROWS, 6, 1, D)
        qa, ka, va, rw, qn, kn, vn = _in_projection(
            xs, mod4, g_pre_mix[i], w_in[i], q_gain[i], k_gain[i], cos, sin, bd, n_lat_tiles)
        oa = _gqa_attention(qa, ka, va, n_lat, n_rows)
        on = _neighbourhood_attention(qn, kn, vn, _natten_bias(rpb[i], n_lat // GRID_W), n_lat, need_ctx)
        rk = dict(w0=rk_w0[i], w_up=rk_w_up[i], a0=rk_a0[i], a_up=rk_a_up[i], g_up=rk_g_up[i],
                  k_k=rk_k_k[i], k_a=rk_k_a[i], r_k=rk_r_k[i], lnx_w=rk_lnx_w[i], lnx_b=rk_lnx_b[i])
        rhat, yhat, gt, ht, gate, bonus = _rwkv_chunks(rw, rk)
        y0, y1 = _rwkv_scan(rhat, yhat, gt, ht, n_lat // CHUNK)
        xs = _out_projection(xs, oa, y0, y1, bonus, gate, on, rk, bd, w_out[i], g_post_mix[i], mod4,
                             n_lat_tiles, n_rows)
        h, aff_t = _router(xs, g_pre_ffn[i], mod4, w_router[i], n_lat_tiles)
        wg, wu = w_e_gate[i].astype(BF16), w_e_up[i].astype(BF16)
        wdt = jnp.swapaxes(w_e_down[i], 1, 2).astype(BF16)
        f = _moe(h, aff_t, wg, wu, wdt, n_lat, n_ctx if need_ctx else 0)
        xs = _ffn_residual(xs, f, g_post_ffn[i], mod4, n_lat_tiles)
    return xs
```

```python
import functools

import numpy as np
import jax
import jax.numpy as jnp
from jax import lax
from jax.experimental import pallas as pl
from jax.experimental.pallas import tpu as pltpu

F32 = jnp.float32
BF16 = jnp.bfloat16

HEAD_DIM = 64
GRID_W = 64
A_HEADS = 8
A_KV_HEADS = 2
B_HEADS = 4
C_HEADS = 4
A_WIDTH = A_HEADS * HEAD_DIM
A_KV_WIDTH = A_KV_HEADS * HEAD_DIM
B_WIDTH = B_HEADS * HEAD_DIM
C_WIDTH = C_HEADS * HEAD_DIM
DECAY_LORA = 64
AAA_LORA = 64
GATE_LORA = 128
RW_WIDTH = 3 * B_WIDTH + 2 * DECAY_LORA + 2 * AAA_LORA + GATE_LORA
ROPE_BASE = 10000.0
ROPE_HALF = HEAD_DIM // 4
WIN_ROWS = 8
WIN_COLS = 16
N_EXPERTS = 16
CAPACITY_FACTOR = 2
NORM_EPS = 1e-6
LNX_EPS = 64e-5
ATTN_SCALE = HEAD_DIM ** -0.5
V_EXT = 2 * HEAD_DIM
MASK_VALUE = -1e30

LANES = 128
ROW_TILE = 256
CHUNK = 64
RWKV_CHUNKS_PER_STEP = 4
NATTEN_ROWS_PER_ITER = 8
MOD_ROWS = 16
VMEM_LIMIT = 56 * 1024 * 1024


def _cparams(*sem):
    return pltpu.CompilerParams(dimension_semantics=sem, vmem_limit_bytes=VMEM_LIMIT)


def _dot(a, b):
    return jnp.dot(a, b, preferred_element_type=F32)


def _dot_nt(a, b):
    return lax.dot_general(a, b, (((1,), (1,)), ((), ())), preferred_element_type=F32)


def _dot_tn(a, b):
    return lax.dot_general(a, b, (((0,), (0,)), ((), ())), preferred_element_type=F32)


def _split2(x):
    hi = x.astype(BF16)
    lo = (x - hi.astype(F32)).astype(BF16)
    return hi, lo


def _split3(x):
    hi = x.astype(BF16)
    r1 = x - hi.astype(F32)
    mid = r1.astype(BF16)
    lo = (r1 - mid.astype(F32)).astype(BF16)
    return hi, mid, lo


def _head_sum(x, bd):
    outs = []
    for c in range(x.shape[1] // LANES):
        hi, lo = _split2(x[:, c * LANES:(c + 1) * LANES])
        outs.append(_dot(hi, bd) + _dot(lo, bd))
    return outs[0] if len(outs) == 1 else jnp.concatenate(outs, axis=1)


def _tile_lanes(t, width):
    reps = width // t.shape[1]
    return t if reps == 1 else jnp.concatenate([t] * reps, axis=1)


def _rms(x, g):
    return x * lax.rsqrt(jnp.mean(x * x, axis=-1, keepdims=True) + NORM_EPS) * g


def _sigmoid(x):
    return 1.0 / (1.0 + jnp.exp(-x))


def _segment_specs(width, n_lat_tiles, ctx_tile):
    return (pl.BlockSpec((None, ROW_TILE, width), lambda b, t: (b, jnp.minimum(t, n_lat_tiles - 1), 0)),
            pl.BlockSpec((None, ROW_TILE, width), lambda b, t: (b, ctx_tile, 0)))


def _segment_rows(lat_ref, ctx_ref, n_lat_tiles):
    return jnp.where(pl.program_id(1) >= n_lat_tiles, ctx_ref[...], lat_ref[...])


def _mod_kernel(s_ref, w_ref, b_ref, o_ref):
    s = s_ref[...]
    s = s * _sigmoid(s)
    o_ref[...] = _dot(s, w_ref[...]) + b_ref[...]


def _modulation(cond, w_mod, b_mod):
    L, D, N = w_mod.shape
    tn = 512
    return pl.pallas_call(
        _mod_kernel,
        grid=(L, N // tn),
        in_specs=[pl.BlockSpec((MOD_ROWS, D), lambda l, j: (0, 0)),
                  pl.BlockSpec((None, D, tn), lambda l, j: (l, 0, j)),
                  pl.BlockSpec((None, 1, tn), lambda l, j: (l, 0, j))],
        out_specs=pl.BlockSpec((None, MOD_ROWS, tn), lambda l, j: (l, 0, j)),
        out_shape=jax.ShapeDtypeStruct((L, MOD_ROWS, N), F32),
        compiler_params=_cparams("parallel", "parallel"),
        name="modulation",
    )(cond, w_mod, b_mod.reshape(L, 1, N))


def _rope(x, cos, sin):
    w = x.shape[1]
    lane = lax.broadcasted_iota(jnp.int32, x.shape, 1)
    upper = (lane // ROPE_HALF) % 2 == 1
    partner = jnp.where(upper, pltpu.roll(x, ROPE_HALF, 1), pltpu.roll(x, w - ROPE_HALF, 1))
    return x * _tile_lanes(cos, w) + partner * _tile_lanes(sin, w)


def _inproj_kernel(xl_ref, xc_ref, g_ref, shift_ref, scale_ref, wqa_ref, wka_ref, wva_ref, wrw_ref, wn_ref,
                   qg_ref, kg_ref, cos_ref, sin_ref, bd_ref, vone_ref,
                   qa_ref, ka_ref, va_ref, rw_ref, qn_ref, kn_ref, vn_ref, *, n_lat_tiles):
    x = _segment_rows(xl_ref, xc_ref, n_lat_tiles)
    h = _rms(x, g_ref[...]) * (1.0 + scale_ref[...]) + shift_ref[...]
    hb = h.astype(BF16)
    bd = bd_ref[...]
    cos, sin = cos_ref[...], sin_ref[...]

    def normed(w_ref, gain_ref):
        y = _dot(hb, w_ref[...])
        ms = _head_sum(y * y, bd) * (1.0 / HEAD_DIM)
        return _rope(y * lax.rsqrt(ms + NORM_EPS) * gain_ref[...], cos, sin)

    def put_heads(o_ref, y, width=HEAD_DIM):
        for hd in range(y.shape[1] // width):
            o_ref[hd] = y[:, hd * width:(hd + 1) * width].astype(o_ref.dtype)

    put_heads(qa_ref, normed(wqa_ref, qg_ref) * ATTN_SCALE)
    put_heads(ka_ref, normed(wka_ref, kg_ref))
    vone = vone_ref[...]
    put_heads(va_ref, _dot(hb, wva_ref[...]) + vone[:, :A_KV_HEADS * V_EXT], V_EXT)
    rw_ref[...] = _dot(hb, wrw_ref[...])
    yn = _dot(hb, wn_ref[...])
    put_heads(qn_ref, yn[:, :C_WIDTH] * ATTN_SCALE)
    put_heads(kn_ref, yn[:, C_WIDTH:2 * C_WIDTH])
    put_heads(vn_ref, yn[:, 2 * C_WIDTH:] + vone, V_EXT)


def _extend_values(w, heads):
    D = w.shape[0]
    w = jnp.pad(w.reshape(D, heads, HEAD_DIM), ((0, 0), (0, 0), (0, V_EXT - HEAD_DIM)))
    return w.reshape(D, heads * V_EXT)


def _in_projection(x_lat, x_ctx, ctx_tile, mod4, g_pre, w_in, q_gain, k_gain, cos, sin, bd, n_lat_tiles):
    B, _, D = x_lat.shape
    tm = ROW_TILE
    T = (n_lat_tiles + 1) * tm
    o = np.cumsum([0, A_WIDTH, A_KV_WIDTH, A_KV_WIDTH, RW_WIDTH, 2 * C_WIDTH, C_WIDTH])
    wb = w_in.astype(BF16)
    wqa, wka, wva, wrw, wqk, wvn = [wb[:, o[i]:o[i + 1]] for i in range(6)]
    ws = [wqa, wka, _extend_values(wva, A_KV_HEADS), wrw,
          jnp.concatenate([wqk, _extend_values(wvn, C_HEADS)], axis=1)]
    vone = jnp.asarray(np.arange(C_HEADS * V_EXT) % V_EXT == HEAD_DIM, F32).reshape(1, C_HEADS * V_EXT)
    const = lambda shape: pl.BlockSpec(shape, lambda b, t: (0,) * len(shape))
    mod_spec = lambda k: pl.BlockSpec(
        (None, None, 1, D), lambda b, t: (jnp.where(t >= n_lat_tiles, B, b), k, 0, 0))
    heads = lambda nh, width=HEAD_DIM: pl.BlockSpec((None, nh, tm, width), lambda b, t: (b, 0, t, 0))
    hshape = lambda nh, width=HEAD_DIM: jax.ShapeDtypeStruct((B, nh, T, width), BF16)
    return pl.pallas_call(
        functools.partial(_inproj_kernel, n_lat_tiles=n_lat_tiles),
        grid=(B, T // tm),
        in_specs=[*_segment_specs(D, n_lat_tiles, ctx_tile), const((1, D)), mod_spec(0), mod_spec(1)]
                 + [const(w.shape) for w in ws]
                 + [const((1, A_WIDTH)), const((1, A_KV_WIDTH)),
                    pl.BlockSpec((tm, LANES), lambda b, t: (t, 0)),
                    pl.BlockSpec((tm, LANES), lambda b, t: (t, 0)),
                    const((LANES, LANES)), const(vone.shape)],
        out_specs=[heads(A_HEADS), heads(A_KV_HEADS), heads(A_KV_HEADS, V_EXT),
                   pl.BlockSpec((None, tm, RW_WIDTH), lambda b, t: (b, t, 0)),
                   heads(C_HEADS), heads(C_HEADS), heads(C_HEADS, V_EXT)],
        out_shape=[hshape(A_HEADS), hshape(A_KV_HEADS), hshape(A_KV_HEADS, V_EXT),
                   jax.ShapeDtypeStruct((B, T, RW_WIDTH), F32),
                   hshape(C_HEADS), hshape(C_HEADS), hshape(C_HEADS, V_EXT)],
        compiler_params=_cparams("parallel", "parallel"),
        name="in_projection",
    )(x_lat, x_ctx, g_pre.reshape(1, D), mod4, mod4, *ws,
      jnp.tile(q_gain, A_HEADS).reshape(1, A_WIDTH), jnp.tile(k_gain, A_KV_HEADS).reshape(1, A_KV_WIDTH),
      cos, sin, bd, vone)


def _softmax_weights(s, m):
    return jnp.exp((s - m).astype(BF16))


def _normalised(ov):
    return ov[:, :HEAD_DIM] / ov[:, HEAD_DIM:HEAD_DIM + 1]


def _gqa_kernel(q_ref, k_ref, v_ref, o_ref, *, n_lat, n_lat_tiles):
    group = q_ref.shape[0]

    def attend(k, v):
        scores = [_dot_nt(q_ref[hd], k) for hd in range(group)]
        probs = [_softmax_weights(s, jnp.max(s, axis=-1, keepdims=True)) for s in scores]
        for hd, p in enumerate(probs):
            o_ref[:, hd * HEAD_DIM:(hd + 1) * HEAD_DIM] = _normalised(_dot(p, v)).astype(o_ref.dtype)

    @pl.when(pl.program_id(2) < n_lat_tiles)
    def _():
        attend(k_ref[...], v_ref[...])

    @pl.when(pl.program_id(2) >= n_lat_tiles)
    def _():
        attend(k_ref[n_lat:, :], v_ref[n_lat:, :])


def _gqa_attention(q, k, v, n_lat, n_rows):
    B, _, T, _ = q.shape
    tq = ROW_TILE
    group = A_HEADS // A_KV_HEADS
    kv_spec = lambda width: pl.BlockSpec((None, None, T, width), lambda b, g, t: (b, g, 0, 0))
    return pl.pallas_call(
        functools.partial(_gqa_kernel, n_lat=n_lat, n_lat_tiles=n_lat // tq),
        grid=(B, A_KV_HEADS, n_rows // tq),
        in_specs=[pl.BlockSpec((None, group, tq, HEAD_DIM), lambda b, g, t: (b, g, t, 0)),
                  kv_spec(HEAD_DIM), kv_spec(V_EXT)],
        out_specs=pl.BlockSpec((None, tq, group * HEAD_DIM), lambda b, g, t: (b, t, g)),
        out_shape=jax.ShapeDtypeStruct((B, n_rows, A_WIDTH), BF16),
        compiler_params=_cparams("parallel", "parallel", "parallel"),
        name="gqa_attention",
    )(q, k, v)


def _natten_kernel(q_ref, k_ref, v_ref, bias_ref, o_ref, *, n_lat, rows, need_ctx):
    win = WIN_ROWS * GRID_W
    for hd in range(C_HEADS):
        kc = k_ref[hd, n_lat:, :]
        vc = v_ref[hd, n_lat:, :]

        def row_group(g, carry, hd=hd, kc=kc, vc=vc):
            scores = []
            for u in range(NATTEN_ROWS_PER_ITER):
                r = g * NATTEN_ROWS_PER_ITER + u
                rs = jnp.clip(r - WIN_ROWS // 2, 0, rows - WIN_ROWS)
                q = q_ref[hd, pl.ds(pl.multiple_of(r * GRID_W, GRID_W), GRID_W), :]
                kw = k_ref[hd, pl.ds(pl.multiple_of(rs * GRID_W, GRID_W), win), :]
                scores.append((r, rs, _dot_nt(q, kw) + bias_ref[r - rs, hd], _dot_nt(q, kc)))
            probs = []
            for r, rs, s_w, s_c in scores:
                m = jnp.maximum(jnp.max(s_w, axis=-1, keepdims=True), jnp.max(s_c, axis=-1, keepdims=True))
                probs.append((r, rs, _softmax_weights(s_w, m), _softmax_weights(s_c, m)))
            for r, rs, p_w, p_c in probs:
                vw = v_ref[hd, pl.ds(pl.multiple_of(rs * GRID_W, GRID_W), win), :]
                o = _normalised(_dot(p_w, vw) + _dot(p_c, vc))
                o_ref[pl.ds(pl.multiple_of(r * GRID_W, GRID_W), GRID_W),
                      hd * HEAD_DIM:(hd + 1) * HEAD_DIM] = o.astype(o_ref.dtype)
            return carry

        lax.fori_loop(0, rows // NATTEN_ROWS_PER_ITER, row_group, 0)
        if need_ctx:
            s = _dot_nt(q_ref[hd, n_lat:, :], kc)
            o = _normalised(_dot(_softmax_weights(s, jnp.max(s, axis=-1, keepdims=True)), vc))
            o_ref[n_lat:, hd * HEAD_DIM:(hd + 1) * HEAD_DIM] = o.astype(o_ref.dtype)


def _natten_bias(rpb, rows):
    off = np.arange(WIN_ROWS)[:, None, None]
    jr = np.arange(WIN_ROWS)[None, :, None]
    row_sel = (np.arange(2 * WIN_ROWS - 1)[None, None, :] == jr - off + WIN_ROWS - 1)
    qc = np.arange(GRID_W)[:, None, None]
    kc = np.arange(GRID_W)[None, :, None]
    col_start = np.clip(qc - WIN_COLS // 2, 0, GRID_W - WIN_COLS)
    valid = (kc >= col_start) & (kc < col_start + WIN_COLS)
    col_sel = (np.arange(2 * WIN_COLS - 1)[None, None, :] == kc - qc + WIN_COLS - 1) & valid
    bias = jnp.einsum("hrc,ojr,qkc->ohqjk", rpb.astype(F32), jnp.asarray(row_sel, F32), jnp.asarray(col_sel, F32),
                      precision=lax.Precision.HIGHEST)
    bias = bias + jnp.asarray(np.where(valid[None, None, :, None, :, 0], 0.0, MASK_VALUE), F32)
    return bias.reshape(WIN_ROWS, C_HEADS, GRID_W, WIN_ROWS * GRID_W)


def _neighbourhood_attention(q, k, v, bias, n_lat, need_ctx):
    B, _, T, _ = q.shape
    n_rows = T if need_ctx else n_lat
    full = lambda width: pl.BlockSpec((None, C_HEADS, T, width), lambda b: (b, 0, 0, 0))
    return pl.pallas_call(
        functools.partial(_natten_kernel, n_lat=n_lat, rows=n_lat // GRID_W, need_ctx=need_ctx),
        grid=(B,),
        in_specs=[full(HEAD_DIM), full(HEAD_DIM), full(V_EXT), pl.BlockSpec(bias.shape, lambda b: (0, 0, 0, 0))],
        out_specs=pl.BlockSpec((None, n_rows, C_WIDTH), lambda b: (b, 0, 0)),
        out_shape=jax.ShapeDtypeStruct((B, n_rows, C_WIDTH), BF16),
        compiler_params=_cparams("parallel"),
        name="neighbourhood_attention",
    )(q, k, v, bias)


def _rwkv_chunk_kernel(rw_ref, w0_ref, a0_ref, lora_ref, kk_ref, ka_ref, rk_ref, bdw_ref,
                       rhat_ref, yhat_ref, gt_ref, ht_ref, gate_ref, bonus_ref):
    C, W, NH = CHUNK, B_WIDTH, B_HEADS
    S = NH * C
    R = rw_ref.shape[0]
    rw = rw_ref[...]
    r, k, v = rw[:, :W], rw[:, W:2 * W], rw[:, 2 * W:3 * W]
    wl = rw[:, 3 * W:3 * W + 2 * DECAY_LORA]
    al = rw[:, 3 * W + 2 * DECAY_LORA:3 * W + 2 * DECAY_LORA + 2 * AAA_LORA]
    gl = rw[:, 3 * W + 2 * DECAY_LORA + 2 * AAA_LORA:]
    bdw = bdw_ref[...]

    def head_sum(x):
        hi, lo = _split2(x)
        p = _dot(jnp.concatenate([hi, lo], axis=0), bdw)
        return p[:R] + p[R:]

    kkv = k * kk_ref[...]
    kkv = kkv / jnp.maximum(jnp.sqrt(head_sum(kkv * kkv)), 1e-12)
    lora = _dot(jnp.concatenate([jnp.tanh(wl), al, _sigmoid(gl)], axis=1), lora_ref[...])
    gate_ref[...] = lora[:, 4 * W:]
    lws, asigs = [], []
    for d in range(2):
        w = w0_ref[d] + lora[:, d * W:(d + 1) * W]
        w = -(jnp.maximum(-w, 0.0) + jnp.log(1.0 + jnp.exp(-jnp.abs(w)))) - 0.5
        lws.append(-jnp.exp(w))
        asigs.append(_sigmoid(a0_ref[d] + lora[:, (2 + d) * W:(3 + d) * W]))

    prow = lax.broadcasted_iota(jnp.int32, (R, R), 0)
    pcol = lax.broadcasted_iota(jnp.int32, (R, R), 1)
    tri = ((prow >= pcol) & (prow // C == pcol // C)).astype(BF16)
    cs = _dot(tri, jnp.concatenate([p for lw in lws for p in _split3(lw)], axis=1))
    prefix = [cs[:, 3 * d * W:(3 * d + 1) * W] + cs[:, (3 * d + 1) * W:(3 * d + 2) * W]
              + cs[:, (3 * d + 2) * W:(3 * d + 3) * W] for d in range(2)]

    srow = lax.broadcasted_iota(jnp.int32, (S, S), 0)
    scol = lax.broadcasted_iota(jnp.int32, (S, S), 1)
    same = (srow // C) == (scol // HEAD_DIM)
    trow, tcol = srow % C, scol % C
    eye_s = srow == scol
    rrow = lax.broadcasted_iota(jnp.int32, (S, C), 0) % C
    rcol = lax.broadcasted_iota(jnp.int32, (S, C), 1)
    eye_c = lax.broadcasted_iota(jnp.int32, (C, W), 0) == lax.broadcasted_iota(jnp.int32, (C, W), 1) % HEAD_DIM

    def stack(x):
        return jnp.where(same, jnp.concatenate([x] * NH, axis=0), 0.0)

    def unstack(x):
        out = x[:C]
        for hd in range(1, NH):
            out = out + x[hd * C:(hd + 1) * C]
        return out

    kds = [k * (1.0 + (asig - 1.0) * ka_ref[...]) for asig in asigs]
    bonus_ref[...] = head_sum(r * (kds[0] + kds[1]) * rk_ref[...]) * v
    class Chain:
        pass

    chains = []
    for ci in range(R // C):
        for d in range(2):
            ch = Chain()
            ch.d, ch.ci, ch.rows = d, ci, slice(ci * C, (ci + 1) * C)
            chains.append(ch)

    for ch in chains:
        d, rows = ch.d, ch.rows
        lw, kd, b, a = lws[d][rows], kds[d][rows], (kkv * asigs[d])[rows], -kkv[rows]
        ch.v = v[rows]
        pre = prefix[d][rows]
        ch.tot = tot = pre[C - 1:C, :]
        cum_incl = pre if d == 0 else tot - pre + lw
        cum_excl = cum_incl - lw
        rho = 0.5 * tot
        e_in = jnp.exp(rho - cum_incl)
        e_end = jnp.exp(tot - cum_incl)
        ch.e_rho = jnp.exp(rho)
        ch.xa = stack(a * jnp.exp(cum_excl - rho))
        ch.xr = stack(r[rows] * jnp.exp(cum_incl - rho))
        ch.ends = jnp.concatenate([stack(b * e_end), stack(kd * e_end)], axis=0).astype(BF16)
        ch.m = _dot_nt(jnp.concatenate([ch.xa, ch.xr], axis=0).astype(BF16),
                       jnp.concatenate([(b * e_in).astype(BF16)] * NH + [(kd * e_in).astype(BF16)], axis=0))

    for ch in chains:
        if ch.d == 0:
            before_s, upto_s, before_r, upto_r = trow > tcol, trow >= tcol, rrow > rcol, rrow >= rcol
        else:
            before_s, upto_s, before_r, upto_r = trow < tcol, trow <= tcol, rrow < rcol, rrow <= rcol
        m = ch.m
        l_ab = jnp.where(same & before_s, m[:S, :S], 0.0)
        ch.a_rb = jnp.where(same & upto_s, m[S:, :S], 0.0).astype(BF16)
        ch.kv = jnp.concatenate([jnp.where(before_r, m[:S, S:], 0.0), jnp.where(upto_r, m[S:, S:], 0.0)],
                                axis=0).astype(BF16)
        lb = l_ab.astype(BF16)
        ch.lp = _dot(lb, lb)
        ch.tinv = jnp.where(eye_s, 1.0, l_ab)
    for _ in range(int(np.log2(C)) - 2):
        for ch in chains:
            p = _dot(jnp.concatenate([ch.tinv, ch.lp], axis=0).astype(BF16), ch.lp.astype(BF16))
            ch.tinv = ch.tinv + p[:S]
            ch.lp = p[S:]
    for ch in chains:
        ch.tinv = ch.tinv + _dot(ch.tinv.astype(BF16), ch.lp.astype(BF16))
    for ch in chains:
        ch.wv = _dot(ch.kv, ch.v.astype(BF16))
    for ch in chains:
        w1 = jnp.where(same, ch.wv[:S], 0.0)
        ch.z = _dot(ch.tinv.astype(BF16), jnp.concatenate([ch.xa, w1], axis=1).astype(BF16))
    for ch in chains:
        rz = _dot(ch.a_rb, ch.z.astype(BF16))
        rhat_ref[ch.d, ch.rows] = unstack((ch.xr + rz[:, :W]) * ch.e_rho)
        yhat_ref[ch.d, ch.rows] = unstack(rz[:, W:] + jnp.where(same, ch.wv[S:], 0.0))
    for ch in chains:
        z = ch.z
        vals = jnp.concatenate([jnp.concatenate([z[:, :W] * ch.e_rho, z[:, W:]], axis=1),
                                jnp.concatenate([jnp.zeros((S, W), F32), stack(ch.v)], axis=1)], axis=0)
        gh = _dot_tn(ch.ends, vals.astype(BF16))
        gt_ref[ch.d, ch.ci] = unstack(gh[:, :W]) + jnp.where(eye_c, jnp.exp(ch.tot), 0.0)
        ht_ref[ch.d, ch.ci] = unstack(gh[:, W:])


def _rwkv_chunks(rw, p):
    B, T, _ = rw.shape
    C, W = CHUNK, B_WIDTH
    nc = T // C
    per = RWKV_CHUNKS_PER_STEP
    assert C == HEAD_DIM and nc % per == 0
    const = lambda shape: pl.BlockSpec(shape, lambda b, c: (0,) * len(shape))
    rows = pl.BlockSpec((None, 2, per * C, W), lambda b, c: (b, 0, c, 0))
    mats = pl.BlockSpec((None, 2, per, C, W), lambda b, c: (b, 0, c, 0, 0))
    flat = pl.BlockSpec((None, per * C, W), lambda b, c: (b, c, 0))
    vec = lambda a: a.reshape(1, W)
    lora_w = jnp.zeros((2 * DECAY_LORA + 2 * AAA_LORA + GATE_LORA, 5 * W), F32)
    for d in range(2):
        lora_w = lora_w.at[d * DECAY_LORA:(d + 1) * DECAY_LORA, d * W:(d + 1) * W].set(p["w_up"][d])
        lora_w = lora_w.at[2 * DECAY_LORA + d * AAA_LORA:2 * DECAY_LORA + (d + 1) * AAA_LORA,
                           (2 + d) * W:(3 + d) * W].set(p["a_up"][d])
    lora_w = lora_w.at[2 * DECAY_LORA + 2 * AAA_LORA:, 4 * W:].set(p["g_up"])
    lane = np.arange(W)
    bdw = jnp.asarray(lane[:, None] // HEAD_DIM == lane[None, :] // HEAD_DIM, BF16)
    return pl.pallas_call(
        _rwkv_chunk_kernel,
        grid=(B, nc // per),
        in_specs=[pl.BlockSpec((None, per * C, RW_WIDTH), lambda b, c: (b, c, 0)),
                  const((2, 1, W)), const((2, 1, W)), const(lora_w.shape),
                  const((1, W)), const((1, W)), const((1, W)), const((W, W))],
        out_specs=[rows, rows, mats, mats, flat, flat],
        out_shape=[jax.ShapeDtypeStruct((B, 2, T, W), F32), jax.ShapeDtypeStruct((B, 2, T, W), F32),
                   jax.ShapeDtypeStruct((B, 2, nc, C, W), F32), jax.ShapeDtypeStruct((B, 2, nc, C, W), F32),
                   jax.ShapeDtypeStruct((B, T, W), F32), jax.ShapeDtypeStruct((B, T, W), F32)],
        compiler_params=_cparams("parallel", "parallel"),
        name="rwkv_chunks",
    )(rw, p["w0"].reshape(2, 1, W), p["a0"].reshape(2, 1, W), lora_w,
      vec(p["k_k"]), vec(p["k_a"]), vec(p["r_k"]), bdw)


def _rwkv_scan_kernel(rhat0_ref, yhat0_ref, gt0_ref, ht0_ref, rhat1_ref, yhat1_ref, gt1_ref, ht1_ref,
                      y0_ref, y1_ref, st_ref):
    @pl.when(pl.program_id(0) == 0)
    def _():
        st_ref[...] = jnp.zeros_like(st_ref)

    W = B_WIDTH
    same = (lax.broadcasted_iota(jnp.int32, (W, W), 0) // HEAD_DIM
            == lax.broadcasted_iota(jnp.int32, (W, W), 1) // HEAD_DIM)
    expand = lambda x: jnp.where(same, jnp.concatenate([x] * B_HEADS, axis=0), 0.0)
    dirs = ((rhat0_ref, yhat0_ref, gt0_ref, ht0_ref, y0_ref), (rhat1_ref, yhat1_ref, gt1_ref, ht1_ref, y1_ref))
    for b in range(st_ref.shape[1]):
        for d, (rhat_ref, yhat_ref, gt_ref, ht_ref, y_ref) in enumerate(dirs):
            stb = st_ref[d, b].astype(BF16)
            y_ref[b] = _dot(rhat_ref[b].astype(BF16), stb) + yhat_ref[b]
            st_ref[d, b] = _dot(expand(gt_ref[b]).astype(BF16), stb) + expand(ht_ref[b])


def _rwkv_scan(rhat, yhat, gt, ht, n_lat_chunks):
    B, _, T, W = rhat.shape
    C = CHUNK
    nc = T // C
    chunk = (lambda s: (s + n_lat_chunks) % nc, lambda s: nc - 1 - s)
    rows = lambda d: pl.BlockSpec((B, None, C, W), lambda s: (0, d, chunk[d](s), 0))
    mats = lambda d: pl.BlockSpec((B, None, None, C, W), lambda s: (0, d, chunk[d](s), 0, 0))
    outs = lambda d: pl.BlockSpec((B, C, W), lambda s: (0, chunk[d](s), 0))
    return pl.pallas_call(
        _rwkv_scan_kernel,
        grid=(nc,),
        in_specs=[rows(0), rows(0), mats(0), mats(0), rows(1), rows(1), mats(1), mats(1)],
        out_specs=[outs(0), outs(1)],
        out_shape=[jax.ShapeDtypeStruct((B, T, W), F32)] * 2,
        scratch_shapes=[pltpu.VMEM((2, B, W, W), F32)],
        compiler_params=_cparams("arbitrary"),
        name="rwkv_scan",
    )(rhat, yhat, gt, ht, rhat, yhat, gt, ht)


def _outproj_kernel(xl_ref, xc_ref, oa_ref, y0_ref, y1_ref, bonus_ref, gate_ref, on_ref, lnw_ref, lnb_ref, bd_ref,
                    wa_ref, wb_ref, wn_ref, g_ref, mgate_ref, o_ref, *, n_lat_tiles):
    y = y0_ref[...] + y1_ref[...]
    bd = bd_ref[...]
    mu = _head_sum(y, bd) * (1.0 / HEAD_DIM)
    yc = y - mu
    var = _head_sum(yc * yc, bd) * (1.0 / HEAD_DIM)
    ob = (yc * lax.rsqrt(var + LNX_EPS) * lnw_ref[...] + lnb_ref[...] + bonus_ref[...]) * gate_ref[...]
    out = _dot(oa_ref[...], wa_ref[...]) + _dot(ob.astype(BF16), wb_ref[...]) + _dot(on_ref[...], wn_ref[...])
    o_ref[...] = _segment_rows(xl_ref, xc_ref, n_lat_tiles) + mgate_ref[...] * _rms(out, g_ref[...])


def _out_projection(x_lat, x_ctx, ctx_tile, oa, y0, y1, bonus, gate, on, p, bd, w_out, g_post, mod4,
                    n_lat_tiles, n_rows):
    B, _, D = x_lat.shape
    tm = ROW_TILE
    W = B_WIDTH
    wb = w_out.astype(BF16)
    const = lambda shape: pl.BlockSpec(shape, lambda b, t: (0,) * len(shape))
    rows = lambda width: pl.BlockSpec((None, tm, width), lambda b, t: (b, t, 0))
    return pl.pallas_call(
        functools.partial(_outproj_kernel, n_lat_tiles=n_lat_tiles),
        grid=(B, n_rows // tm),
        in_specs=[*_segment_specs(D, n_lat_tiles, ctx_tile),
                  rows(A_WIDTH), rows(W), rows(W), rows(W), rows(W), rows(C_WIDTH),
                  const((1, W)), const((1, W)), const((LANES, LANES)),
                  const((A_WIDTH, D)), const((W, D)), const((C_WIDTH, D)), const((1, D)),
                  pl.BlockSpec((None, None, 1, D), lambda b, t: (jnp.where(t >= n_lat_tiles, B, b), 2, 0, 0))],
        out_specs=rows(D),
        out_shape=jax.ShapeDtypeStruct((B, n_rows, D), F32),
        compiler_params=_cparams("parallel", "parallel"),
        name="out_projection",
    )(x_lat, x_ctx, oa, y0, y1, bonus, gate, on, p["lnx_w"].reshape(1, W), p["lnx_b"].reshape(1, W), bd,
      wb[:A_WIDTH], wb[A_WIDTH:A_WIDTH + W], wb[A_WIDTH + W:], g_post.reshape(1, D), mod4)


def _router_kernel(x_ref, g_ref, shift_ref, scale_ref, wr_ref, h_ref, aff_ref):
    h = _rms(x_ref[...], g_ref[...]) * (1.0 + scale_ref[...]) + shift_ref[...]
    h_ref[...] = h.astype(h_ref.dtype)
    logits = _dot_nt(wr_ref[...], h)
    p = jnp.exp(logits - jnp.max(logits, axis=0, keepdims=True))
    aff_ref[...] = p / jnp.sum(p, axis=0, keepdims=True)


def _router(xs, g_pre, mod4, w_router, n_lat_tiles):
    B, R, D = xs.shape
    tm = ROW_TILE
    E = w_router.shape[1]
    mod_spec = lambda k: pl.BlockSpec(
        (None, None, 1, D), lambda b, t: (jnp.where(t >= n_lat_tiles, B, b), k, 0, 0))
    return pl.pallas_call(
        _router_kernel,
        grid=(B, R // tm),
        in_specs=[pl.BlockSpec((None, tm, D), lambda b, t: (b, t, 0)),
                  pl.BlockSpec((1, D), lambda b, t: (0, 0)), mod_spec(3), mod_spec(4),
                  pl.BlockSpec((E, D), lambda b, t: (0, 0))],
        out_specs=[pl.BlockSpec((None, tm, D), lambda b, t: (b, t, 0)),
                   pl.BlockSpec((None, E, tm), lambda b, t: (b, 0, t))],
        out_shape=[jax.ShapeDtypeStruct((B, R, D), BF16), jax.ShapeDtypeStruct((B, E, R), F32)],
        compiler_params=_cparams("parallel", "parallel"),
        name="moe_router",
    )(xs, g_pre.reshape(1, D), mod4, mod4, w_router.T)


def _column(ref, e):
    lane = lax.broadcasted_iota(jnp.int32, ref.shape, 1)
    return jnp.sum(jnp.where(lane == e, ref[...], 0.0), axis=1, keepdims=True)


RANK_BLOCK = 256


def _rank_partials(a_row, a_col):
    n = a_row.shape[-1]
    tj = min(n, RANK_BLOCK)
    as_count = lambda mask: mask.astype(F32).astype(BF16)

    def partial(jc):
        lo, hi = jc * tj, (jc + 1) * tj
        aj = a_col[lo:hi, :]
        ai = a_row[:, lo:hi]
        j_first = (lax.broadcasted_iota(jnp.int32, (tj, tj), 0) < lax.broadcasted_iota(jnp.int32, (tj, tj), 1))
        parts = [as_count(aj > a_row[:, :lo])] if lo else []
        parts.append(as_count((aj > ai) | ((aj == ai) & j_first)))
        if hi < n:
            parts.append(as_count(aj >= a_row[:, hi:]))
        return _dot(jnp.ones((8, tj), BF16), parts[0] if len(parts) == 1 else jnp.concatenate(parts, axis=1))

    return [functools.partial(partial, jc) for jc in range(n // tj)]


def _token_ranks(a_row, a_col):
    cnt = None
    for thunk in _rank_partials(a_row, a_col):
        cnt = thunk() if cnt is None else cnt + thunk()
    return cnt[0:1, :]


def _rank_kernel(arow_ref, acol_ref, rank_ref):
    rank_ref[...] = _token_ranks(arow_ref[...], _column(acol_ref, pl.program_id(1)))


def _ranks(aff_t, aff_c):
    B, E, n = aff_t.shape
    return pl.pallas_call(
        _rank_kernel,
        grid=(B, E),
        in_specs=[pl.BlockSpec((None, None, 1, n), lambda b, e: (b, e, 0, 0)),
                  pl.BlockSpec((None, n, E), lambda b, e: (b, 0, 0))],
        out_specs=pl.BlockSpec((None, None, 1, n), lambda b, e: (b, e, 0, 0)),
        out_shape=jax.ShapeDtypeStruct((B, E, 1, n), F32),
        compiler_params=_cparams("parallel", "parallel"),
        name="moe_ranks",
    )(aff_t.reshape(B, E, 1, n), aff_c)


def _expert_kernel(h_ref, aff_ref, rank_src_ref, wg_ref, wu_ref, wd_ref, f_ref, yet_ref, ye_ref, ft_ref,
                   *rank_scratch, cap):
    n = h_ref.shape[0]
    E = aff_ref.shape[0]
    e = pl.program_id(1)
    if rank_scratch:
        rank_ref, = rank_scratch

        @pl.when(e == 0)
        def _():
            rank_ref[0:1, :] = _token_ranks(aff_ref[0:1, :], _column(rank_src_ref, 0))
    else:
        rank_ref = rank_src_ref
    slot = lax.broadcasted_iota(jnp.int32, (cap, n), 0).astype(F32)
    pick = (rank_ref[pl.ds(e, 1), :] == slot).astype(F32).astype(BF16)
    nxt = jnp.minimum(e + 1, E - 1)
    pieces = _rank_partials(aff_ref[pl.ds(nxt, 1), :], _column(rank_src_ref, nxt)) if rank_scratch else []
    counts = []

    def rank_pieces(share):
        for _ in range(-(-len(pieces) // 4) if share else len(pieces)):
            if pieces:
                counts.append(pieces.pop(0)())

    rank_pieces(True)
    xe = _dot(pick, h_ref[...]).astype(BF16)
    rank_pieces(True)
    gte = _dot(xe, wg_ref[...])
    rank_pieces(True)
    hid = (gte * _sigmoid(gte) * _dot(xe, wu_ref[...])).astype(BF16)
    rank_pieces(False)
    ye_ref[...] = _dot(hid, wd_ref[...])
    yet_ref[:, pl.ds(pl.multiple_of(e * cap, cap), cap)] = ye_ref[...].T.astype(BF16)
    if counts:
        rank_ref[pl.ds(nxt, 1), :] = functools.reduce(lambda a, b: a + b, counts)[0:1, :]

    @pl.when(e == E - 1)
    def _():
        tn = min(n, ROW_TILE)
        slot_c = lax.broadcasted_iota(jnp.int32, (cap, tn), 0).astype(F32)
        for c in range(n // tn):
            cols = slice(c * tn, (c + 1) * tn)
            put = jnp.concatenate(
                [jnp.where(rank_ref[x:x + 1, cols] == slot_c, aff_ref[x:x + 1, cols], 0.0).astype(BF16)
                 for x in range(E)], axis=0)
            ft_ref[...] = _dot(yet_ref[...], put)
            f_ref[cols, :] = ft_ref[...].T


def _experts(h, aff, slots, weights, layer, cap):
    S, E, n = aff.shape
    D = h.shape[-1]
    F = weights[0].shape[-1]
    assert cap % LANES == 0
    rows = pl.BlockSpec((None, E, n), lambda s, e: (s, 0, 0))
    if slots is None:
        rank_src, rank_spec = jnp.swapaxes(aff, 1, 2), pl.BlockSpec((None, n, E), lambda s, e: (s, 0, 0))
    else:
        rank_src, rank_spec = slots, rows
    weight = lambda a, b: pl.BlockSpec((None, None, a, b), lambda s, e: (layer, e, 0, 0))
    return pl.pallas_call(
        functools.partial(_expert_kernel, cap=cap),
        grid=(S, E),
        in_specs=[pl.BlockSpec((None, n, D), lambda s, e: (s, 0, 0)), rows, rank_spec,
                  weight(D, F), weight(D, F), weight(F, D)],
        out_specs=pl.BlockSpec((None, n, D), lambda s, e: (s, 0, 0)),
        out_shape=jax.ShapeDtypeStruct((S, n, D), F32),
        scratch_shapes=[pltpu.VMEM((D, E * cap), BF16), pltpu.VMEM((cap, D), F32),
                        pltpu.VMEM((D, min(n, ROW_TILE)), F32)]
                       + ([pltpu.VMEM((E, n), F32)] if slots is None else []),
        compiler_params=_cparams("parallel", "arbitrary"),
        name="moe_experts",
    )(h, aff, rank_src, *weights)


def _ffn_residual_kernel(x_ref, fl_ref, fc_ref, g_ref, mgate_ref, o_ref, *, n_lat_tiles):
    f = _segment_rows(fl_ref, fc_ref, n_lat_tiles)
    o_ref[...] = x_ref[...] + mgate_ref[...] * _rms(f, g_ref[...])


def _ffn_residual(xs, f_lat, f_ctx, g_post, mod4, n_lat_tiles):
    B, R, D = xs.shape
    tm = ROW_TILE
    rows = pl.BlockSpec((None, tm, D), lambda b, t: (b, t, 0))
    return pl.pallas_call(
        functools.partial(_ffn_residual_kernel, n_lat_tiles=n_lat_tiles),
        grid=(B, R // tm),
        in_specs=[rows, *_segment_specs(D, n_lat_tiles, 0), pl.BlockSpec((1, D), lambda b, t: (0, 0)),
                  pl.BlockSpec((None, None, 1, D), lambda b, t: (jnp.where(t >= n_lat_tiles, B, b), 5, 0, 0))],
        out_specs=rows,
        out_shape=jax.ShapeDtypeStruct((B, R, D), F32),
        compiler_params=_cparams("parallel", "parallel"),
        name="ffn_residual",
    )(xs, f_lat, f_lat if f_ctx is None else f_ctx, g_post.reshape(1, D), mod4)


def _moe(h, aff_t, weights, layer, n_lat, n_ctx):
    B, _, D = h.shape
    E = aff_t.shape[1]
    f_l = _experts(h, aff_t[:, :, :n_lat], None, weights, layer, CAPACITY_FACTOR * n_lat // E)
    if not n_ctx:
        return f_l, None
    a_c = aff_t[:, :, n_lat:]
    rank_c = _ranks(a_c, jnp.swapaxes(a_c, 1, 2))[:, :, 0, :]
    cap = CAPACITY_FACTOR * n_ctx // E
    slots = jnp.where(rank_c < cap, rank_c + cap * jnp.arange(B, dtype=F32)[:, None, None], -1.0)
    merge = lambda t: jnp.swapaxes(t, 0, 1).reshape(1, E, B * n_ctx)
    f_c = _experts(h[:, n_lat:].reshape(1, B * n_ctx, D), merge(a_c), merge(slots), weights, layer, B * cap)
    return f_l, f_c.reshape(B, n_ctx, D)


def _rope_tables(n_lat, n_ctx):
    inv = ROPE_BASE ** (-jnp.arange(ROPE_HALF, dtype=F32) / ROPE_HALF)
    pos = jnp.arange(n_lat)
    ang_r = (pos // GRID_W).astype(F32)[:, None] * inv[None, :]
    ang_c = (pos % GRID_W).astype(F32)[:, None] * inv[None, :]
    cos = jnp.concatenate([jnp.cos(ang_r)] * 2 + [jnp.cos(ang_c)] * 2, axis=1)
    sin = jnp.concatenate([-jnp.sin(ang_r), jnp.sin(ang_r), -jnp.sin(ang_c), jnp.sin(ang_c)], axis=1)
    cos = jnp.concatenate([jnp.tile(cos, (1, LANES // HEAD_DIM)), jnp.ones((n_ctx, LANES), F32)], axis=0)
    sin = jnp.concatenate([jnp.tile(sin, (1, LANES // HEAD_DIM)), jnp.zeros((n_ctx, LANES), F32)], axis=0)
    return cos, sin


def kernel(x, c, ctx, c_ctx, w_mod, b_mod, g_pre_mix, g_post_mix, g_pre_ffn, g_post_ffn, w_in, w_out, q_gain, k_gain, rpb, rk_w0, rk_w_up, rk_a0, rk_a_up, rk_g_up, rk_k_k, rk_k_a, rk_r_k, rk_lnx_w, rk_lnx_b, w_router, w_e_gate, w_e_up, w_e_down):
    B, n_lat, D = x.shape
    n_ctx = ctx.shape[1]
    T = n_lat + n_ctx
    depth = w_mod.shape[0]
    assert n_lat % ROW_TILE == 0 and n_ctx == ROW_TILE and n_lat % n_ctx == 0 and B < MOD_ROWS
    n_lat_tiles = n_lat // ROW_TILE

    cond = jnp.zeros((MOD_ROWS, D), F32).at[:B].set(c).at[B].set(c_ctx)
    mod = _modulation(cond, w_mod, b_mod)
    cos, sin = _rope_tables(n_lat, n_ctx)
    lane = np.arange(LANES)
    bd = jnp.asarray(lane[:, None] // HEAD_DIM == lane[None, :] // HEAD_DIM, BF16)

    weights = (w_e_gate.astype(BF16), w_e_up.astype(BF16), w_e_down.astype(BF16))
    x_lat, x_ctx, ctx_tile = x, ctx, 0
    for i in range(depth):
        need_ctx = i < depth - 1
        n_rows = T if need_ctx else n_lat
        mod4 = mod[i].reshape(MOD_ROWS, 6, 1, D)
        qa, ka, va, rw, qn, kn, vn = _in_projection(
            x_lat, x_ctx, ctx_tile, mod4, g_pre_mix[i], w_in[i], q_gain[i], k_gain[i], cos, sin, bd, n_lat_tiles)
        oa = _gqa_attention(qa, ka, va, n_lat, n_rows)
        on = _neighbourhood_attention(qn, kn, vn, _natten_bias(rpb[i], n_lat // GRID_W), n_lat, need_ctx)
        rk = dict(w0=rk_w0[i], w_up=rk_w_up[i], a0=rk_a0[i], a_up=rk_a_up[i], g_up=rk_g_up[i],
                  k_k=rk_k_k[i], k_a=rk_k_a[i], r_k=rk_r_k[i], lnx_w=rk_lnx_w[i], lnx_b=rk_lnx_b[i])
        rhat, yhat, gt, ht, gate, bonus = _rwkv_chunks(rw, rk)
        y0, y1 = _rwkv_scan(rhat, yhat, gt, ht, n_lat // CHUNK)
        xs = _out_projection(x_lat, x_ctx, ctx_tile, oa, y0, y1, bonus, gate, on, rk, bd, w_out[i], g_post_mix[i], mod4,
                             n_lat_tiles, n_rows)
        h, aff_t = _router(xs, g_pre_ffn[i], mod4, w_router[i], n_lat_tiles)
        f_lat, f_ctx = _moe(h, aff_t, weights, i, n_lat, n_ctx if need_ctx else 0)
        x_lat = x_ctx = _ffn_residual(xs, f_lat, f_ctx, g_post_ffn[i], mod4, n_lat_tiles)
        ctx_tile = n_lat_tiles
    return x_lat
```

```python
import functools

import numpy as np
import jax
import jax.numpy as jnp
from jax import lax
from jax.experimental import pallas as pl
from jax.experimental.pallas import tpu as pltpu

F32 = jnp.float32
BF16 = jnp.bfloat16

HEAD_DIM = 64
GRID_W = 64
A_HEADS = 8
A_KV_HEADS = 2
B_HEADS = 4
C_HEADS = 4
A_WIDTH = A_HEADS * HEAD_DIM
A_KV_WIDTH = A_KV_HEADS * HEAD_DIM
B_WIDTH = B_HEADS * HEAD_DIM
C_WIDTH = C_HEADS * HEAD_DIM
DECAY_LORA = 64
AAA_LORA = 64
GATE_LORA = 128
RW_WIDTH = 3 * B_WIDTH + 2 * DECAY_LORA + 2 * AAA_LORA + GATE_LORA
ROPE_BASE = 10000.0
ROPE_HALF = HEAD_DIM // 4
WIN_ROWS = 8
WIN_COLS = 16
N_EXPERTS = 16
CAPACITY_FACTOR = 2
NORM_EPS = 1e-6
LNX_EPS = 64e-5
ATTN_SCALE = HEAD_DIM ** -0.5
V_EXT = 2 * HEAD_DIM
MASK_VALUE = -1e30

LANES = 128
ROW_TILE = 256
CHUNK = 64
RWKV_CHUNKS_PER_STEP = 4
NATTEN_ROWS_PER_ITER = 8
MOD_ROWS = 16
VMEM_LIMIT = 56 * 1024 * 1024


def _cparams(*sem):
    return pltpu.CompilerParams(dimension_semantics=sem, vmem_limit_bytes=VMEM_LIMIT)


def _dot(a, b):
    return jnp.dot(a, b, preferred_element_type=F32)


def _dot_nt(a, b):
    return lax.dot_general(a, b, (((1,), (1,)), ((), ())), preferred_element_type=F32)


def _dot_tn(a, b):
    return lax.dot_general(a, b, (((0,), (0,)), ((), ())), preferred_element_type=F32)


def _split2(x):
    hi = x.astype(BF16)
    lo = (x - hi.astype(F32)).astype(BF16)
    return hi, lo


def _split3(x):
    hi = x.astype(BF16)
    r1 = x - hi.astype(F32)
    mid = r1.astype(BF16)
    lo = (r1 - mid.astype(F32)).astype(BF16)
    return hi, mid, lo


def _head_sum(x, bd):
    outs = []
    for c in range(x.shape[1] // LANES):
        hi, lo = _split2(x[:, c * LANES:(c + 1) * LANES])
        outs.append(_dot(hi, bd) + _dot(lo, bd))
    return outs[0] if len(outs) == 1 else jnp.concatenate(outs, axis=1)


def _tile_lanes(t, width):
    reps = width // t.shape[1]
    return t if reps == 1 else jnp.concatenate([t] * reps, axis=1)


def _rms(x, g):
    return x * lax.rsqrt(jnp.mean(x * x, axis=-1, keepdims=True) + NORM_EPS) * g


def _sigmoid(x):
    return 1.0 / (1.0 + jnp.exp(-x))


def _segment_specs(width, n_lat_tiles, ctx_tile):
    return (pl.BlockSpec((None, ROW_TILE, width), lambda b, t: (b, jnp.minimum(t, n_lat_tiles - 1), 0)),
            pl.BlockSpec((None, ROW_TILE, width), lambda b, t: (b, ctx_tile, 0)))


def _segment_rows(lat_ref, ctx_ref, n_lat_tiles):
    return jnp.where(pl.program_id(1) >= n_lat_tiles, ctx_ref[...], lat_ref[...])


def _mod_kernel(s_ref, w_ref, b_ref, o_ref):
    s = s_ref[...]
    s = s * _sigmoid(s)
    o_ref[...] = _dot(s, w_ref[...]) + b_ref[...]


def _modulation(cond, w_mod, b_mod):
    L, D, N = w_mod.shape
    tn = 512
    return pl.pallas_call(
        _mod_kernel,
        grid=(L, N // tn),
        in_specs=[pl.BlockSpec((MOD_ROWS, D), lambda l, j: (0, 0)),
                  pl.BlockSpec((None, D, tn), lambda l, j: (l, 0, j)),
                  pl.BlockSpec((None, 1, tn), lambda l, j: (l, 0, j))],
        out_specs=pl.BlockSpec((None, MOD_ROWS, tn), lambda l, j: (l, 0, j)),
        out_shape=jax.ShapeDtypeStruct((L, MOD_ROWS, N), F32),
        compiler_params=_cparams("parallel", "parallel"),
        name="modulation",
    )(cond, w_mod, b_mod.reshape(L, 1, N))


def _rope(x, cos, sin):
    w = x.shape[1]
    lane = lax.broadcasted_iota(jnp.int32, x.shape, 1)
    upper = (lane // ROPE_HALF) % 2 == 1
    partner = jnp.where(upper, pltpu.roll(x, ROPE_HALF, 1), pltpu.roll(x, w - ROPE_HALF, 1))
    return x * _tile_lanes(cos, w) + partner * _tile_lanes(sin, w)


def _inproj_kernel(xl_ref, xc_ref, g_ref, shift_ref, scale_ref, wqa_ref, wka_ref, wva_ref, wrw_ref, wn_ref,
                   qg_ref, kg_ref, cos_ref, sin_ref, bd_ref, vone_ref,
                   qa_ref, ka_ref, va_ref, rw_ref, qn_ref, kn_ref, vn_ref, *, n_lat_tiles):
    x = _segment_rows(xl_ref, xc_ref, n_lat_tiles)
    h = _rms(x, g_ref[...]) * (1.0 + scale_ref[...]) + shift_ref[...]
    hb = h.astype(BF16)
    bd = bd_ref[...]
    cos, sin = cos_ref[...], sin_ref[...]

    def normed(w_ref, gain_ref):
        y = _dot(hb, w_ref[...])
        ms = _head_sum(y * y, bd) * (1.0 / HEAD_DIM)
        return _rope(y * lax.rsqrt(ms + NORM_EPS) * gain_ref[...], cos, sin)

    def put_heads(o_ref, y, width=HEAD_DIM):
        for hd in range(y.shape[1] // width):
            o_ref[hd] = y[:, hd * width:(hd + 1) * width].astype(o_ref.dtype)

    put_heads(qa_ref, normed(wqa_ref, qg_ref) * ATTN_SCALE)
    put_heads(ka_ref, normed(wka_ref, kg_ref))
    vone = vone_ref[...]
    put_heads(va_ref, _dot(hb, wva_ref[...]) + vone[:, :A_KV_HEADS * V_EXT], V_EXT)
    rw_ref[...] = _dot(hb, wrw_ref[...])
    yn = _dot(hb, wn_ref[...])
    put_heads(qn_ref, yn[:, :C_WIDTH] * ATTN_SCALE)
    put_heads(kn_ref, yn[:, C_WIDTH:2 * C_WIDTH])
    put_heads(vn_ref, yn[:, 2 * C_WIDTH:] + vone, V_EXT)


def _extend_values(w, heads):
    D = w.shape[0]
    w = jnp.pad(w.reshape(D, heads, HEAD_DIM), ((0, 0), (0, 0), (0, V_EXT - HEAD_DIM)))
    return w.reshape(D, heads * V_EXT)


def _in_projection(x_lat, x_ctx, ctx_tile, mod4, g_pre, w_in, q_gain, k_gain, cos, sin, bd, n_lat_tiles):
    B, _, D = x_lat.shape
    tm = ROW_TILE
    T = (n_lat_tiles + 1) * tm
    o = np.cumsum([0, A_WIDTH, A_KV_WIDTH, A_KV_WIDTH, RW_WIDTH, 2 * C_WIDTH, C_WIDTH])
    wb = w_in.astype(BF16)
    wqa, wka, wva, wrw, wqk, wvn = [wb[:, o[i]:o[i + 1]] for i in range(6)]
    ws = [wqa, wka, _extend_values(wva, A_KV_HEADS), wrw,
          jnp.concatenate([wqk, _extend_values(wvn, C_HEADS)], axis=1)]
    vone = jnp.asarray(np.arange(C_HEADS * V_EXT) % V_EXT == HEAD_DIM, F32).reshape(1, C_HEADS * V_EXT)
    const = lambda shape: pl.BlockSpec(shape, lambda b, t: (0,) * len(shape))
    mod_spec = lambda k: pl.BlockSpec(
        (None, None, 1, D), lambda b, t: (jnp.where(t >= n_lat_tiles, B, b), k, 0, 0))
    heads = lambda nh, width=HEAD_DIM: pl.BlockSpec((None, nh, tm, width), lambda b, t: (b, 0, t, 0))
    hshape = lambda nh, width=HEAD_DIM: jax.ShapeDtypeStruct((B, nh, T, width), BF16)
    return pl.pallas_call(
        functools.partial(_inproj_kernel, n_lat_tiles=n_lat_tiles),
        grid=(B, T // tm),
        in_specs=[*_segment_specs(D, n_lat_tiles, ctx_tile), const((1, D)), mod_spec(0), mod_spec(1)]
                 + [const(w.shape) for w in ws]
                 + [const((1, A_WIDTH)), const((1, A_KV_WIDTH)),
                    pl.BlockSpec((tm, LANES), lambda b, t: (t, 0)),
                    pl.BlockSpec((tm, LANES), lambda b, t: (t, 0)),
                    const((LANES, LANES)), const(vone.shape)],
        out_specs=[heads(A_HEADS), heads(A_KV_HEADS), heads(A_KV_HEADS, V_EXT),
                   pl.BlockSpec((None, tm, RW_WIDTH), lambda b, t: (b, t, 0)),
                   heads(C_HEADS), heads(C_HEADS), heads(C_HEADS, V_EXT)],
        out_shape=[hshape(A_HEADS), hshape(A_KV_HEADS), hshape(A_KV_HEADS, V_EXT),
                   jax.ShapeDtypeStruct((B, T, RW_WIDTH), F32),
                   hshape(C_HEADS), hshape(C_HEADS), hshape(C_HEADS, V_EXT)],
        compiler_params=_cparams("parallel", "parallel"),
        name="in_projection",
    )(x_lat, x_ctx, g_pre.reshape(1, D), mod4, mod4, *ws,
      jnp.tile(q_gain, A_HEADS).reshape(1, A_WIDTH), jnp.tile(k_gain, A_KV_HEADS).reshape(1, A_KV_WIDTH),
      cos, sin, bd, vone)


def _softmax_weights(s, m):
    return jnp.exp((s - m).astype(BF16))


def _normalised(ov):
    return ov[:, :HEAD_DIM] / ov[:, HEAD_DIM:HEAD_DIM + 1]


def _gqa_kernel(q_ref, k_ref, v_ref, o_ref, *, n_lat, n_lat_tiles):
    group = q_ref.shape[0]

    def attend(k, v):
        scores = [_dot_nt(q_ref[hd], k) for hd in range(group)]
        probs = [_softmax_weights(s, jnp.max(s, axis=-1, keepdims=True)) for s in scores]
        for hd, p in enumerate(probs):
            o_ref[:, hd * HEAD_DIM:(hd + 1) * HEAD_DIM] = _normalised(_dot(p, v)).astype(o_ref.dtype)

    @pl.when(pl.program_id(2) < n_lat_tiles)
    def _():
        attend(k_ref[...], v_ref[...])

    @pl.when(pl.program_id(2) >= n_lat_tiles)
    def _():
        attend(k_ref[n_lat:, :], v_ref[n_lat:, :])


def _gqa_attention(q, k, v, n_lat, n_rows):
    B, _, T, _ = q.shape
    tq = ROW_TILE
    group = A_HEADS // A_KV_HEADS
    kv_spec = lambda width: pl.BlockSpec((None, None, T, width), lambda b, g, t: (b, g, 0, 0))
    return pl.pallas_call(
        functools.partial(_gqa_kernel, n_lat=n_lat, n_lat_tiles=n_lat // tq),
        grid=(B, A_KV_HEADS, n_rows // tq),
        in_specs=[pl.BlockSpec((None, group, tq, HEAD_DIM), lambda b, g, t: (b, g, t, 0)),
                  kv_spec(HEAD_DIM), kv_spec(V_EXT)],
        out_specs=pl.BlockSpec((None, tq, group * HEAD_DIM), lambda b, g, t: (b, t, g)),
        out_shape=jax.ShapeDtypeStruct((B, n_rows, A_WIDTH), BF16),
        compiler_params=_cparams("parallel", "parallel", "parallel"),
        name="gqa_attention",
    )(q, k, v)


def _natten_kernel(q_ref, k_ref, v_ref, bias_ref, o_ref, *, n_lat, rows, need_ctx):
    win = WIN_ROWS * GRID_W
    for hd in range(C_HEADS):
        kc = k_ref[hd, n_lat:, :]
        vc = v_ref[hd, n_lat:, :]

        def row_group(g, carry, hd=hd, kc=kc, vc=vc):
            scores = []
            for u in range(NATTEN_ROWS_PER_ITER):
                r = g * NATTEN_ROWS_PER_ITER + u
                rs = jnp.clip(r - WIN_ROWS // 2, 0, rows - WIN_ROWS)
                q = q_ref[hd, pl.ds(pl.multiple_of(r * GRID_W, GRID_W), GRID_W), :]
                kw = k_ref[hd, pl.ds(pl.multiple_of(rs * GRID_W, GRID_W), win), :]
                scores.append((r, rs, _dot_nt(q, kw) + bias_ref[r - rs, hd], _dot_nt(q, kc)))
            probs = []
            for r, rs, s_w, s_c in scores:
                m = jnp.maximum(jnp.max(s_w, axis=-1, keepdims=True), jnp.max(s_c, axis=-1, keepdims=True))
                probs.append((r, rs, _softmax_weights(s_w, m), _softmax_weights(s_c, m)))
            for r, rs, p_w, p_c in probs:
                vw = v_ref[hd, pl.ds(pl.multiple_of(rs * GRID_W, GRID_W), win), :]
                o = _normalised(_dot(p_w, vw) + _dot(p_c, vc))
                o_ref[pl.ds(pl.multiple_of(r * GRID_W, GRID_W), GRID_W),
                      hd * HEAD_DIM:(hd + 1) * HEAD_DIM] = o.astype(o_ref.dtype)
            return carry

        lax.fori_loop(0, rows // NATTEN_ROWS_PER_ITER, row_group, 0)
        if need_ctx:
            s = _dot_nt(q_ref[hd, n_lat:, :], kc)
            o = _normalised(_dot(_softmax_weights(s, jnp.max(s, axis=-1, keepdims=True)), vc))
            o_ref[n_lat:, hd * HEAD_DIM:(hd + 1) * HEAD_DIM] = o.astype(o_ref.dtype)


def _natten_bias(rpb, rows):
    off = np.arange(WIN_ROWS)[:, None, None]
    jr = np.arange(WIN_ROWS)[None, :, None]
    row_sel = (np.arange(2 * WIN_ROWS - 1)[None, None, :] == jr - off + WIN_ROWS - 1)
    qc = np.arange(GRID_W)[:, None, None]
    kc = np.arange(GRID_W)[None, :, None]
    col_start = np.clip(qc - WIN_COLS // 2, 0, GRID_W - WIN_COLS)
    valid = (kc >= col_start) & (kc < col_start + WIN_COLS)
    col_sel = (np.arange(2 * WIN_COLS - 1)[None, None, :] == kc - qc + WIN_COLS - 1) & valid
    bias = jnp.einsum("hrc,ojr,qkc->ohqjk", rpb.astype(F32), jnp.asarray(row_sel, F32), jnp.asarray(col_sel, F32),
                      precision=lax.Precision.HIGHEST)
    bias = bias + jnp.asarray(np.where(valid[None, None, :, None, :, 0], 0.0, MASK_VALUE), F32)
    return bias.reshape(WIN_ROWS, C_HEADS, GRID_W, WIN_ROWS * GRID_W)


def _neighbourhood_attention(q, k, v, bias, n_lat, need_ctx):
    B, _, T, _ = q.shape
    n_rows = T if need_ctx else n_lat
    full = lambda width: pl.BlockSpec((None, C_HEADS, T, width), lambda b: (b, 0, 0, 0))
    return pl.pallas_call(
        functools.partial(_natten_kernel, n_lat=n_lat, rows=n_lat // GRID_W, need_ctx=need_ctx),
        grid=(B,),
        in_specs=[full(HEAD_DIM), full(HEAD_DIM), full(V_EXT), pl.BlockSpec(bias.shape, lambda b: (0, 0, 0, 0))],
        out_specs=pl.BlockSpec((None, n_rows, C_WIDTH), lambda b: (b, 0, 0)),
        out_shape=jax.ShapeDtypeStruct((B, n_rows, C_WIDTH), BF16),
        compiler_params=_cparams("parallel"),
        name="neighbourhood_attention",
    )(q, k, v, bias)


def _rwkv_chunk_kernel(rw_ref, w0_ref, a0_ref, lora_ref, kk_ref, ka_ref, rk_ref, bdw_ref,
                       rhat_ref, yhat_ref, gt_ref, ht_ref, gate_ref, bonus_ref):
    C, W, NH = CHUNK, B_WIDTH, B_HEADS
    S = NH * C
    R = rw_ref.shape[0]
    rw = rw_ref[...]
    r, k, v = rw[:, :W], rw[:, W:2 * W], rw[:, 2 * W:3 * W]
    wl = rw[:, 3 * W:3 * W + 2 * DECAY_LORA]
    al = rw[:, 3 * W + 2 * DECAY_LORA:3 * W + 2 * DECAY_LORA + 2 * AAA_LORA]
    gl = rw[:, 3 * W + 2 * DECAY_LORA + 2 * AAA_LORA:]
    bdw = bdw_ref[...]

    def head_sum(x):
        hi, lo = _split2(x)
        p = _dot(jnp.concatenate([hi, lo], axis=0), bdw)
        return p[:R] + p[R:]

    kkv = k * kk_ref[...]
    kkv = kkv / jnp.maximum(jnp.sqrt(head_sum(kkv * kkv)), 1e-12)
    lora = _dot(jnp.concatenate([jnp.tanh(wl), al, _sigmoid(gl)], axis=1), lora_ref[...])
    gate_ref[...] = lora[:, 4 * W:]
    lws, asigs = [], []
    for d in range(2):
        w = w0_ref[d] + lora[:, d * W:(d + 1) * W]
        w = -(jnp.maximum(-w, 0.0) + jnp.log(1.0 + jnp.exp(-jnp.abs(w)))) - 0.5
        lws.append(-jnp.exp(w))
        asigs.append(_sigmoid(a0_ref[d] + lora[:, (2 + d) * W:(3 + d) * W]))

    prow = lax.broadcasted_iota(jnp.int32, (R, R), 0)
    pcol = lax.broadcasted_iota(jnp.int32, (R, R), 1)
    tri = ((prow >= pcol) & (prow // C == pcol // C)).astype(BF16)
    cs = _dot(tri, jnp.concatenate([p for lw in lws for p in _split3(lw)], axis=1))
    prefix = [cs[:, 3 * d * W:(3 * d + 1) * W] + cs[:, (3 * d + 1) * W:(3 * d + 2) * W]
              + cs[:, (3 * d + 2) * W:(3 * d + 3) * W] for d in range(2)]

    srow = lax.broadcasted_iota(jnp.int32, (S, W), 0)
    scol = lax.broadcasted_iota(jnp.int32, (S, W), 1)
    same = (srow // C) == (scol // HEAD_DIM)
    crow = lax.broadcasted_iota(jnp.int32, (C, W), 0)
    cpos = lax.broadcasted_iota(jnp.int32, (C, W), 1) % HEAD_DIM
    eye_c = crow == cpos
    eye_b = eye_c.astype(BF16)

    def stack(x):
        return jnp.where(same, jnp.concatenate([x] * NH, axis=0), 0.0).astype(BF16)

    kds = [k * (1.0 + (asig - 1.0) * ka_ref[...]) for asig in asigs]
    bonus_ref[...] = head_sum(r * (kds[0] + kds[1]) * rk_ref[...]) * v

    class Chain:
        pass

    chains = []
    for ci in range(R // C):
        v_s = stack(v[ci * C:(ci + 1) * C])
        for d in range(2):
            ch = Chain()
            ch.d, ch.ci, ch.rows, ch.v_s = d, ci, slice(ci * C, (ci + 1) * C), v_s
            chains.append(ch)

    for ch in chains:
        d, rows = ch.d, ch.rows
        lw, kd, b, a = lws[d][rows], kds[d][rows], (kkv * asigs[d])[rows], -kkv[rows]
        pre = prefix[d][rows]
        ch.tot = tot = pre[C - 1:C, :]
        cum_incl = pre if d == 0 else tot - pre + lw
        cum_excl = cum_incl - lw
        rho = 0.5 * tot
        e_in = jnp.exp(rho - cum_incl)
        e_end = jnp.exp(tot - cum_incl)
        ch.e_rho = jnp.exp(rho)
        ch.a_rho = a * jnp.exp(cum_excl - rho)
        ch.r_rho = r[rows] * jnp.exp(cum_incl - rho)
        ch.ends = jnp.concatenate([stack(b * e_end), stack(kd * e_end)], axis=0)
        ch.m = _dot_nt(jnp.concatenate([ch.a_rho, ch.r_rho], axis=0).astype(BF16),
                       jnp.concatenate([stack(b * e_in), stack(kd * e_in)], axis=0))

    for ch in chains:
        before, upto = (crow > cpos, crow >= cpos) if ch.d == 0 else (crow < cpos, crow <= cpos)
        m = ch.m
        l_ab = jnp.where(before, m[:C, :W], 0.0)
        ch.a_rb = jnp.where(upto, m[C:, :W], 0.0).astype(BF16)
        ch.kv = jnp.concatenate([jnp.where(before, m[:C, W:], 0.0), jnp.where(upto, m[C:, W:], 0.0)],
                                axis=0).astype(BF16)
        ch.lp = _dot(l_ab.astype(BF16), stack(l_ab))
        ch.tinv = jnp.where(eye_c, 1.0, l_ab)
    for _ in range(int(np.log2(C)) - 2):
        for ch in chains:
            p = _dot(jnp.concatenate([ch.tinv, ch.lp], axis=0).astype(BF16), stack(ch.lp))
            ch.tinv = ch.tinv + p[:C]
            ch.lp = p[C:]
    for ch in chains:
        ch.tinv = ch.tinv + _dot(ch.tinv.astype(BF16), stack(ch.lp))
    for ch in chains:
        ch.wv = _dot(ch.kv, ch.v_s)
    for ch in chains:
        ch.z = _dot(ch.tinv.astype(BF16),
                    jnp.concatenate([stack(ch.a_rho), stack(ch.wv[:C])], axis=1))
    for ch in chains:
        ch.ahat = ch.z[:, :W] * ch.e_rho
        ch.av = jnp.concatenate([stack(ch.z[:, :W]), stack(ch.z[:, W:])], axis=1)
        rz = _dot(ch.a_rb, ch.av)
        rhat_ref[ch.d, ch.rows] = (ch.r_rho + rz[:, :W]) * ch.e_rho
        yhat_ref[ch.d, ch.rows] = rz[:, W:] + ch.wv[C:]
    for ch in chains:
        ch.ends_t = _dot_nt(eye_b, ch.ends).astype(BF16)
    for ch in chains:
        gh = _dot(ch.ends_t[:, :S], jnp.concatenate([stack(ch.ahat), ch.av[:, W:]], axis=1))
        gt_ref[ch.d, ch.ci] = gh[:, :W] + jnp.where(eye_c, jnp.exp(ch.tot), 0.0)
        ht_ref[ch.d, ch.ci] = gh[:, W:] + _dot(ch.ends_t[:, S:], ch.v_s)


def _rwkv_chunks(rw, p):
    B, T, _ = rw.shape
    C, W = CHUNK, B_WIDTH
    nc = T // C
    per = RWKV_CHUNKS_PER_STEP
    assert C == HEAD_DIM and nc % per == 0
    const = lambda shape: pl.BlockSpec(shape, lambda b, c: (0,) * len(shape))
    rows = pl.BlockSpec((None, 2, per * C, W), lambda b, c: (b, 0, c, 0))
    mats = pl.BlockSpec((None, 2, per, C, W), lambda b, c: (b, 0, c, 0, 0))
    flat = pl.BlockSpec((None, per * C, W), lambda b, c: (b, c, 0))
    vec = lambda a: a.reshape(1, W)
    lora_w = jnp.zeros((2 * DECAY_LORA + 2 * AAA_LORA + GATE_LORA, 5 * W), F32)
    for d in range(2):
        lora_w = lora_w.at[d * DECAY_LORA:(d + 1) * DECAY_LORA, d * W:(d + 1) * W].set(p["w_up"][d])
        lora_w = lora_w.at[2 * DECAY_LORA + d * AAA_LORA:2 * DECAY_LORA + (d + 1) * AAA_LORA,
                           (2 + d) * W:(3 + d) * W].set(p["a_up"][d])
    lora_w = lora_w.at[2 * DECAY_LORA + 2 * AAA_LORA:, 4 * W:].set(p["g_up"])
    lane = np.arange(W)
    bdw = jnp.asarray(lane[:, None] // HEAD_DIM == lane[None, :] // HEAD_DIM, BF16)
    return pl.pallas_call(
        _rwkv_chunk_kernel,
        grid=(B, nc // per),
        in_specs=[pl.BlockSpec((None, per * C, RW_WIDTH), lambda b, c: (b, c, 0)),
                  const((2, 1, W)), const((2, 1, W)), const(lora_w.shape),
                  const((1, W)), const((1, W)), const((1, W)), const((W, W))],
        out_specs=[rows, rows, mats, mats, flat, flat],
        out_shape=[jax.ShapeDtypeStruct((B, 2, T, W), F32), jax.ShapeDtypeStruct((B, 2, T, W), F32),
                   jax.ShapeDtypeStruct((B, 2, nc, C, W), F32), jax.ShapeDtypeStruct((B, 2, nc, C, W), F32),
                   jax.ShapeDtypeStruct((B, T, W), F32), jax.ShapeDtypeStruct((B, T, W), F32)],
        compiler_params=_cparams("parallel", "parallel"),
        name="rwkv_chunks",
    )(rw, p["w0"].reshape(2, 1, W), p["a0"].reshape(2, 1, W), lora_w,
      vec(p["k_k"]), vec(p["k_a"]), vec(p["r_k"]), bdw)


def _rwkv_scan_kernel(rhat0_ref, yhat0_ref, gt0_ref, ht0_ref, rhat1_ref, yhat1_ref, gt1_ref, ht1_ref,
                      y0_ref, y1_ref, st_ref):
    @pl.when(pl.program_id(0) == 0)
    def _():
        st_ref[...] = jnp.zeros_like(st_ref)

    W = B_WIDTH
    same = (lax.broadcasted_iota(jnp.int32, (W, W), 0) // HEAD_DIM
            == lax.broadcasted_iota(jnp.int32, (W, W), 1) // HEAD_DIM)
    expand = lambda x: jnp.where(same, jnp.concatenate([x] * B_HEADS, axis=0), 0.0)
    dirs = ((rhat0_ref, yhat0_ref, gt0_ref, ht0_ref, y0_ref), (rhat1_ref, yhat1_ref, gt1_ref, ht1_ref, y1_ref))
    for b in range(st_ref.shape[1]):
        for d, (rhat_ref, yhat_ref, gt_ref, ht_ref, y_ref) in enumerate(dirs):
            stb = st_ref[d, b].astype(BF16)
            y_ref[b] = _dot(rhat_ref[b].astype(BF16), stb) + yhat_ref[b]
            st_ref[d, b] = _dot(expand(gt_ref[b]).astype(BF16), stb) + expand(ht_ref[b])


def _rwkv_scan(rhat, yhat, gt, ht, n_lat_chunks):
    B, _, T, W = rhat.shape
    C = CHUNK
    nc = T // C
    chunk = (lambda s: (s + n_lat_chunks) % nc, lambda s: nc - 1 - s)
    rows = lambda d: pl.BlockSpec((B, None, C, W), lambda s: (0, d, chunk[d](s), 0))
    mats = lambda d: pl.BlockSpec((B, None, None, C, W), lambda s: (0, d, chunk[d](s), 0, 0))
    outs = lambda d: pl.BlockSpec((B, C, W), lambda s: (0, chunk[d](s), 0))
    return pl.pallas_call(
        _rwkv_scan_kernel,
        grid=(nc,),
        in_specs=[rows(0), rows(0), mats(0), mats(0), rows(1), rows(1), mats(1), mats(1)],
        out_specs=[outs(0), outs(1)],
        out_shape=[jax.ShapeDtypeStruct((B, T, W), F32)] * 2,
        scratch_shapes=[pltpu.VMEM((2, B, W, W), F32)],
        compiler_params=_cparams("arbitrary"),
        name="rwkv_scan",
    )(rhat, yhat, gt, ht, rhat, yhat, gt, ht)


def _outproj_kernel(xl_ref, xc_ref, oa_ref, y0_ref, y1_ref, bonus_ref, gate_ref, on_ref, lnw_ref, lnb_ref, bd_ref,
                    wa_ref, wb_ref, wn_ref, g_ref, mgate_ref, o_ref, *, n_lat_tiles):
    y = y0_ref[...] + y1_ref[...]
    bd = bd_ref[...]
    mu = _head_sum(y, bd) * (1.0 / HEAD_DIM)
    yc = y - mu
    var = _head_sum(yc * yc, bd) * (1.0 / HEAD_DIM)
    ob = (yc * lax.rsqrt(var + LNX_EPS) * lnw_ref[...] + lnb_ref[...] + bonus_ref[...]) * gate_ref[...]
    out = _dot(oa_ref[...], wa_ref[...]) + _dot(ob.astype(BF16), wb_ref[...]) + _dot(on_ref[...], wn_ref[...])
    o_ref[...] = _segment_rows(xl_ref, xc_ref, n_lat_tiles) + mgate_ref[...] * _rms(out, g_ref[...])


def _out_projection(x_lat, x_ctx, ctx_tile, oa, y0, y1, bonus, gate, on, p, bd, w_out, g_post, mod4,
                    n_lat_tiles, n_rows):
    B, _, D = x_lat.shape
    tm = ROW_TILE
    W = B_WIDTH
    wb = w_out.astype(BF16)
    const = lambda shape: pl.BlockSpec(shape, lambda b, t: (0,) * len(shape))
    rows = lambda width: pl.BlockSpec((None, tm, width), lambda b, t: (b, t, 0))
    return pl.pallas_call(
        functools.partial(_outproj_kernel, n_lat_tiles=n_lat_tiles),
        grid=(B, n_rows // tm),
        in_specs=[*_segment_specs(D, n_lat_tiles, ctx_tile),
                  rows(A_WIDTH), rows(W), rows(W), rows(W), rows(W), rows(C_WIDTH),
                  const((1, W)), const((1, W)), const((LANES, LANES)),
                  const((A_WIDTH, D)), const((W, D)), const((C_WIDTH, D)), const((1, D)),
                  pl.BlockSpec((None, None, 1, D), lambda b, t: (jnp.where(t >= n_lat_tiles, B, b), 2, 0, 0))],
        out_specs=rows(D),
        out_shape=jax.ShapeDtypeStruct((B, n_rows, D), F32),
        compiler_params=_cparams("parallel", "parallel"),
        name="out_projection",
    )(x_lat, x_ctx, oa, y0, y1, bonus, gate, on, p["lnx_w"].reshape(1, W), p["lnx_b"].reshape(1, W), bd,
      wb[:A_WIDTH], wb[A_WIDTH:A_WIDTH + W], wb[A_WIDTH + W:], g_post.reshape(1, D), mod4)


def _router_kernel(x_ref, g_ref, shift_ref, scale_ref, wr_ref, h_ref, aff_ref):
    h = _rms(x_ref[...], g_ref[...]) * (1.0 + scale_ref[...]) + shift_ref[...]
    h_ref[...] = h.astype(h_ref.dtype)
    logits = _dot_nt(wr_ref[...], h)
    p = jnp.exp(logits - jnp.max(logits, axis=0, keepdims=True))
    aff_ref[...] = p / jnp.sum(p, axis=0, keepdims=True)


def _router(xs, g_pre, mod4, w_router, n_lat_tiles):
    B, R, D = xs.shape
    tm = ROW_TILE
    E = w_router.shape[1]
    mod_spec = lambda k: pl.BlockSpec(
        (None, None, 1, D), lambda b, t: (jnp.where(t >= n_lat_tiles, B, b), k, 0, 0))
    return pl.pallas_call(
        _router_kernel,
        grid=(B, R // tm),
        in_specs=[pl.BlockSpec((None, tm, D), lambda b, t: (b, t, 0)),
                  pl.BlockSpec((1, D), lambda b, t: (0, 0)), mod_spec(3), mod_spec(4),
                  pl.BlockSpec((E, D), lambda b, t: (0, 0))],
        out_specs=[pl.BlockSpec((None, tm, D), lambda b, t: (b, t, 0)),
                   pl.BlockSpec((None, E, tm), lambda b, t: (b, 0, t))],
        out_shape=[jax.ShapeDtypeStruct((B, R, D), BF16), jax.ShapeDtypeStruct((B, E, R), F32)],
        compiler_params=_cparams("parallel", "parallel"),
        name="moe_router",
    )(xs, g_pre.reshape(1, D), mod4, mod4, w_router.T)


def _column(ref, e):
    lane = lax.broadcasted_iota(jnp.int32, ref.shape, 1)
    return jnp.sum(jnp.where(lane == e, ref[...], 0.0), axis=1, keepdims=True)


RANK_BLOCK = 256


def _rank_partials(a_row, a_col):
    n = a_row.shape[-1]
    tj = min(n, RANK_BLOCK)
    as_count = lambda mask: mask.astype(F32).astype(BF16)

    def partial(jc):
        lo, hi = jc * tj, (jc + 1) * tj
        aj = a_col[lo:hi, :]
        ai = a_row[:, lo:hi]
        j_first = (lax.broadcasted_iota(jnp.int32, (tj, tj), 0) < lax.broadcasted_iota(jnp.int32, (tj, tj), 1))
        parts = [as_count(aj > a_row[:, :lo])] if lo else []
        parts.append(as_count((aj > ai) | ((aj == ai) & j_first)))
        if hi < n:
            parts.append(as_count(aj >= a_row[:, hi:]))
        return _dot(jnp.ones((8, tj), BF16), parts[0] if len(parts) == 1 else jnp.concatenate(parts, axis=1))

    return [functools.partial(partial, jc) for jc in range(n // tj)]


def _token_ranks(a_row, a_col):
    cnt = None
    for thunk in _rank_partials(a_row, a_col):
        cnt = thunk() if cnt is None else cnt + thunk()
    return cnt[0:1, :]


def _rank_kernel(arow_ref, acol_ref, rank_ref):
    rank_ref[...] = _token_ranks(arow_ref[...], _column(acol_ref, pl.program_id(1)))


def _ranks(aff_t, aff_c):
    B, E, n = aff_t.shape
    return pl.pallas_call(
        _rank_kernel,
        grid=(B, E),
        in_specs=[pl.BlockSpec((None, None, 1, n), lambda b, e: (b, e, 0, 0)),
                  pl.BlockSpec((None, n, E), lambda b, e: (b, 0, 0))],
        out_specs=pl.BlockSpec((None, None, 1, n), lambda b, e: (b, e, 0, 0)),
        out_shape=jax.ShapeDtypeStruct((B, E, 1, n), F32),
        compiler_params=_cparams("parallel", "parallel"),
        name="moe_ranks",
    )(aff_t.reshape(B, E, 1, n), aff_c)


def _expert_kernel(h_ref, aff_ref, rank_src_ref, wg_ref, wu_ref, wd_ref, f_ref, yet_ref, ye_ref, ft_ref,
                   *rank_scratch, cap):
    n = h_ref.shape[0]
    E = aff_ref.shape[0]
    e = pl.program_id(1)
    if rank_scratch:
        rank_ref, = rank_scratch

        @pl.when(e == 0)
        def _():
            rank_ref[0:1, :] = _token_ranks(aff_ref[0:1, :], _column(rank_src_ref, 0))
    else:
        rank_ref = rank_src_ref
    slot = lax.broadcasted_iota(jnp.int32, (cap, n), 0).astype(F32)
    pick = (rank_ref[pl.ds(e, 1), :] == slot).astype(F32).astype(BF16)
    nxt = jnp.minimum(e + 1, E - 1)
    pieces = _rank_partials(aff_ref[pl.ds(nxt, 1), :], _column(rank_src_ref, nxt)) if rank_scratch else []
    counts = []

    def rank_pieces(share):
        for _ in range(-(-len(pieces) // 4) if share else len(pieces)):
            if pieces:
                counts.append(pieces.pop(0)())

    rank_pieces(True)
    xe = _dot(pick, h_ref[...]).astype(BF16)
    rank_pieces(True)
    gte = _dot(xe, wg_ref[...])
    rank_pieces(True)
    hid = (gte * _sigmoid(gte) * _dot(xe, wu_ref[...])).astype(BF16)
    rank_pieces(False)
    ye_ref[...] = _dot(hid, wd_ref[...])
    yet_ref[:, pl.ds(pl.multiple_of(e * cap, cap), cap)] = ye_ref[...].T.astype(BF16)
    if counts:
        rank_ref[pl.ds(nxt, 1), :] = functools.reduce(lambda a, b: a + b, counts)[0:1, :]

    @pl.when(e == E - 1)
    def _():
        tn = min(n, ROW_TILE)
        slot_c = lax.broadcasted_iota(jnp.int32, (cap, tn), 0).astype(F32)
        for c in range(n // tn):
            cols = slice(c * tn, (c + 1) * tn)
            put = jnp.concatenate(
                [jnp.where(rank_ref[x:x + 1, cols] == slot_c, aff_ref[x:x + 1, cols], 0.0).astype(BF16)
                 for x in range(E)], axis=0)
            ft_ref[...] = _dot(yet_ref[...], put)
            f_ref[cols, :] = ft_ref[...].T


def _experts(h, aff, slots, weights, layer, cap):
    S, E, n = aff.shape
    D = h.shape[-1]
    F = weights[0].shape[-1]
    assert cap % LANES == 0
    rows = pl.BlockSpec((None, E, n), lambda s, e: (s, 0, 0))
    if slots is None:
        rank_src, rank_spec = jnp.swapaxes(aff, 1, 2), pl.BlockSpec((None, n, E), lambda s, e: (s, 0, 0))
    else:
        rank_src, rank_spec = slots, rows
    weight = lambda a, b: pl.BlockSpec((None, None, a, b), lambda s, e: (layer, e, 0, 0))
    return pl.pallas_call(
        functools.partial(_expert_kernel, cap=cap),
        grid=(S, E),
        in_specs=[pl.BlockSpec((None, n, D), lambda s, e: (s, 0, 0)), rows, rank_spec,
                  weight(D, F), weight(D, F), weight(F, D)],
        out_specs=pl.BlockSpec((None, n, D), lambda s, e: (s, 0, 0)),
        out_shape=jax.ShapeDtypeStruct((S, n, D), F32),
        scratch_shapes=[pltpu.VMEM((D, E * cap), BF16), pltpu.VMEM((cap, D), F32),
                        pltpu.VMEM((D, min(n, ROW_TILE)), F32)]
                       + ([pltpu.VMEM((E, n), F32)] if slots is None else []),
        compiler_params=_cparams("parallel", "arbitrary"),
        name="moe_experts",
    )(h, aff, rank_src, *weights)


def _ffn_residual_kernel(x_ref, fl_ref, fc_ref, g_ref, mgate_ref, o_ref, *, n_lat_tiles):
    f = _segment_rows(fl_ref, fc_ref, n_lat_tiles)
    o_ref[...] = x_ref[...] + mgate_ref[...] * _rms(f, g_ref[...])


def _ffn_residual(xs, f_lat, f_ctx, g_post, mod4, n_lat_tiles):
    B, R, D = xs.shape
    tm = ROW_TILE
    rows = pl.BlockSpec((None, tm, D), lambda b, t: (b, t, 0))
    return pl.pallas_call(
        functools.partial(_ffn_residual_kernel, n_lat_tiles=n_lat_tiles),
        grid=(B, R // tm),
        in_specs=[rows, *_segment_specs(D, n_lat_tiles, 0), pl.BlockSpec((1, D), lambda b, t: (0, 0)),
                  pl.BlockSpec((None, None, 1, D), lambda b, t: (jnp.where(t >= n_lat_tiles, B, b), 5, 0, 0))],
        out_specs=rows,
        out_shape=jax.ShapeDtypeStruct((B, R, D), F32),
        compiler_params=_cparams("parallel", "parallel"),
        name="ffn_residual",
    )(xs, f_lat, f_lat if f_ctx is None else f_ctx, g_post.reshape(1, D), mod4)


def _moe(h, aff_t, weights, layer, n_lat, n_ctx):
    B, _, D = h.shape
    E = aff_t.shape[1]
    f_l = _experts(h, aff_t[:, :, :n_lat], None, weights, layer, CAPACITY_FACTOR * n_lat // E)
    if not n_ctx:
        return f_l, None
    a_c = aff_t[:, :, n_lat:]
    rank_c = _ranks(a_c, jnp.swapaxes(a_c, 1, 2))[:, :, 0, :]
    cap = CAPACITY_FACTOR * n_ctx // E
    slots = jnp.where(rank_c < cap, rank_c + cap * jnp.arange(B, dtype=F32)[:, None, None], -1.0)
    merge = lambda t: jnp.swapaxes(t, 0, 1).reshape(1, E, B * n_ctx)
    f_c = _experts(h[:, n_lat:].reshape(1, B * n_ctx, D), merge(a_c), merge(slots), weights, layer, B * cap)
    return f_l, f_c.reshape(B, n_ctx, D)


def _rope_tables(n_lat, n_ctx):
    inv = ROPE_BASE ** (-jnp.arange(ROPE_HALF, dtype=F32) / ROPE_HALF)
    pos = jnp.arange(n_lat)
    ang_r = (pos // GRID_W).astype(F32)[:, None] * inv[None, :]
    ang_c = (pos % GRID_W).astype(F32)[:, None] * inv[None, :]
    cos = jnp.concatenate([jnp.cos(ang_r)] * 2 + [jnp.cos(ang_c)] * 2, axis=1)
    sin = jnp.concatenate([-jnp.sin(ang_r), jnp.sin(ang_r), -jnp.sin(ang_c), jnp.sin(ang_c)], axis=1)
    cos = jnp.concatenate([jnp.tile(cos, (1, LANES // HEAD_DIM)), jnp.ones((n_ctx, LANES), F32)], axis=0)
    sin = jnp.concatenate([jnp.tile(sin, (1, LANES // HEAD_DIM)), jnp.zeros((n_ctx, LANES), F32)], axis=0)
    return cos, sin


def kernel(x, c, ctx, c_ctx, w_mod, b_mod, g_pre_mix, g_post_mix, g_pre_ffn, g_post_ffn, w_in, w_out, q_gain, k_gain, rpb, rk_w0, rk_w_up, rk_a0, rk_a_up, rk_g_up, rk_k_k, rk_k_a, rk_r_k, rk_lnx_w, rk_lnx_b, w_router, w_e_gate, w_e_up, w_e_down):
    B, n_lat, D = x.shape
    n_ctx = ctx.shape[1]
    T = n_lat + n_ctx
    depth = w_mod.shape[0]
    assert n_lat % ROW_TILE == 0 and n_ctx == ROW_TILE and n_lat % n_ctx == 0 and B < MOD_ROWS
    n_lat_tiles = n_lat // ROW_TILE

    cond = jnp.zeros((MOD_ROWS, D), F32).at[:B].set(c).at[B].set(c_ctx)
    mod = _modulation(cond, w_mod, b_mod)
    cos, sin = _rope_tables(n_lat, n_ctx)
    lane = np.arange(LANES)
    bd = jnp.asarray(lane[:, None] // HEAD_DIM == lane[None, :] // HEAD_DIM, BF16)

    weights = (w_e_gate.astype(BF16), w_e_up.astype(BF16), w_e_down.astype(BF16))
    x_lat, x_ctx, ctx_tile = x, ctx, 0
    for i in range(depth):
        need_ctx = i < depth - 1
        n_rows = T if need_ctx else n_lat
        mod4 = mod[i].reshape(MOD_ROWS, 6, 1, D)
        qa, ka, va, rw, qn, kn, vn = _in_projection(
            x_lat, x_ctx, ctx_tile, mod4, g_pre_mix[i], w_in[i], q_gain[i], k_gain[i], cos, sin, bd, n_lat_tiles)
        oa = _gqa_attention(qa, ka, va, n_lat, n_rows)
        on = _neighbourhood_attention(qn, kn, vn, _natten_bias(rpb[i], n_lat // GRID_W), n_lat, need_ctx)
        rk = dict(w0=rk_w0[i], w_up=rk_w_up[i], a0=rk_a0[i], a_up=rk_a_up[i], g_up=rk_g_up[i],
                  k_k=rk_k_k[i], k_a=rk_k_a[i], r_k=rk_r_k[i], lnx_w=rk_lnx_w[i], lnx_b=rk_lnx_b[i])
        rhat, yhat, gt, ht, gate, bonus = _rwkv_chunks(rw, rk)
        y0, y1 = _rwkv_scan(rhat, yhat, gt, ht, n_lat // CHUNK)
        xs = _out_projection(x_lat, x_ctx, ctx_tile, oa, y0, y1, bonus, gate, on, rk, bd, w_out[i], g_post_mix[i], mod4,
                             n_lat_tiles, n_rows)
        h, aff_t = _router(xs, g_pre_ffn[i], mod4, w_router[i], n_lat_tiles)
        f_lat, f_ctx = _moe(h, aff_t, weights, i, n_lat, n_ctx if need_ctx else 0)
        x_lat = x_ctx = _ffn_residual(xs, f_lat, f_ctx, g_post_ffn[i], mod4, n_lat_tiles)
        ctx_tile = n_lat_tiles
    return x_lat
```

```python
import functools

import numpy as np
import jax
import jax.numpy as jnp
from jax import lax
from jax.experimental import pallas as pl
from jax.experimental.pallas import tpu as pltpu

F32 = jnp.float32
BF16 = jnp.bfloat16

HEAD_DIM = 64
GRID_W = 64
A_HEADS = 8
A_KV_HEADS = 2
B_HEADS = 4
C_HEADS = 4
A_WIDTH = A_HEADS * HEAD_DIM
A_KV_WIDTH = A_KV_HEADS * HEAD_DIM
B_WIDTH = B_HEADS * HEAD_DIM
C_WIDTH = C_HEADS * HEAD_DIM
DECAY_LORA = 64
AAA_LORA = 64
GATE_LORA = 128
RW_WIDTH = 3 * B_WIDTH + 2 * DECAY_LORA + 2 * AAA_LORA + GATE_LORA
ROPE_BASE = 10000.0
ROPE_HALF = HEAD_DIM // 4
WIN_ROWS = 8
WIN_COLS = 16
N_EXPERTS = 16
CAPACITY_FACTOR = 2
NORM_EPS = 1e-6
LNX_EPS = 64e-5
ATTN_SCALE = HEAD_DIM ** -0.5
V_EXT = 2 * HEAD_DIM
MASK_VALUE = -1e30

LANES = 128
ROW_TILE = 256
CHUNK = 64
RWKV_CHUNKS_PER_STEP = 4
NATTEN_ROWS_PER_ITER = 8
MOD_ROWS = 16
VMEM_LIMIT = 56 * 1024 * 1024


def _cparams(*sem):
    return pltpu.CompilerParams(dimension_semantics=sem, vmem_limit_bytes=VMEM_LIMIT)


def _dot(a, b):
    return jnp.dot(a, b, preferred_element_type=F32)


def _dot_nt(a, b):
    return lax.dot_general(a, b, (((1,), (1,)), ((), ())), preferred_element_type=F32)


def _dot_tn(a, b):
    return lax.dot_general(a, b, (((0,), (0,)), ((), ())), preferred_element_type=F32)


def _split2(x):
    hi = x.astype(BF16)
    lo = (x - hi.astype(F32)).astype(BF16)
    return hi, lo


def _split3(x):
    hi = x.astype(BF16)
    r1 = x - hi.astype(F32)
    mid = r1.astype(BF16)
    lo = (r1 - mid.astype(F32)).astype(BF16)
    return hi, mid, lo


def _head_sum(x, bd):
    outs = []
    for c in range(x.shape[1] // LANES):
        hi, lo = _split2(x[:, c * LANES:(c + 1) * LANES])
        outs.append(_dot(hi, bd) + _dot(lo, bd))
    return outs[0] if len(outs) == 1 else jnp.concatenate(outs, axis=1)


def _tile_lanes(t, width):
    reps = width // t.shape[1]
    return t if reps == 1 else jnp.concatenate([t] * reps, axis=1)


def _rms(x, g):
    return x * lax.rsqrt(jnp.mean(x * x, axis=-1, keepdims=True) + NORM_EPS) * g


def _sigmoid(x):
    return 1.0 / (1.0 + jnp.exp(-x))


def _segment_specs(width, n_lat_tiles, ctx_tile):
    return (pl.BlockSpec((None, ROW_TILE, width), lambda b, t: (b, jnp.minimum(t, n_lat_tiles - 1), 0)),
            pl.BlockSpec((None, ROW_TILE, width), lambda b, t: (b, ctx_tile, 0)))


def _segment_rows(lat_ref, ctx_ref, n_lat_tiles):
    return jnp.where(pl.program_id(1) >= n_lat_tiles, ctx_ref[...], lat_ref[...])


def _mod_kernel(s_ref, w_ref, b_ref, o_ref):
    s = s_ref[...]
    s = s * _sigmoid(s)
    o_ref[...] = _dot(s, w_ref[...]) + b_ref[...]


def _modulation(cond, w_mod, b_mod):
    L, D, N = w_mod.shape
    tn = 512
    return pl.pallas_call(
        _mod_kernel,
        grid=(L, N // tn),
        in_specs=[pl.BlockSpec((MOD_ROWS, D), lambda l, j: (0, 0)),
                  pl.BlockSpec((None, D, tn), lambda l, j: (l, 0, j)),
                  pl.BlockSpec((None, 1, tn), lambda l, j: (l, 0, j))],
        out_specs=pl.BlockSpec((None, MOD_ROWS, tn), lambda l, j: (l, 0, j)),
        out_shape=jax.ShapeDtypeStruct((L, MOD_ROWS, N), F32),
        compiler_params=_cparams("parallel", "parallel"),
        name="modulation",
    )(cond, w_mod, b_mod.reshape(L, 1, N))


def _rope(x, cos, sin):
    w = x.shape[1]
    lane = lax.broadcasted_iota(jnp.int32, x.shape, 1)
    upper = (lane // ROPE_HALF) % 2 == 1
    partner = jnp.where(upper, pltpu.roll(x, ROPE_HALF, 1), pltpu.roll(x, w - ROPE_HALF, 1))
    return x * _tile_lanes(cos, w) + partner * _tile_lanes(sin, w)


def _inproj_kernel(xl_ref, xc_ref, g_ref, shift_ref, scale_ref, wqa_ref, wka_ref, wva_ref, wrw_ref, wn_ref,
                   qg_ref, kg_ref, cos_ref, sin_ref, bd_ref, vone_ref,
                   qa_ref, ka_ref, va_ref, rw_ref, qn_ref, kn_ref, vn_ref, *, n_lat_tiles):
    x = _segment_rows(xl_ref, xc_ref, n_lat_tiles)
    h = _rms(x, g_ref[...]) * (1.0 + scale_ref[...]) + shift_ref[...]
    hb = h.astype(BF16)
    bd = bd_ref[...]
    cos, sin = cos_ref[...], sin_ref[...]

    def normed(w_ref, gain_ref):
        y = _dot(hb, w_ref[...])
        ms = _head_sum(y * y, bd) * (1.0 / HEAD_DIM)
        return _rope(y * lax.rsqrt(ms + NORM_EPS) * gain_ref[...], cos, sin)

    def put_heads(o_ref, y, width=HEAD_DIM):
        for hd in range(y.shape[1] // width):
            o_ref[hd] = y[:, hd * width:(hd + 1) * width].astype(o_ref.dtype)

    put_heads(qa_ref, normed(wqa_ref, qg_ref) * ATTN_SCALE)
    put_heads(ka_ref, normed(wka_ref, kg_ref))
    vone = vone_ref[...]
    put_heads(va_ref, _dot(hb, wva_ref[...]) + vone[:, :A_KV_HEADS * V_EXT], V_EXT)
    rw_ref[...] = _dot(hb, wrw_ref[...])
    yn = _dot(hb, wn_ref[...])
    put_heads(qn_ref, yn[:, :C_WIDTH] * ATTN_SCALE)
    put_heads(kn_ref, yn[:, C_WIDTH:2 * C_WIDTH])
    put_heads(vn_ref, yn[:, 2 * C_WIDTH:] + vone, V_EXT)


def _extend_values(w, heads):
    D = w.shape[0]
    w = jnp.pad(w.reshape(D, heads, HEAD_DIM), ((0, 0), (0, 0), (0, V_EXT - HEAD_DIM)))
    return w.reshape(D, heads * V_EXT)


def _in_projection(x_lat, x_ctx, ctx_tile, mod4, g_pre, w_in, q_gain, k_gain, cos, sin, bd, n_lat_tiles):
    B, _, D = x_lat.shape
    tm = ROW_TILE
    T = (n_lat_tiles + 1) * tm
    o = np.cumsum([0, A_WIDTH, A_KV_WIDTH, A_KV_WIDTH, RW_WIDTH, 2 * C_WIDTH, C_WIDTH])
    wb = w_in.astype(BF16)
    wqa, wka, wva, wrw, wqk, wvn = [wb[:, o[i]:o[i + 1]] for i in range(6)]
    ws = [wqa, wka, _extend_values(wva, A_KV_HEADS), wrw,
          jnp.concatenate([wqk, _extend_values(wvn, C_HEADS)], axis=1)]
    vone = jnp.asarray(np.arange(C_HEADS * V_EXT) % V_EXT == HEAD_DIM, F32).reshape(1, C_HEADS * V_EXT)
    const = lambda shape: pl.BlockSpec(shape, lambda b, t: (0,) * len(shape))
    mod_spec = lambda k: pl.BlockSpec(
        (None, None, 1, D), lambda b, t: (jnp.where(t >= n_lat_tiles, B, b), k, 0, 0))
    heads = lambda nh, width=HEAD_DIM: pl.BlockSpec((None, nh, tm, width), lambda b, t: (b, 0, t, 0))
    hshape = lambda nh, width=HEAD_DIM: jax.ShapeDtypeStruct((B, nh, T, width), BF16)
    return pl.pallas_call(
        functools.partial(_inproj_kernel, n_lat_tiles=n_lat_tiles),
        grid=(B, T // tm),
        in_specs=[*_segment_specs(D, n_lat_tiles, ctx_tile), const((1, D)), mod_spec(0), mod_spec(1)]
                 + [const(w.shape) for w in ws]
                 + [const((1, A_WIDTH)), const((1, A_KV_WIDTH)),
                    pl.BlockSpec((tm, LANES), lambda b, t: (t, 0)),
                    pl.BlockSpec((tm, LANES), lambda b, t: (t, 0)),
                    const((LANES, LANES)), const(vone.shape)],
        out_specs=[heads(A_HEADS), heads(A_KV_HEADS), heads(A_KV_HEADS, V_EXT),
                   pl.BlockSpec((None, tm, RW_WIDTH), lambda b, t: (b, t, 0)),
                   heads(C_HEADS), heads(C_HEADS), heads(C_HEADS, V_EXT)],
        out_shape=[hshape(A_HEADS), hshape(A_KV_HEADS), hshape(A_KV_HEADS, V_EXT),
                   jax.ShapeDtypeStruct((B, T, RW_WIDTH), F32),
                   hshape(C_HEADS), hshape(C_HEADS), hshape(C_HEADS, V_EXT)],
        compiler_params=_cparams("parallel", "parallel"),
        name="in_projection",
    )(x_lat, x_ctx, g_pre.reshape(1, D), mod4, mod4, *ws,
      jnp.tile(q_gain, A_HEADS).reshape(1, A_WIDTH), jnp.tile(k_gain, A_KV_HEADS).reshape(1, A_KV_WIDTH),
      cos, sin, bd, vone)


def _softmax_weights(s, m):
    return jnp.exp((s - m).astype(BF16))


def _normalised(ov):
    return ov[:, :HEAD_DIM] / ov[:, HEAD_DIM:HEAD_DIM + 1]


def _gqa_kernel(q_ref, k_ref, v_ref, o_ref, *, n_lat, n_lat_tiles):
    group = q_ref.shape[0]

    def attend(k, v):
        scores = [_dot_nt(q_ref[hd], k) for hd in range(group)]
        probs = [_softmax_weights(s, jnp.max(s, axis=-1, keepdims=True)) for s in scores]
        for hd, p in enumerate(probs):
            o_ref[:, hd * HEAD_DIM:(hd + 1) * HEAD_DIM] = _normalised(_dot(p, v)).astype(o_ref.dtype)

    @pl.when(pl.program_id(2) < n_lat_tiles)
    def _():
        attend(k_ref[...], v_ref[...])

    @pl.when(pl.program_id(2) >= n_lat_tiles)
    def _():
        attend(k_ref[n_lat:, :], v_ref[n_lat:, :])


def _gqa_attention(q, k, v, n_lat, n_rows):
    B, _, T, _ = q.shape
    tq = ROW_TILE
    group = A_HEADS // A_KV_HEADS
    kv_spec = lambda width: pl.BlockSpec((None, None, T, width), lambda b, g, t: (b, g, 0, 0))
    return pl.pallas_call(
        functools.partial(_gqa_kernel, n_lat=n_lat, n_lat_tiles=n_lat // tq),
        grid=(B, A_KV_HEADS, n_rows // tq),
        in_specs=[pl.BlockSpec((None, group, tq, HEAD_DIM), lambda b, g, t: (b, g, t, 0)),
                  kv_spec(HEAD_DIM), kv_spec(V_EXT)],
        out_specs=pl.BlockSpec((None, tq, group * HEAD_DIM), lambda b, g, t: (b, t, g)),
        out_shape=jax.ShapeDtypeStruct((B, n_rows, A_WIDTH), BF16),
        compiler_params=_cparams("parallel", "parallel", "parallel"),
        name="gqa_attention",
    )(q, k, v)


def _natten_kernel(q_ref, k_ref, v_ref, bias_ref, o_ref, *, n_lat, rows, need_ctx):
    win = WIN_ROWS * GRID_W
    for hd in range(C_HEADS):
        kc = k_ref[hd, n_lat:, :]
        vc = v_ref[hd, n_lat:, :]

        def row_group(g, carry, hd=hd, kc=kc, vc=vc):
            scores = []
            for u in range(NATTEN_ROWS_PER_ITER):
                r = g * NATTEN_ROWS_PER_ITER + u
                rs = jnp.clip(r - WIN_ROWS // 2, 0, rows - WIN_ROWS)
                q = q_ref[hd, pl.ds(pl.multiple_of(r * GRID_W, GRID_W), GRID_W), :]
                kw = k_ref[hd, pl.ds(pl.multiple_of(rs * GRID_W, GRID_W), win), :]
                scores.append((r, rs, _dot_nt(q, kw) + bias_ref[r - rs, hd], _dot_nt(q, kc)))
            probs = []
            for r, rs, s_w, s_c in scores:
                m = jnp.maximum(jnp.max(s_w, axis=-1, keepdims=True), jnp.max(s_c, axis=-1, keepdims=True))
                probs.append((r, rs, _softmax_weights(s_w, m), _softmax_weights(s_c, m)))
            for r, rs, p_w, p_c in probs:
                vw = v_ref[hd, pl.ds(pl.multiple_of(rs * GRID_W, GRID_W), win), :]
                o = _normalised(_dot(p_w, vw) + _dot(p_c, vc))
                o_ref[pl.ds(pl.multiple_of(r * GRID_W, GRID_W), GRID_W),
                      hd * HEAD_DIM:(hd + 1) * HEAD_DIM] = o.astype(o_ref.dtype)
            return carry

        lax.fori_loop(0, rows // NATTEN_ROWS_PER_ITER, row_group, 0)
        if need_ctx:
            s = _dot_nt(q_ref[hd, n_lat:, :], kc)
            o = _normalised(_dot(_softmax_weights(s, jnp.max(s, axis=-1, keepdims=True)), vc))
            o_ref[n_lat:, hd * HEAD_DIM:(hd + 1) * HEAD_DIM] = o.astype(o_ref.dtype)


def _natten_bias(rpb, rows):
    off = np.arange(WIN_ROWS)[:, None, None]
    jr = np.arange(WIN_ROWS)[None, :, None]
    row_sel = (np.arange(2 * WIN_ROWS - 1)[None, None, :] == jr - off + WIN_ROWS - 1)
    qc = np.arange(GRID_W)[:, None, None]
    kc = np.arange(GRID_W)[None, :, None]
    col_start = np.clip(qc - WIN_COLS // 2, 0, GRID_W - WIN_COLS)
    valid = (kc >= col_start) & (kc < col_start + WIN_COLS)
    col_sel = (np.arange(2 * WIN_COLS - 1)[None, None, :] == kc - qc + WIN_COLS - 1) & valid
    bias = jnp.einsum("hrc,ojr,qkc->ohqjk", rpb.astype(F32), jnp.asarray(row_sel, F32), jnp.asarray(col_sel, F32),
                      precision=lax.Precision.HIGHEST)
    bias = bias + jnp.asarray(np.where(valid[None, None, :, None, :, 0], 0.0, MASK_VALUE), F32)
    return bias.reshape(WIN_ROWS, C_HEADS, GRID_W, WIN_ROWS * GRID_W)


def _neighbourhood_attention(q, k, v, bias, n_lat, need_ctx):
    B, _, T, _ = q.shape
    n_rows = T if need_ctx else n_lat
    full = lambda width: pl.BlockSpec((None, C_HEADS, T, width), lambda b: (b, 0, 0, 0))
    return pl.pallas_call(
        functools.partial(_natten_kernel, n_lat=n_lat, rows=n_lat // GRID_W, need_ctx=need_ctx),
        grid=(B,),
        in_specs=[full(HEAD_DIM), full(HEAD_DIM), full(V_EXT), pl.BlockSpec(bias.shape, lambda b: (0, 0, 0, 0))],
        out_specs=pl.BlockSpec((None, n_rows, C_WIDTH), lambda b: (b, 0, 0)),
        out_shape=jax.ShapeDtypeStruct((B, n_rows, C_WIDTH), BF16),
        compiler_params=_cparams("parallel"),
        name="neighbourhood_attention",
    )(q, k, v, bias)


def _rwkv_chunk_kernel(rw_ref, w0_ref, a0_ref, lora_ref, kk_ref, ka_ref, rk_ref, bdw_ref,
                       rhat_ref, yhat_ref, gt_ref, ht_ref, gate_ref, bonus_ref):
    C, W, NH = CHUNK, B_WIDTH, B_HEADS
    S = NH * C
    R = rw_ref.shape[0]
    rw = rw_ref[...]
    r, k, v = rw[:, :W], rw[:, W:2 * W], rw[:, 2 * W:3 * W]
    wl = rw[:, 3 * W:3 * W + 2 * DECAY_LORA]
    al = rw[:, 3 * W + 2 * DECAY_LORA:3 * W + 2 * DECAY_LORA + 2 * AAA_LORA]
    gl = rw[:, 3 * W + 2 * DECAY_LORA + 2 * AAA_LORA:]
    bdw = bdw_ref[...]

    def head_sum(x):
        hi, lo = _split2(x)
        p = _dot(jnp.concatenate([hi, lo], axis=0), bdw)
        return p[:R] + p[R:]

    kkv = k * kk_ref[...]
    kkv = kkv / jnp.maximum(jnp.sqrt(head_sum(kkv * kkv)), 1e-12)
    lora = _dot(jnp.concatenate([jnp.tanh(wl), al, _sigmoid(gl)], axis=1), lora_ref[...])
    gate_ref[...] = lora[:, 4 * W:]
    lws, asigs = [], []
    for d in range(2):
        w = w0_ref[d] + lora[:, d * W:(d + 1) * W]
        w = -(jnp.maximum(-w, 0.0) + jnp.log(1.0 + jnp.exp(-jnp.abs(w)))) - 0.5
        lws.append(-jnp.exp(w))
        asigs.append(_sigmoid(a0_ref[d] + lora[:, (2 + d) * W:(3 + d) * W]))

    prow = lax.broadcasted_iota(jnp.int32, (R, R), 0)
    pcol = lax.broadcasted_iota(jnp.int32, (R, R), 1)
    tri = ((prow >= pcol) & (prow // C == pcol // C)).astype(BF16)
    cs = _dot(tri, jnp.concatenate([p for lw in lws for p in _split3(lw)], axis=1))
    prefix = [cs[:, 3 * d * W:(3 * d + 1) * W] + cs[:, (3 * d + 1) * W:(3 * d + 2) * W]
              + cs[:, (3 * d + 2) * W:(3 * d + 3) * W] for d in range(2)]

    srow = lax.broadcasted_iota(jnp.int32, (S, W), 0)
    scol = lax.broadcasted_iota(jnp.int32, (S, W), 1)
    same = (srow // C) == (scol // HEAD_DIM)
    crow = lax.broadcasted_iota(jnp.int32, (C, W), 0)
    cpos = lax.broadcasted_iota(jnp.int32, (C, W), 1) % HEAD_DIM
    eye_c = crow == cpos
    eye_b = eye_c.astype(BF16)

    def stack(x):
        return jnp.where(same, jnp.concatenate([x] * NH, axis=0), 0.0).astype(BF16)

    kds = [k * (1.0 + (asig - 1.0) * ka_ref[...]) for asig in asigs]
    bonus_ref[...] = head_sum(r * (kds[0] + kds[1]) * rk_ref[...]) * v

    class Chain:
        pass

    chains = []
    for ci in range(R // C):
        v_s = stack(v[ci * C:(ci + 1) * C])
        for d in range(2):
            ch = Chain()
            ch.d, ch.ci, ch.rows, ch.v_s = d, ci, slice(ci * C, (ci + 1) * C), v_s
            chains.append(ch)

    for ch in chains:
        d, rows = ch.d, ch.rows
        lw, kd, b, a = lws[d][rows], kds[d][rows], (kkv * asigs[d])[rows], -kkv[rows]
        pre = prefix[d][rows]
        ch.tot = tot = pre[C - 1:C, :]
        cum_incl = pre if d == 0 else tot - pre + lw
        cum_excl = cum_incl - lw
        rho = 0.5 * tot
        e_in = jnp.exp(rho - cum_incl)
        e_end = jnp.exp(tot - cum_incl)
        ch.e_rho = jnp.exp(rho)
        ch.a_rho = a * jnp.exp(cum_excl - rho)
        ch.r_rho = r[rows] * jnp.exp(cum_incl - rho)
        ch.ends = jnp.concatenate([stack(b * e_end), stack(kd * e_end)], axis=0)
        ch.m = _dot_nt(jnp.concatenate([ch.a_rho, ch.r_rho], axis=0).astype(BF16),
                       jnp.concatenate([stack(b * e_in), stack(kd * e_in)], axis=0))

    for ch in chains:
        before, upto = (crow > cpos, crow >= cpos) if ch.d == 0 else (crow < cpos, crow <= cpos)
        m = ch.m
        l_ab = jnp.where(before, m[:C, :W], 0.0)
        ch.a_rb = jnp.where(upto, m[C:, :W], 0.0).astype(BF16)
        ch.kv = jnp.concatenate([jnp.where(before, m[:C, W:], 0.0), jnp.where(upto, m[C:, W:], 0.0)],
                                axis=0).astype(BF16)
        ch.lp = _dot(l_ab.astype(BF16), stack(l_ab))
        ch.tinv = jnp.where(eye_c, 1.0, l_ab)
    for _ in range(int(np.log2(C)) - 2):
        for ch in chains:
            p = _dot(jnp.concatenate([ch.tinv, ch.lp], axis=0).astype(BF16), stack(ch.lp))
            ch.tinv = ch.tinv + p[:C]
            ch.lp = p[C:]
    for ch in chains:
        ch.tinv = ch.tinv + _dot(ch.tinv.astype(BF16), stack(ch.lp))
    for ch in chains:
        ch.wv = _dot(ch.kv, ch.v_s)
    for ch in chains:
        ch.z = _dot(ch.tinv.astype(BF16),
                    jnp.concatenate([stack(ch.a_rho), stack(ch.wv[:C])], axis=1))
    for ch in chains:
        ch.ahat = ch.z[:, :W] * ch.e_rho
        ch.av = jnp.concatenate([stack(ch.z[:, :W]), stack(ch.z[:, W:])], axis=1)
        rz = _dot(ch.a_rb, ch.av)
        rhat_ref[ch.d, ch.rows] = (ch.r_rho + rz[:, :W]) * ch.e_rho
        yhat_ref[ch.d, ch.rows] = rz[:, W:] + ch.wv[C:]
    for ch in chains:
        ch.ends_t = _dot_nt(eye_b, ch.ends).astype(BF16)
    for ch in chains:
        gh = _dot(ch.ends_t[:, :S], jnp.concatenate([stack(ch.ahat), ch.av[:, W:]], axis=1))
        gt_ref[ch.d, ch.ci] = gh[:, :W] + jnp.where(eye_c, jnp.exp(ch.tot), 0.0)
        ht_ref[ch.d, ch.ci] = gh[:, W:] + _dot(ch.ends_t[:, S:], ch.v_s)


def _rwkv_chunks(rw, p):
    B, T, _ = rw.shape
    C, W = CHUNK, B_WIDTH
    nc = T // C
    per = RWKV_CHUNKS_PER_STEP
    assert C == HEAD_DIM and nc % per == 0
    const = lambda shape: pl.BlockSpec(shape, lambda b, c: (0,) * len(shape))
    rows = pl.BlockSpec((None, 2, per * C, W), lambda b, c: (b, 0, c, 0))
    mats = pl.BlockSpec((None, 2, per, C, W), lambda b, c: (b, 0, c, 0, 0))
    flat = pl.BlockSpec((None, per * C, W), lambda b, c: (b, c, 0))
    vec = lambda a: a.reshape(1, W)
    lora_w = jnp.zeros((2 * DECAY_LORA + 2 * AAA_LORA + GATE_LORA, 5 * W), F32)
    for d in range(2):
        lora_w = lora_w.at[d * DECAY_LORA:(d + 1) * DECAY_LORA, d * W:(d + 1) * W].set(p["w_up"][d])
        lora_w = lora_w.at[2 * DECAY_LORA + d * AAA_LORA:2 * DECAY_LORA + (d + 1) * AAA_LORA,
                           (2 + d) * W:(3 + d) * W].set(p["a_up"][d])
    lora_w = lora_w.at[2 * DECAY_LORA + 2 * AAA_LORA:, 4 * W:].set(p["g_up"])
    lane = np.arange(W)
    bdw = jnp.asarray(lane[:, None] // HEAD_DIM == lane[None, :] // HEAD_DIM, BF16)
    return pl.pallas_call(
        _rwkv_chunk_kernel,
        grid=(B, nc // per),
        in_specs=[pl.BlockSpec((None, per * C, RW_WIDTH), lambda b, c: (b, c, 0)),
                  const((2, 1, W)), const((2, 1, W)), const(lora_w.shape),
                  const((1, W)), const((1, W)), const((1, W)), const((W, W))],
        out_specs=[rows, rows, mats, mats, flat, flat],
        out_shape=[jax.ShapeDtypeStruct((B, 2, T, W), F32), jax.ShapeDtypeStruct((B, 2, T, W), F32),
                   jax.ShapeDtypeStruct((B, 2, nc, C, W), F32), jax.ShapeDtypeStruct((B, 2, nc, C, W), F32),
                   jax.ShapeDtypeStruct((B, T, W), F32), jax.ShapeDtypeStruct((B, T, W), F32)],
        compiler_params=_cparams("parallel", "parallel"),
        name="rwkv_chunks",
    )(rw, p["w0"].reshape(2, 1, W), p["a0"].reshape(2, 1, W), lora_w,
      vec(p["k_k"]), vec(p["k_a"]), vec(p["r_k"]), bdw)


def _rwkv_scan_kernel(rhat0_ref, yhat0_ref, gt0_ref, ht0_ref, rhat1_ref, yhat1_ref, gt1_ref, ht1_ref,
                      y0_ref, y1_ref, st_ref):
    @pl.when(pl.program_id(0) == 0)
    def _():
        st_ref[...] = jnp.zeros_like(st_ref)

    W = B_WIDTH
    same = (lax.broadcasted_iota(jnp.int32, (W, W), 0) // HEAD_DIM
            == lax.broadcasted_iota(jnp.int32, (W, W), 1) // HEAD_DIM)
    expand = lambda x: jnp.where(same, jnp.concatenate([x] * B_HEADS, axis=0), 0.0)
    dirs = ((rhat0_ref, yhat0_ref, gt0_ref, ht0_ref, y0_ref), (rhat1_ref, yhat1_ref, gt1_ref, ht1_ref, y1_ref))
    for b in range(st_ref.shape[1]):
        for d, (rhat_ref, yhat_ref, gt_ref, ht_ref, y_ref) in enumerate(dirs):
            stb = st_ref[d, b].astype(BF16)
            y_ref[b] = _dot(rhat_ref[b].astype(BF16), stb) + yhat_ref[b]
            st_ref[d, b] = _dot(expand(gt_ref[b]).astype(BF16), stb) + expand(ht_ref[b])


def _rwkv_scan(rhat, yhat, gt, ht, n_lat_chunks):
    B, _, T, W = rhat.shape
    C = CHUNK
    nc = T // C
    chunk = (lambda s: (s + n_lat_chunks) % nc, lambda s: nc - 1 - s)
    rows = lambda d: pl.BlockSpec((B, None, C, W), lambda s: (0, d, chunk[d](s), 0))
    mats = lambda d: pl.BlockSpec((B, None, None, C, W), lambda s: (0, d, chunk[d](s), 0, 0))
    outs = lambda d: pl.BlockSpec((B, C, W), lambda s: (0, chunk[d](s), 0))
    return pl.pallas_call(
        _rwkv_scan_kernel,
        grid=(nc,),
        in_specs=[rows(0), rows(0), mats(0), mats(0), rows(1), rows(1), mats(1), mats(1)],
        out_specs=[outs(0), outs(1)],
        out_shape=[jax.ShapeDtypeStruct((B, T, W), F32)] * 2,
        scratch_shapes=[pltpu.VMEM((2, B, W, W), F32)],
        compiler_params=_cparams("arbitrary"),
        name="rwkv_scan",
    )(rhat, yhat, gt, ht, rhat, yhat, gt, ht)


def _outproj_kernel(xl_ref, xc_ref, oa_ref, y0_ref, y1_ref, bonus_ref, gate_ref, on_ref, lnw_ref, lnb_ref, bd_ref,
                    wa_ref, wb_ref, wn_ref, g_ref, mgate_ref, o_ref, *, n_lat_tiles):
    y = y0_ref[...] + y1_ref[...]
    bd = bd_ref[...]
    mu = _head_sum(y, bd) * (1.0 / HEAD_DIM)
    yc = y - mu
    var = _head_sum(yc * yc, bd) * (1.0 / HEAD_DIM)
    ob = (yc * lax.rsqrt(var + LNX_EPS) * lnw_ref[...] + lnb_ref[...] + bonus_ref[...]) * gate_ref[...]
    out = _dot(oa_ref[...], wa_ref[...]) + _dot(ob.astype(BF16), wb_ref[...]) + _dot(on_ref[...], wn_ref[...])
    o_ref[...] = _segment_rows(xl_ref, xc_ref, n_lat_tiles) + mgate_ref[...] * _rms(out, g_ref[...])


def _out_projection(x_lat, x_ctx, ctx_tile, oa, y0, y1, bonus, gate, on, p, bd, w_out, g_post, mod4,
                    n_lat_tiles, n_rows):
    B, _, D = x_lat.shape
    tm = ROW_TILE
    W = B_WIDTH
    wb = w_out.astype(BF16)
    const = lambda shape: pl.BlockSpec(shape, lambda b, t: (0,) * len(shape))
    rows = lambda width: pl.BlockSpec((None, tm, width), lambda b, t: (b, t, 0))
    return pl.pallas_call(
        functools.partial(_outproj_kernel, n_lat_tiles=n_lat_tiles),
        grid=(B, n_rows // tm),
        in_specs=[*_segment_specs(D, n_lat_tiles, ctx_tile),
                  rows(A_WIDTH), rows(W), rows(W), rows(W), rows(W), rows(C_WIDTH),
                  const((1, W)), const((1, W)), const((LANES, LANES)),
                  const((A_WIDTH, D)), const((W, D)), const((C_WIDTH, D)), const((1, D)),
                  pl.BlockSpec((None, None, 1, D), lambda b, t: (jnp.where(t >= n_lat_tiles, B, b), 2, 0, 0))],
        out_specs=rows(D),
        out_shape=jax.ShapeDtypeStruct((B, n_rows, D), F32),
        compiler_params=_cparams("parallel", "parallel"),
        name="out_projection",
    )(x_lat, x_ctx, oa, y0, y1, bonus, gate, on, p["lnx_w"].reshape(1, W), p["lnx_b"].reshape(1, W), bd,
      wb[:A_WIDTH], wb[A_WIDTH:A_WIDTH + W], wb[A_WIDTH + W:], g_post.reshape(1, D), mod4)


def _router_kernel(x_ref, g_ref, shift_ref, scale_ref, wr_ref, h_ref, aff_ref):
    h = _rms(x_ref[...], g_ref[...]) * (1.0 + scale_ref[...]) + shift_ref[...]
    h_ref[...] = h.astype(h_ref.dtype)
    logits = _dot_nt(wr_ref[...], h)
    p = jnp.exp(logits - jnp.max(logits, axis=0, keepdims=True))
    aff_ref[...] = p / jnp.sum(p, axis=0, keepdims=True)


def _router(xs, g_pre, mod4, w_router, n_lat_tiles):
    B, R, D = xs.shape
    tm = ROW_TILE
    E = w_router.shape[1]
    mod_spec = lambda k: pl.BlockSpec(
        (None, None, 1, D), lambda b, t: (jnp.where(t >= n_lat_tiles, B, b), k, 0, 0))
    return pl.pallas_call(
        _router_kernel,
        grid=(B, R // tm),
        in_specs=[pl.BlockSpec((None, tm, D), lambda b, t: (b, t, 0)),
                  pl.BlockSpec((1, D), lambda b, t: (0, 0)), mod_spec(3), mod_spec(4),
                  pl.BlockSpec((E, D), lambda b, t: (0, 0))],
        out_specs=[pl.BlockSpec((None, tm, D), lambda b, t: (b, t, 0)),
                   pl.BlockSpec((None, E, tm), lambda b, t: (b, 0, t))],
        out_shape=[jax.ShapeDtypeStruct((B, R, D), BF16), jax.ShapeDtypeStruct((B, E, R), F32)],
        compiler_params=_cparams("parallel", "parallel"),
        name="moe_router",
    )(xs, g_pre.reshape(1, D), mod4, mod4, w_router.T)


def _column(ref, e):
    lane = lax.broadcasted_iota(jnp.int32, ref.shape, 1)
    return jnp.sum(jnp.where(lane == e, ref[...], 0.0), axis=1, keepdims=True)


RANK_BLOCK = 256


def _rank_partials(a_row, a_col):
    n = a_row.shape[-1]
    tj = min(n, RANK_BLOCK)
    as_count = lambda mask: mask.astype(F32).astype(BF16)

    def partial(jc):
        lo, hi = jc * tj, (jc + 1) * tj
        aj = a_col[lo:hi, :]
        ai = a_row[:, lo:hi]
        j_first = (lax.broadcasted_iota(jnp.int32, (tj, tj), 0) < lax.broadcasted_iota(jnp.int32, (tj, tj), 1))
        parts = [as_count(aj > a_row[:, :lo])] if lo else []
        parts.append(as_count((aj > ai) | ((aj == ai) & j_first)))
        if hi < n:
            parts.append(as_count(aj >= a_row[:, hi:]))
        return _dot(jnp.ones((8, tj), BF16), parts[0] if len(parts) == 1 else jnp.concatenate(parts, axis=1))

    return [functools.partial(partial, jc) for jc in range(n // tj)]


def _token_ranks(a_row, a_col):
    cnt = None
    for thunk in _rank_partials(a_row, a_col):
        cnt = thunk() if cnt is None else cnt + thunk()
    return cnt[0:1, :]


def _propose_threshold(aff, cap):
    keys = lax.bitcast_convert_type(aff, jnp.int32)
    t = jnp.zeros((aff.shape[0], 1), jnp.int32)
    for bit in range(30, -1, -1):
        cand = t | (1 << bit)
        cnt = jnp.sum((keys >= cand).astype(F32), axis=1, keepdims=True)
        t = jnp.where(cnt >= cap, cand, t)
    return lax.bitcast_convert_type(t, F32)


def _prefix_count(mask):
    n = mask.shape[1]
    blk = min(n, RANK_BLOCK)
    tri = (lax.broadcasted_iota(jnp.int32, (blk, blk), 0) <= lax.broadcasted_iota(jnp.int32, (blk, blk), 1))
    tri = tri.astype(F32).astype(BF16)
    ones = mask.astype(F32).astype(BF16)
    outs = []
    for c in range(n // blk):
        p = _dot(ones[:, c * blk:(c + 1) * blk], tri)
        outs.append(p + outs[-1][:, blk - 1:blk] if outs else p)
    return outs[0] if len(outs) == 1 else jnp.concatenate(outs, axis=1)


def _slots_from_threshold(aff, tau, cap):
    above, equal = aff > tau, aff == tau
    n_above = jnp.sum(above.astype(F32), axis=1, keepdims=True)
    n_equal = jnp.sum(equal.astype(F32), axis=1, keepdims=True)
    ok = (n_above < cap) & (n_above + n_equal >= cap)
    take = above | (equal & (_prefix_count(equal) <= cap - n_above))
    return jnp.where(take, _prefix_count(take) - 1.0, -1.0), ok


def _rank_kernel(arow_ref, acol_ref, rank_ref):
    rank_ref[...] = _token_ranks(arow_ref[...], _column(acol_ref, pl.program_id(1)))


def _ranks(aff_t, aff_c):
    B, E, n = aff_t.shape
    return pl.pallas_call(
        _rank_kernel,
        grid=(B, E),
        in_specs=[pl.BlockSpec((None, None, 1, n), lambda b, e: (b, e, 0, 0)),
                  pl.BlockSpec((None, n, E), lambda b, e: (b, 0, 0))],
        out_specs=pl.BlockSpec((None, None, 1, n), lambda b, e: (b, e, 0, 0)),
        out_shape=jax.ShapeDtypeStruct((B, E, 1, n), F32),
        compiler_params=_cparams("parallel", "parallel"),
        name="moe_ranks",
    )(aff_t.reshape(B, E, 1, n), aff_c)


def _expert_kernel(h_ref, aff_ref, rank_src_ref, wg_ref, wu_ref, wd_ref, f_ref, yet_ref, ye_ref, ft_ref,
                   *rank_scratch, cap):
    n = h_ref.shape[0]
    E = aff_ref.shape[0]
    e = pl.program_id(1)
    if rank_scratch:
        rank_ref, = rank_scratch

        @pl.when(e == 0)
        def _():
            aff = aff_ref[...]
            slots, ok = _slots_from_threshold(aff, _propose_threshold(aff, cap), cap)
            rank_ref[...] = slots

            @pl.when(jnp.sum(jnp.where(ok, 0.0, 1.0)) > 0.0)
            def _():
                def rerank(x, carry):
                    rank_ref[pl.ds(x, 1), :] = _token_ranks(aff_ref[pl.ds(x, 1), :], _column(rank_src_ref, x))
                    return carry
                lax.fori_loop(0, E, rerank, 0)
    else:
        rank_ref = rank_src_ref
    slot = lax.broadcasted_iota(jnp.int32, (cap, n), 0).astype(F32)
    pick = (rank_ref[pl.ds(e, 1), :] == slot).astype(F32).astype(BF16)
    xe = _dot(pick, h_ref[...]).astype(BF16)
    gte = _dot(xe, wg_ref[...])
    hid = (gte * _sigmoid(gte) * _dot(xe, wu_ref[...])).astype(BF16)
    ye_ref[...] = _dot(hid, wd_ref[...])
    yet_ref[:, pl.ds(pl.multiple_of(e * cap, cap), cap)] = ye_ref[...].T.astype(BF16)

    @pl.when(e == E - 1)
    def _():
        tn = min(n, ROW_TILE)
        slot_c = lax.broadcasted_iota(jnp.int32, (cap, tn), 0).astype(F32)
        for c in range(n // tn):
            cols = slice(c * tn, (c + 1) * tn)
            put = jnp.concatenate(
                [jnp.where(rank_ref[x:x + 1, cols] == slot_c, aff_ref[x:x + 1, cols], 0.0).astype(BF16)
                 for x in range(E)], axis=0)
            ft_ref[...] = _dot(yet_ref[...], put)
            f_ref[cols, :] = ft_ref[...].T


def _experts(h, aff, slots, weights, layer, cap):
    S, E, n = aff.shape
    D = h.shape[-1]
    F = weights[0].shape[-1]
    assert cap % LANES == 0
    rows = pl.BlockSpec((None, E, n), lambda s, e: (s, 0, 0))
    if slots is None:
        rank_src, rank_spec = jnp.swapaxes(aff, 1, 2), pl.BlockSpec((None, n, E), lambda s, e: (s, 0, 0))
    else:
        rank_src, rank_spec = slots, rows
    weight = lambda a, b: pl.BlockSpec((None, None, a, b), lambda s, e: (layer, e, 0, 0))
    return pl.pallas_call(
        functools.partial(_expert_kernel, cap=cap),
        grid=(S, E),
        in_specs=[pl.BlockSpec((None, n, D), lambda s, e: (s, 0, 0)), rows, rank_spec,
                  weight(D, F), weight(D, F), weight(F, D)],
        out_specs=pl.BlockSpec((None, n, D), lambda s, e: (s, 0, 0)),
        out_shape=jax.ShapeDtypeStruct((S, n, D), F32),
        scratch_shapes=[pltpu.VMEM((D, E * cap), BF16), pltpu.VMEM((cap, D), F32),
                        pltpu.VMEM((D, min(n, ROW_TILE)), F32)]
                       + ([pltpu.VMEM((E, n), F32)] if slots is None else []),
        compiler_params=_cparams("parallel", "arbitrary"),
        name="moe_experts",
    )(h, aff, rank_src, *weights)


def _ffn_residual_kernel(x_ref, fl_ref, fc_ref, g_ref, mgate_ref, o_ref, *, n_lat_tiles):
    f = _segment_rows(fl_ref, fc_ref, n_lat_tiles)
    o_ref[...] = x_ref[...] + mgate_ref[...] * _rms(f, g_ref[...])


def _ffn_residual(xs, f_lat, f_ctx, g_post, mod4, n_lat_tiles):
    B, R, D = xs.shape
    tm = ROW_TILE
    rows = pl.BlockSpec((None, tm, D), lambda b, t: (b, t, 0))
    return pl.pallas_call(
        functools.partial(_ffn_residual_kernel, n_lat_tiles=n_lat_tiles),
        grid=(B, R // tm),
        in_specs=[rows, *_segment_specs(D, n_lat_tiles, 0), pl.BlockSpec((1, D), lambda b, t: (0, 0)),
                  pl.BlockSpec((None, None, 1, D), lambda b, t: (jnp.where(t >= n_lat_tiles, B, b), 5, 0, 0))],
        out_specs=rows,
        out_shape=jax.ShapeDtypeStruct((B, R, D), F32),
        compiler_params=_cparams("parallel", "parallel"),
        name="ffn_residual",
    )(xs, f_lat, f_lat if f_ctx is None else f_ctx, g_post.reshape(1, D), mod4)


def _moe(h, aff_t, weights, layer, n_lat, n_ctx):
    B, _, D = h.shape
    E = aff_t.shape[1]
    f_l = _experts(h, aff_t[:, :, :n_lat], None, weights, layer, CAPACITY_FACTOR * n_lat // E)
    if not n_ctx:
        return f_l, None
    a_c = aff_t[:, :, n_lat:]
    rank_c = _ranks(a_c, jnp.swapaxes(a_c, 1, 2))[:, :, 0, :]
    cap = CAPACITY_FACTOR * n_ctx // E
    slots = jnp.where(rank_c < cap, rank_c + cap * jnp.arange(B, dtype=F32)[:, None, None], -1.0)
    merge = lambda t: jnp.swapaxes(t, 0, 1).reshape(1, E, B * n_ctx)
    f_c = _experts(h[:, n_lat:].reshape(1, B * n_ctx, D), merge(a_c), merge(slots), weights, layer, B * cap)
    return f_l, f_c.reshape(B, n_ctx, D)


def _rope_tables(n_lat, n_ctx):
    inv = ROPE_BASE ** (-jnp.arange(ROPE_HALF, dtype=F32) / ROPE_HALF)
    pos = jnp.arange(n_lat)
    ang_r = (pos // GRID_W).astype(F32)[:, None] * inv[None, :]
    ang_c = (pos % GRID_W).astype(F32)[:, None] * inv[None, :]
    cos = jnp.concatenate([jnp.cos(ang_r)] * 2 + [jnp.cos(ang_c)] * 2, axis=1)
    sin = jnp.concatenate([-jnp.sin(ang_r), jnp.sin(ang_r), -jnp.sin(ang_c), jnp.sin(ang_c)], axis=1)
    cos = jnp.concatenate([jnp.tile(cos, (1, LANES // HEAD_DIM)), jnp.ones((n_ctx, LANES), F32)], axis=0)
    sin = jnp.concatenate([jnp.tile(sin, (1, LANES // HEAD_DIM)), jnp.zeros((n_ctx, LANES), F32)], axis=0)
    return cos, sin


def kernel(x, c, ctx, c_ctx, w_mod, b_mod, g_pre_mix, g_post_mix, g_pre_ffn, g_post_ffn, w_in, w_out, q_gain, k_gain, rpb, rk_w0, rk_w_up, rk_a0, rk_a_up, rk_g_up, rk_k_k, rk_k_a, rk_r_k, rk_lnx_w, rk_lnx_b, w_router, w_e_gate, w_e_up, w_e_down):
    B, n_lat, D = x.shape
    n_ctx = ctx.shape[1]
    T = n_lat + n_ctx
    depth = w_mod.shape[0]
    assert n_lat % ROW_TILE == 0 and n_ctx == ROW_TILE and n_lat % n_ctx == 0 and B < MOD_ROWS
    n_lat_tiles = n_lat // ROW_TILE

    cond = jnp.zeros((MOD_ROWS, D), F32).at[:B].set(c).at[B].set(c_ctx)
    mod = _modulation(cond, w_mod, b_mod)
    cos, sin = _rope_tables(n_lat, n_ctx)
    lane = np.arange(LANES)
    bd = jnp.asarray(lane[:, None] // HEAD_DIM == lane[None, :] // HEAD_DIM, BF16)

    weights = (w_e_gate.astype(BF16), w_e_up.astype(BF16), w_e_down.astype(BF16))
    x_lat, x_ctx, ctx_tile = x, ctx, 0
    for i in range(depth):
        need_ctx = i < depth - 1
        n_rows = T if need_ctx else n_lat
        mod4 = mod[i].reshape(MOD_ROWS, 6, 1, D)
        qa, ka, va, rw, qn, kn, vn = _in_projection(
            x_lat, x_ctx, ctx_tile, mod4, g_pre_mix[i], w_in[i], q_gain[i], k_gain[i], cos, sin, bd, n_lat_tiles)
        oa = _gqa_attention(qa, ka, va, n_lat, n_rows)
        on = _neighbourhood_attention(qn, kn, vn, _natten_bias(rpb[i], n_lat // GRID_W), n_lat, need_ctx)
        rk = dict(w0=rk_w0[i], w_up=rk_w_up[i], a0=rk_a0[i], a_up=rk_a_up[i], g_up=rk_g_up[i],
                  k_k=rk_k_k[i], k_a=rk_k_a[i], r_k=rk_r_k[i], lnx_w=rk_lnx_w[i], lnx_b=rk_lnx_b[i])
        rhat, yhat, gt, ht, gate, bonus = _rwkv_chunks(rw, rk)
        y0, y1 = _rwkv_scan(rhat, yhat, gt, ht, n_lat // CHUNK)
        xs = _out_projection(x_lat, x_ctx, ctx_tile, oa, y0, y1, bonus, gate, on, rk, bd, w_out[i], g_post_mix[i], mod4,
                             n_lat_tiles, n_rows)
        h, aff_t = _router(xs, g_pre_ffn[i], mod4, w_router[i], n_lat_tiles)
        f_lat, f_ctx = _moe(h, aff_t, weights, i, n_lat, n_ctx if need_ctx else 0)
        x_lat = x_ctx = _ffn_residual(xs, f_lat, f_ctx, g_post_ffn[i], mod4, n_lat_tiles)
        ctx_tile = n_lat_tiles
    return x_lat
```

```python
import functools

import numpy as np
import jax
import jax.numpy as jnp
from jax import lax
from jax.experimental import pallas as pl
from jax.experimental.pallas import tpu as pltpu

F32 = jnp.float32
BF16 = jnp.bfloat16

HEAD_DIM = 64
GRID_W = 64
A_HEADS = 8
A_KV_HEADS = 2
B_HEADS = 4
C_HEADS = 4
A_WIDTH = A_HEADS * HEAD_DIM
A_KV_WIDTH = A_KV_HEADS * HEAD_DIM
B_WIDTH = B_HEADS * HEAD_DIM
C_WIDTH = C_HEADS * HEAD_DIM
DECAY_LORA = 64
AAA_LORA = 64
GATE_LORA = 128
RW_WIDTH = 3 * B_WIDTH + 2 * DECAY_LORA + 2 * AAA_LORA + GATE_LORA
ROPE_BASE = 10000.0
ROPE_HALF = HEAD_DIM // 4
WIN_ROWS = 8
WIN_COLS = 16
N_EXPERTS = 16
CAPACITY_FACTOR = 2
NORM_EPS = 1e-6
LNX_EPS = 64e-5
ATTN_SCALE = HEAD_DIM ** -0.5
V_EXT = 2 * HEAD_DIM
MASK_VALUE = -1e30

LANES = 128
HEAD_SUM_TILE = 256
ROW_TILE = 256
CHUNK = 64
RWKV_CHUNKS_PER_STEP = 4
NATTEN_ROWS_PER_ITER = 8
MOD_ROWS = 16
VMEM_LIMIT = 56 * 1024 * 1024


def _cparams(*sem):
    return pltpu.CompilerParams(dimension_semantics=sem, vmem_limit_bytes=VMEM_LIMIT)


def _dot(a, b):
    return jnp.dot(a, b, preferred_element_type=F32)


def _dot_nt(a, b):
    return lax.dot_general(a, b, (((1,), (1,)), ((), ())), preferred_element_type=F32)


def _dot_tn(a, b):
    return lax.dot_general(a, b, (((0,), (0,)), ((), ())), preferred_element_type=F32)


def _split2(x):
    hi = x.astype(BF16)
    lo = (x - hi.astype(F32)).astype(BF16)
    return hi, lo


def _split3(x):
    hi = x.astype(BF16)
    r1 = x - hi.astype(F32)
    mid = r1.astype(BF16)
    lo = (r1 - mid.astype(F32)).astype(BF16)
    return hi, mid, lo


def _head_sum(x, bd):
    rows, width = x.shape
    tile = bd.shape[0]
    outs = []
    for c in range(0, width, tile):
        w = min(tile, width - c)
        hi, lo = _split2(x[:, c:c + w])
        p = _dot(jnp.concatenate([hi, lo], axis=0), bd[:w, :w])
        outs.append(p[:rows] + p[rows:])
    return outs[0] if len(outs) == 1 else jnp.concatenate(outs, axis=1)


def _tile_lanes(t, width):
    reps = width // t.shape[1]
    return t if reps == 1 else jnp.concatenate([t] * reps, axis=1)


def _rms(x, g):
    return x * lax.rsqrt(jnp.mean(x * x, axis=-1, keepdims=True) + NORM_EPS) * g


def _sigmoid(x):
    return 1.0 / (1.0 + jnp.exp(-x))


def _segment_specs(width, n_lat_tiles, ctx_tile):
    return (pl.BlockSpec((None, ROW_TILE, width), lambda b, t: (b, jnp.minimum(t, n_lat_tiles - 1), 0)),
            pl.BlockSpec((None, ROW_TILE, width), lambda b, t: (b, ctx_tile, 0)))


def _segment_rows(lat_ref, ctx_ref, n_lat_tiles):
    return jnp.where(pl.program_id(1) >= n_lat_tiles, ctx_ref[...], lat_ref[...])


def _mod_kernel(s_ref, w_ref, b_ref, o_ref):
    s = s_ref[...]
    s = s * _sigmoid(s)
    o_ref[...] = _dot(s, w_ref[...]) + b_ref[...]


def _modulation(cond, w_mod, b_mod):
    L, D, N = w_mod.shape
    tn = 512
    return pl.pallas_call(
        _mod_kernel,
        grid=(L, N // tn),
        in_specs=[pl.BlockSpec((MOD_ROWS, D), lambda l, j: (0, 0)),
                  pl.BlockSpec((None, D, tn), lambda l, j: (l, 0, j)),
                  pl.BlockSpec((None, 1, tn), lambda l, j: (l, 0, j))],
        out_specs=pl.BlockSpec((None, MOD_ROWS, tn), lambda l, j: (l, 0, j)),
        out_shape=jax.ShapeDtypeStruct((L, MOD_ROWS, N), F32),
        compiler_params=_cparams("parallel", "parallel"),
        name="modulation",
    )(cond, w_mod, b_mod.reshape(L, 1, N))


def _rope(x, cos, sin):
    w = x.shape[1]
    lane = lax.broadcasted_iota(jnp.int32, x.shape, 1)
    upper = (lane // ROPE_HALF) % 2 == 1
    partner = jnp.where(upper, pltpu.roll(x, ROPE_HALF, 1), pltpu.roll(x, w - ROPE_HALF, 1))
    return x * _tile_lanes(cos, w) + partner * _tile_lanes(sin, w)


def _inproj_kernel(xl_ref, xc_ref, g_ref, shift_ref, scale_ref, wqa_ref, wka_ref, wva_ref, wrw_ref, wn_ref,
                   qg_ref, kg_ref, cos_ref, sin_ref, bd_ref, vone_ref,
                   qa_ref, ka_ref, va_ref, rw_ref, qn_ref, kn_ref, vn_ref, *, n_lat_tiles):
    x = _segment_rows(xl_ref, xc_ref, n_lat_tiles)
    h = _rms(x, g_ref[...]) * (1.0 + scale_ref[...]) + shift_ref[...]
    hb = h.astype(BF16)
    bd = bd_ref[...]
    cos, sin = cos_ref[...], sin_ref[...]

    def normed(w_ref, gain_ref):
        y = _dot(hb, w_ref[...])
        ms = _head_sum(y * y, bd) * (1.0 / HEAD_DIM)
        return _rope(y * lax.rsqrt(ms + NORM_EPS) * gain_ref[...], cos, sin)

    def put_heads(o_ref, y, width=HEAD_DIM):
        for hd in range(y.shape[1] // width):
            o_ref[hd] = y[:, hd * width:(hd + 1) * width].astype(o_ref.dtype)

    put_heads(qa_ref, normed(wqa_ref, qg_ref) * ATTN_SCALE)
    put_heads(ka_ref, normed(wka_ref, kg_ref))
    vone = vone_ref[...]
    put_heads(va_ref, _dot(hb, wva_ref[...]) + vone[:, :A_KV_HEADS * V_EXT], V_EXT)
    rw_ref[...] = _dot(hb, wrw_ref[...])
    yn = _dot(hb, wn_ref[...])
    put_heads(qn_ref, yn[:, :C_WIDTH] * ATTN_SCALE)
    put_heads(kn_ref, yn[:, C_WIDTH:2 * C_WIDTH])
    put_heads(vn_ref, yn[:, 2 * C_WIDTH:] + vone, V_EXT)


def _extend_values(w, heads):
    D = w.shape[0]
    w = jnp.pad(w.reshape(D, heads, HEAD_DIM), ((0, 0), (0, 0), (0, V_EXT - HEAD_DIM)))
    return w.reshape(D, heads * V_EXT)


def _in_projection(x_lat, x_ctx, ctx_tile, mod4, g_pre, w_in, q_gain, k_gain, cos, sin, bd, n_lat_tiles):
    B, _, D = x_lat.shape
    tm = ROW_TILE
    T = (n_lat_tiles + 1) * tm
    o = np.cumsum([0, A_WIDTH, A_KV_WIDTH, A_KV_WIDTH, RW_WIDTH, 2 * C_WIDTH, C_WIDTH])
    wb = w_in.astype(BF16)
    wqa, wka, wva, wrw, wqk, wvn = [wb[:, o[i]:o[i + 1]] for i in range(6)]
    ws = [wqa, wka, _extend_values(wva, A_KV_HEADS), wrw,
          jnp.concatenate([wqk, _extend_values(wvn, C_HEADS)], axis=1)]
    vone = jnp.asarray(np.arange(C_HEADS * V_EXT) % V_EXT == HEAD_DIM, F32).reshape(1, C_HEADS * V_EXT)
    const = lambda shape: pl.BlockSpec(shape, lambda b, t: (0,) * len(shape))
    mod_spec = lambda k: pl.BlockSpec(
        (None, None, 1, D), lambda b, t: (jnp.where(t >= n_lat_tiles, B, b), k, 0, 0))
    heads = lambda nh, width=HEAD_DIM: pl.BlockSpec((None, nh, tm, width), lambda b, t: (b, 0, t, 0))
    hshape = lambda nh, width=HEAD_DIM: jax.ShapeDtypeStruct((B, nh, T, width), BF16)
    return pl.pallas_call(
        functools.partial(_inproj_kernel, n_lat_tiles=n_lat_tiles),
        grid=(B, T // tm),
        in_specs=[*_segment_specs(D, n_lat_tiles, ctx_tile), const((1, D)), mod_spec(0), mod_spec(1)]
                 + [const(w.shape) for w in ws]
                 + [const((1, A_WIDTH)), const((1, A_KV_WIDTH)),
                    pl.BlockSpec((tm, LANES), lambda b, t: (t, 0)),
                    pl.BlockSpec((tm, LANES), lambda b, t: (t, 0)),
                    const(bd.shape), const(vone.shape)],
        out_specs=[heads(A_HEADS), heads(A_KV_HEADS), heads(A_KV_HEADS, V_EXT),
                   pl.BlockSpec((None, tm, RW_WIDTH), lambda b, t: (b, t, 0)),
                   heads(C_HEADS), heads(C_HEADS), heads(C_HEADS, V_EXT)],
        out_shape=[hshape(A_HEADS), hshape(A_KV_HEADS), hshape(A_KV_HEADS, V_EXT),
                   jax.ShapeDtypeStruct((B, T, RW_WIDTH), F32),
                   hshape(C_HEADS), hshape(C_HEADS), hshape(C_HEADS, V_EXT)],
        compiler_params=_cparams("parallel", "parallel"),
        name="in_projection",
    )(x_lat, x_ctx, g_pre.reshape(1, D), mod4, mod4, *ws,
      jnp.tile(q_gain, A_HEADS).reshape(1, A_WIDTH), jnp.tile(k_gain, A_KV_HEADS).reshape(1, A_KV_WIDTH),
      cos, sin, bd, vone)


def _softmax_weights(s, m):
    return jnp.exp((s - m).astype(BF16))


def _normalised(ov):
    return ov[:, :HEAD_DIM] / ov[:, HEAD_DIM:HEAD_DIM + 1]


def _gqa_kernel(q_ref, k_ref, v_ref, o_ref, *, n_lat, n_lat_tiles):
    group = q_ref.shape[0]

    def attend(k, v):
        scores = [_dot_nt(q_ref[hd], k) for hd in range(group)]
        probs = [_softmax_weights(s, jnp.max(s, axis=-1, keepdims=True)) for s in scores]
        for hd, p in enumerate(probs):
            o_ref[:, hd * HEAD_DIM:(hd + 1) * HEAD_DIM] = _normalised(_dot(p, v)).astype(o_ref.dtype)

    @pl.when(pl.program_id(2) < n_lat_tiles)
    def _():
        attend(k_ref[...], v_ref[...])

    @pl.when(pl.program_id(2) >= n_lat_tiles)
    def _():
        attend(k_ref[n_lat:, :], v_ref[n_lat:, :])


def _gqa_attention(q, k, v, n_lat, n_rows):
    B, _, T, _ = q.shape
    tq = ROW_TILE
    group = A_HEADS // A_KV_HEADS
    kv_spec = lambda width: pl.BlockSpec((None, None, T, width), lambda b, g, t: (b, g, 0, 0))
    return pl.pallas_call(
        functools.partial(_gqa_kernel, n_lat=n_lat, n_lat_tiles=n_lat // tq),
        grid=(B, A_KV_HEADS, n_rows // tq),
        in_specs=[pl.BlockSpec((None, group, tq, HEAD_DIM), lambda b, g, t: (b, g, t, 0)),
                  kv_spec(HEAD_DIM), kv_spec(V_EXT)],
        out_specs=pl.BlockSpec((None, tq, group * HEAD_DIM), lambda b, g, t: (b, t, g)),
        out_shape=jax.ShapeDtypeStruct((B, n_rows, A_WIDTH), BF16),
        compiler_params=_cparams("parallel", "parallel", "parallel"),
        name="gqa_attention",
    )(q, k, v)


def _natten_kernel(q_ref, k_ref, v_ref, bias_ref, o_ref, *, n_lat, rows, need_ctx):
    win = WIN_ROWS * GRID_W
    for hd in range(C_HEADS):
        kc = k_ref[hd, n_lat:, :]
        vc = v_ref[hd, n_lat:, :]

        def row_group(g, carry, hd=hd, kc=kc, vc=vc):
            scores = []
            for u in range(NATTEN_ROWS_PER_ITER):
                r = g * NATTEN_ROWS_PER_ITER + u
                rs = jnp.clip(r - WIN_ROWS // 2, 0, rows - WIN_ROWS)
                q = q_ref[hd, pl.ds(pl.multiple_of(r * GRID_W, GRID_W), GRID_W), :]
                kw = k_ref[hd, pl.ds(pl.multiple_of(rs * GRID_W, GRID_W), win), :]
                scores.append((r, rs, _dot_nt(q, kw) + bias_ref[r - rs, hd], _dot_nt(q, kc)))
            probs = []
            for r, rs, s_w, s_c in scores:
                m = jnp.maximum(jnp.max(s_w, axis=-1, keepdims=True), jnp.max(s_c, axis=-1, keepdims=True))
                probs.append((r, rs, _softmax_weights(s_w, m), _softmax_weights(s_c, m)))
            for r, rs, p_w, p_c in probs:
                vw = v_ref[hd, pl.ds(pl.multiple_of(rs * GRID_W, GRID_W), win), :]
                o = _normalised(_dot(p_w, vw) + _dot(p_c, vc))
                o_ref[pl.ds(pl.multiple_of(r * GRID_W, GRID_W), GRID_W),
                      hd * HEAD_DIM:(hd + 1) * HEAD_DIM] = o.astype(o_ref.dtype)
            return carry

        lax.fori_loop(0, rows // NATTEN_ROWS_PER_ITER, row_group, 0)
        if need_ctx:
            s = _dot_nt(q_ref[hd, n_lat:, :], kc)
            o = _normalised(_dot(_softmax_weights(s, jnp.max(s, axis=-1, keepdims=True)), vc))
            o_ref[n_lat:, hd * HEAD_DIM:(hd + 1) * HEAD_DIM] = o.astype(o_ref.dtype)


def _natten_bias(rpb, rows):
    off = np.arange(WIN_ROWS)[:, None, None]
    jr = np.arange(WIN_ROWS)[None, :, None]
    row_sel = (np.arange(2 * WIN_ROWS - 1)[None, None, :] == jr - off + WIN_ROWS - 1)
    qc = np.arange(GRID_W)[:, None, None]
    kc = np.arange(GRID_W)[None, :, None]
    col_start = np.clip(qc - WIN_COLS // 2, 0, GRID_W - WIN_COLS)
    valid = (kc >= col_start) & (kc < col_start + WIN_COLS)
    col_sel = (np.arange(2 * WIN_COLS - 1)[None, None, :] == kc - qc + WIN_COLS - 1) & valid
    bias = jnp.einsum("hrc,ojr,qkc->ohqjk", rpb.astype(F32), jnp.asarray(row_sel, F32), jnp.asarray(col_sel, F32),
                      precision=lax.Precision.HIGHEST)
    bias = bias + jnp.asarray(np.where(valid[None, None, :, None, :, 0], 0.0, MASK_VALUE), F32)
    return bias.reshape(WIN_ROWS, C_HEADS, GRID_W, WIN_ROWS * GRID_W)


def _neighbourhood_attention(q, k, v, bias, n_lat, need_ctx):
    B, _, T, _ = q.shape
    n_rows = T if need_ctx else n_lat
    full = lambda width: pl.BlockSpec((None, C_HEADS, T, width), lambda b: (b, 0, 0, 0))
    return pl.pallas_call(
        functools.partial(_natten_kernel, n_lat=n_lat, rows=n_lat // GRID_W, need_ctx=need_ctx),
        grid=(B,),
        in_specs=[full(HEAD_DIM), full(HEAD_DIM), full(V_EXT), pl.BlockSpec(bias.shape, lambda b: (0, 0, 0, 0))],
        out_specs=pl.BlockSpec((None, n_rows, C_WIDTH), lambda b: (b, 0, 0)),
        out_shape=jax.ShapeDtypeStruct((B, n_rows, C_WIDTH), BF16),
        compiler_params=_cparams("parallel"),
        name="neighbourhood_attention",
    )(q, k, v, bias)


def _rwkv_chunk_kernel(rw_ref, w0_ref, a0_ref, lora_ref, kk_ref, ka_ref, rk_ref, bdw_ref,
                       rhat_ref, yhat_ref, gt_ref, ht_ref, gate_ref, bonus_ref):
    C, W, NH = CHUNK, B_WIDTH, B_HEADS
    S = NH * C
    R = rw_ref.shape[0]
    rw = rw_ref[...]
    r, k, v = rw[:, :W], rw[:, W:2 * W], rw[:, 2 * W:3 * W]
    wl = rw[:, 3 * W:3 * W + 2 * DECAY_LORA]
    al = rw[:, 3 * W + 2 * DECAY_LORA:3 * W + 2 * DECAY_LORA + 2 * AAA_LORA]
    gl = rw[:, 3 * W + 2 * DECAY_LORA + 2 * AAA_LORA:]
    bdw = bdw_ref[...]
    kkv = k * kk_ref[...]
    kkv = kkv / jnp.maximum(jnp.sqrt(_head_sum(kkv * kkv, bdw)), 1e-12)
    lora = _dot(jnp.concatenate([jnp.tanh(wl), al, _sigmoid(gl)], axis=1), lora_ref[...])
    gate_ref[...] = lora[:, 4 * W:]
    lws, asigs = [], []
    for d in range(2):
        w = w0_ref[d] + lora[:, d * W:(d + 1) * W]
        w = -(jnp.maximum(-w, 0.0) + jnp.log(1.0 + jnp.exp(-jnp.abs(w)))) - 0.5
        lws.append(-jnp.exp(w))
        asigs.append(_sigmoid(a0_ref[d] + lora[:, (2 + d) * W:(3 + d) * W]))

    prow = lax.broadcasted_iota(jnp.int32, (R, R), 0)
    pcol = lax.broadcasted_iota(jnp.int32, (R, R), 1)
    tri = ((prow >= pcol) & (prow // C == pcol // C)).astype(BF16)
    cs = _dot(tri, jnp.concatenate([p for lw in lws for p in _split3(lw)], axis=1))
    prefix = [cs[:, 3 * d * W:(3 * d + 1) * W] + cs[:, (3 * d + 1) * W:(3 * d + 2) * W]
              + cs[:, (3 * d + 2) * W:(3 * d + 3) * W] for d in range(2)]

    srow = lax.broadcasted_iota(jnp.int32, (S, W), 0)
    scol = lax.broadcasted_iota(jnp.int32, (S, W), 1)
    same = (srow // C) == (scol // HEAD_DIM)
    crow = lax.broadcasted_iota(jnp.int32, (C, W), 0)
    cpos = lax.broadcasted_iota(jnp.int32, (C, W), 1) % HEAD_DIM
    eye_c = crow == cpos
    eye_b = eye_c.astype(BF16)

    def stack(x):
        return jnp.where(same, jnp.concatenate([x] * NH, axis=0), 0.0).astype(BF16)

    kds = [k * (1.0 + (asig - 1.0) * ka_ref[...]) for asig in asigs]
    bonus_ref[...] = _head_sum(r * (kds[0] + kds[1]) * rk_ref[...], bdw) * v

    class Chain:
        pass

    chains = []
    for ci in range(R // C):
        v_s = stack(v[ci * C:(ci + 1) * C])
        for d in range(2):
            ch = Chain()
            ch.d, ch.ci, ch.rows, ch.v_s = d, ci, slice(ci * C, (ci + 1) * C), v_s
            chains.append(ch)

    for ch in chains:
        d, rows = ch.d, ch.rows
        lw, kd, b, a = lws[d][rows], kds[d][rows], (kkv * asigs[d])[rows], -kkv[rows]
        pre = prefix[d][rows]
        ch.tot = tot = pre[C - 1:C, :]
        cum_incl = pre if d == 0 else tot - pre + lw
        cum_excl = cum_incl - lw
        rho = 0.5 * tot
        e_in = jnp.exp(rho - cum_incl)
        e_end = jnp.exp(tot - cum_incl)
        ch.e_rho = jnp.exp(rho)
        ch.a_rho = a * jnp.exp(cum_excl - rho)
        ch.r_rho = r[rows] * jnp.exp(cum_incl - rho)
        ch.ends = jnp.concatenate([stack(b * e_end), stack(kd * e_end)], axis=0)
        ch.m = _dot_nt(jnp.concatenate([ch.a_rho, ch.r_rho], axis=0).astype(BF16),
                       jnp.concatenate([stack(b * e_in), stack(kd * e_in)], axis=0))

    for ch in chains:
        before, upto = (crow > cpos, crow >= cpos) if ch.d == 0 else (crow < cpos, crow <= cpos)
        m = ch.m
        l_ab = jnp.where(before, m[:C, :W], 0.0)
        ch.a_rb = jnp.where(upto, m[C:, :W], 0.0).astype(BF16)
        ch.kv = jnp.concatenate([jnp.where(before, m[:C, W:], 0.0), jnp.where(upto, m[C:, W:], 0.0)],
                                axis=0).astype(BF16)
        ch.lp = _dot(l_ab.astype(BF16), stack(l_ab))
        ch.tinv = jnp.where(eye_c, 1.0, l_ab)
    for _ in range(int(np.log2(C)) - 2):
        for ch in chains:
            p = _dot(jnp.concatenate([ch.tinv, ch.lp], axis=0).astype(BF16), stack(ch.lp))
            ch.tinv = ch.tinv + p[:C]
            ch.lp = p[C:]
    for ch in chains:
        ch.tinv = ch.tinv + _dot(ch.tinv.astype(BF16), stack(ch.lp))
    for ch in chains:
        ch.wv = _dot(ch.kv, ch.v_s)
    for ch in chains:
        ch.z = _dot(ch.tinv.astype(BF16),
                    jnp.concatenate([stack(ch.a_rho), stack(ch.wv[:C])], axis=1))
    for ch in chains:
        ch.ahat = ch.z[:, :W] * ch.e_rho
        ch.av = jnp.concatenate([stack(ch.z[:, :W]), stack(ch.z[:, W:])], axis=1)
        rz = _dot(ch.a_rb, ch.av)
        rhat_ref[ch.d, ch.rows] = (ch.r_rho + rz[:, :W]) * ch.e_rho
        yhat_ref[ch.d, ch.rows] = rz[:, W:] + ch.wv[C:]
    for ch in chains:
        ch.ends_t = _dot_nt(eye_b, ch.ends).astype(BF16)
    for ch in chains:
        gh = _dot(ch.ends_t[:, :S], jnp.concatenate([stack(ch.ahat), ch.av[:, W:]], axis=1))
        gt_ref[ch.d, ch.ci] = gh[:, :W] + jnp.where(eye_c, jnp.exp(ch.tot), 0.0)
        ht_ref[ch.d, ch.ci] = gh[:, W:] + _dot(ch.ends_t[:, S:], ch.v_s)


def _rwkv_chunks(rw, p, bdw):
    B, T, _ = rw.shape
    C, W = CHUNK, B_WIDTH
    nc = T // C
    per = RWKV_CHUNKS_PER_STEP
    assert C == HEAD_DIM and nc % per == 0
    const = lambda shape: pl.BlockSpec(shape, lambda b, c: (0,) * len(shape))
    rows = pl.BlockSpec((None, 2, per * C, W), lambda b, c: (b, 0, c, 0))
    mats = pl.BlockSpec((None, 2, per, C, W), lambda b, c: (b, 0, c, 0, 0))
    flat = pl.BlockSpec((None, per * C, W), lambda b, c: (b, c, 0))
    vec = lambda a: a.reshape(1, W)
    lora_w = jnp.zeros((2 * DECAY_LORA + 2 * AAA_LORA + GATE_LORA, 5 * W), F32)
    for d in range(2):
        lora_w = lora_w.at[d * DECAY_LORA:(d + 1) * DECAY_LORA, d * W:(d + 1) * W].set(p["w_up"][d])
        lora_w = lora_w.at[2 * DECAY_LORA + d * AAA_LORA:2 * DECAY_LORA + (d + 1) * AAA_LORA,
                           (2 + d) * W:(3 + d) * W].set(p["a_up"][d])
    lora_w = lora_w.at[2 * DECAY_LORA + 2 * AAA_LORA:, 4 * W:].set(p["g_up"])
    return pl.pallas_call(
        _rwkv_chunk_kernel,
        grid=(B, nc // per),
        in_specs=[pl.BlockSpec((None, per * C, RW_WIDTH), lambda b, c: (b, c, 0)),
                  const((2, 1, W)), const((2, 1, W)), const(lora_w.shape),
                  const((1, W)), const((1, W)), const((1, W)), const(bdw.shape)],
        out_specs=[rows, rows, mats, mats, flat, flat],
        out_shape=[jax.ShapeDtypeStruct((B, 2, T, W), F32), jax.ShapeDtypeStruct((B, 2, T, W), F32),
                   jax.ShapeDtypeStruct((B, 2, nc, C, W), F32), jax.ShapeDtypeStruct((B, 2, nc, C, W), F32),
                   jax.ShapeDtypeStruct((B, T, W), F32), jax.ShapeDtypeStruct((B, T, W), F32)],
        compiler_params=_cparams("parallel", "parallel"),
        name="rwkv_chunks",
    )(rw, p["w0"].reshape(2, 1, W), p["a0"].reshape(2, 1, W), lora_w,
      vec(p["k_k"]), vec(p["k_a"]), vec(p["r_k"]), bdw)


def _rwkv_scan_kernel(rhat0_ref, yhat0_ref, gt0_ref, ht0_ref, rhat1_ref, yhat1_ref, gt1_ref, ht1_ref,
                      y0_ref, y1_ref, st_ref):
    @pl.when(pl.program_id(0) == 0)
    def _():
        st_ref[...] = jnp.zeros_like(st_ref)

    W = B_WIDTH
    same = (lax.broadcasted_iota(jnp.int32, (W, W), 0) // HEAD_DIM
            == lax.broadcasted_iota(jnp.int32, (W, W), 1) // HEAD_DIM)
    expand = lambda x: jnp.where(same, jnp.concatenate([x] * B_HEADS, axis=0), 0.0)
    dirs = ((rhat0_ref, yhat0_ref, gt0_ref, ht0_ref, y0_ref), (rhat1_ref, yhat1_ref, gt1_ref, ht1_ref, y1_ref))
    for b in range(st_ref.shape[1]):
        for d, (rhat_ref, yhat_ref, gt_ref, ht_ref, y_ref) in enumerate(dirs):
            stb = st_ref[d, b].astype(BF16)
            y_ref[b] = _dot(rhat_ref[b].astype(BF16), stb) + yhat_ref[b]
            st_ref[d, b] = _dot(expand(gt_ref[b]).astype(BF16), stb) + expand(ht_ref[b])


def _rwkv_scan(rhat, yhat, gt, ht, n_lat_chunks):
    B, _, T, W = rhat.shape
    C = CHUNK
    nc = T // C
    chunk = (lambda s: (s + n_lat_chunks) % nc, lambda s: nc - 1 - s)
    rows = lambda d: pl.BlockSpec((B, None, C, W), lambda s: (0, d, chunk[d](s), 0))
    mats = lambda d: pl.BlockSpec((B, None, None, C, W), lambda s: (0, d, chunk[d](s), 0, 0))
    outs = lambda d: pl.BlockSpec((B, C, W), lambda s: (0, chunk[d](s), 0))
    return pl.pallas_call(
        _rwkv_scan_kernel,
        grid=(nc,),
        in_specs=[rows(0), rows(0), mats(0), mats(0), rows(1), rows(1), mats(1), mats(1)],
        out_specs=[outs(0), outs(1)],
        out_shape=[jax.ShapeDtypeStruct((B, T, W), F32)] * 2,
        scratch_shapes=[pltpu.VMEM((2, B, W, W), F32)],
        compiler_params=_cparams("arbitrary"),
        name="rwkv_scan",
    )(rhat, yhat, gt, ht, rhat, yhat, gt, ht)


def _outproj_kernel(xl_ref, xc_ref, oa_ref, y0_ref, y1_ref, bonus_ref, gate_ref, on_ref, lnw_ref, lnb_ref, bd_ref,
                    wa_ref, wb_ref, wn_ref, g_ref, mgate_ref, o_ref, *, n_lat_tiles):
    y = y0_ref[...] + y1_ref[...]
    bd = bd_ref[...]
    mu = _head_sum(y, bd) * (1.0 / HEAD_DIM)
    yc = y - mu
    var = _head_sum(yc * yc, bd) * (1.0 / HEAD_DIM)
    ob = (yc * lax.rsqrt(var + LNX_EPS) * lnw_ref[...] + lnb_ref[...] + bonus_ref[...]) * gate_ref[...]
    out = _dot(oa_ref[...], wa_ref[...]) + _dot(ob.astype(BF16), wb_ref[...]) + _dot(on_ref[...], wn_ref[...])
    o_ref[...] = _segment_rows(xl_ref, xc_ref, n_lat_tiles) + mgate_ref[...] * _rms(out, g_ref[...])


def _out_projection(x_lat, x_ctx, ctx_tile, oa, y0, y1, bonus, gate, on, p, bd, w_out, g_post, mod4,
                    n_lat_tiles, n_rows):
    B, _, D = x_lat.shape
    tm = ROW_TILE
    W = B_WIDTH
    wb = w_out.astype(BF16)
    const = lambda shape: pl.BlockSpec(shape, lambda b, t: (0,) * len(shape))
    rows = lambda width: pl.BlockSpec((None, tm, width), lambda b, t: (b, t, 0))
    return pl.pallas_call(
        functools.partial(_outproj_kernel, n_lat_tiles=n_lat_tiles),
        grid=(B, n_rows // tm),
        in_specs=[*_segment_specs(D, n_lat_tiles, ctx_tile),
                  rows(A_WIDTH), rows(W), rows(W), rows(W), rows(W), rows(C_WIDTH),
                  const((1, W)), const((1, W)), const(bd.shape),
                  const((A_WIDTH, D)), const((W, D)), const((C_WIDTH, D)), const((1, D)),
                  pl.BlockSpec((None, None, 1, D), lambda b, t: (jnp.where(t >= n_lat_tiles, B, b), 2, 0, 0))],
        out_specs=rows(D),
        out_shape=jax.ShapeDtypeStruct((B, n_rows, D), F32),
        compiler_params=_cparams("parallel", "parallel"),
        name="out_projection",
    )(x_lat, x_ctx, oa, y0, y1, bonus, gate, on, p["lnx_w"].reshape(1, W), p["lnx_b"].reshape(1, W), bd,
      wb[:A_WIDTH], wb[A_WIDTH:A_WIDTH + W], wb[A_WIDTH + W:], g_post.reshape(1, D), mod4)


def _router_kernel(x_ref, g_ref, shift_ref, scale_ref, wr_ref, h_ref, aff_ref):
    h = _rms(x_ref[...], g_ref[...]) * (1.0 + scale_ref[...]) + shift_ref[...]
    h_ref[...] = h.astype(h_ref.dtype)
    logits = _dot_nt(wr_ref[...], h)
    p = jnp.exp(logits - jnp.max(logits, axis=0, keepdims=True))
    aff_ref[...] = p / jnp.sum(p, axis=0, keepdims=True)


def _router(xs, g_pre, mod4, w_router, n_lat_tiles):
    B, R, D = xs.shape
    tm = ROW_TILE
    E = w_router.shape[1]
    mod_spec = lambda k: pl.BlockSpec(
        (None, None, 1, D), lambda b, t: (jnp.where(t >= n_lat_tiles, B, b), k, 0, 0))
    return pl.pallas_call(
        _router_kernel,
        grid=(B, R // tm),
        in_specs=[pl.BlockSpec((None, tm, D), lambda b, t: (b, t, 0)),
                  pl.BlockSpec((1, D), lambda b, t: (0, 0)), mod_spec(3), mod_spec(4),
                  pl.BlockSpec((E, D), lambda b, t: (0, 0))],
        out_specs=[pl.BlockSpec((None, tm, D), lambda b, t: (b, t, 0)),
                   pl.BlockSpec((None, E, tm), lambda b, t: (b, 0, t))],
        out_shape=[jax.ShapeDtypeStruct((B, R, D), BF16), jax.ShapeDtypeStruct((B, E, R), F32)],
        compiler_params=_cparams("parallel", "parallel"),
        name="moe_router",
    )(xs, g_pre.reshape(1, D), mod4, mod4, w_router.T)


def _column(x, e):
    lane = lax.broadcasted_iota(jnp.int32, x.shape, 1)
    return jnp.sum(jnp.where(lane == e, x, 0.0), axis=1, keepdims=True)


RANK_BLOCK = 256


def _rank_partials(a_row, a_col):
    n = a_row.shape[-1]
    tj = min(n, RANK_BLOCK)
    as_count = lambda mask: mask.astype(F32).astype(BF16)

    def partial(jc):
        lo, hi = jc * tj, (jc + 1) * tj
        aj = a_col[lo:hi, :]
        ai = a_row[:, lo:hi]
        j_first = (lax.broadcasted_iota(jnp.int32, (tj, tj), 0) < lax.broadcasted_iota(jnp.int32, (tj, tj), 1))
        parts = [as_count(aj > a_row[:, :lo])] if lo else []
        parts.append(as_count((aj > ai) | ((aj == ai) & j_first)))
        if hi < n:
            parts.append(as_count(aj >= a_row[:, hi:]))
        return _dot(jnp.ones((8, tj), BF16), parts[0] if len(parts) == 1 else jnp.concatenate(parts, axis=1))

    return [functools.partial(partial, jc) for jc in range(n // tj)]


def _token_ranks(a_row, a_col):
    cnt = None
    for thunk in _rank_partials(a_row, a_col):
        cnt = thunk() if cnt is None else cnt + thunk()
    return cnt[0:1, :]


def _propose_threshold(aff, cap):
    keys = lax.bitcast_convert_type(aff, jnp.int32)
    t = jnp.zeros((aff.shape[0], 1), jnp.int32)
    for bit in range(30, -1, -1):
        cand = t | (1 << bit)
        cnt = jnp.sum((keys >= cand).astype(F32), axis=1, keepdims=True)
        t = jnp.where(cnt >= cap, cand, t)
    return lax.bitcast_convert_type(t, F32)


def _prefix_count(mask):
    n = mask.shape[1]
    blk = min(n, RANK_BLOCK)
    tri = (lax.broadcasted_iota(jnp.int32, (blk, blk), 0) <= lax.broadcasted_iota(jnp.int32, (blk, blk), 1))
    tri = tri.astype(F32).astype(BF16)
    ones = mask.astype(F32).astype(BF16)
    outs = []
    for c in range(n // blk):
        p = _dot(ones[:, c * blk:(c + 1) * blk], tri)
        outs.append(p + outs[-1][:, blk - 1:blk] if outs else p)
    return outs[0] if len(outs) == 1 else jnp.concatenate(outs, axis=1)


def _slots_from_threshold(aff, tau, cap):
    above, equal = aff > tau, aff == tau
    n_above = jnp.sum(above.astype(F32), axis=1, keepdims=True)
    n_equal = jnp.sum(equal.astype(F32), axis=1, keepdims=True)
    ok = (n_above < cap) & (n_above + n_equal >= cap)
    take = above | (equal & (_prefix_count(equal) <= cap - n_above))
    return jnp.where(take, _prefix_count(take) - 1.0, -1.0), ok


def _expert_kernel(h_ref, aff_ref, affc_ref, wg_ref, wu_ref, wd_ref, f_ref, yet_ref, ye_ref, ft_ref, rank_ref,
                   *, cap, groups):
    n = h_ref.shape[0]
    E = aff_ref.shape[0]
    e = pl.program_id(1)

    @pl.when(e == 0)
    def _():
        seg, gcap = n // groups, cap // groups
        bad = 0.0
        for g in range(groups):
            cols = slice(g * seg, (g + 1) * seg)
            aff = aff_ref[:, cols]
            slots, ok = _slots_from_threshold(aff, _propose_threshold(aff, gcap), gcap)
            rank_ref[:, cols] = jnp.where(slots >= 0.0, slots + g * gcap, -1.0)
            bad = bad + jnp.sum(jnp.where(ok, 0.0, 1.0))

        @pl.when(bad > 0.0)
        def _():
            for g in range(groups):
                cols = slice(g * seg, (g + 1) * seg)

                def rerank(x, carry, g=g, cols=cols):
                    r = _token_ranks(aff_ref[pl.ds(x, 1), cols], _column(affc_ref[cols, :], x))
                    rank_ref[pl.ds(x, 1), cols] = jnp.where(r < gcap, r + g * gcap, -1.0)
                    return carry
                lax.fori_loop(0, E, rerank, 0)

    slot = lax.broadcasted_iota(jnp.int32, (cap, n), 0).astype(F32)
    pick = (rank_ref[pl.ds(e, 1), :] == slot).astype(F32).astype(BF16)
    xe = _dot(pick, h_ref[...]).astype(BF16)
    gte = _dot(xe, wg_ref[...])
    hid = (gte * _sigmoid(gte) * _dot(xe, wu_ref[...])).astype(BF16)
    ye_ref[...] = _dot(hid, wd_ref[...])
    yet_ref[:, pl.ds(pl.multiple_of(e * cap, cap), cap)] = ye_ref[...].T.astype(BF16)

    @pl.when(e == E - 1)
    def _():
        tn = min(n, ROW_TILE)
        slot_c = lax.broadcasted_iota(jnp.int32, (cap, tn), 0).astype(F32)
        for c in range(n // tn):
            cols = slice(c * tn, (c + 1) * tn)
            put = jnp.concatenate(
                [jnp.where(rank_ref[x:x + 1, cols] == slot_c, aff_ref[x:x + 1, cols], 0.0).astype(BF16)
                 for x in range(E)], axis=0)
            ft_ref[...] = _dot(yet_ref[...], put)
            f_ref[cols, :] = ft_ref[...].T


def _experts(h, aff, weights, layer, cap, groups):
    S, E, n = aff.shape
    D = h.shape[-1]
    F = weights[0].shape[-1]
    assert cap % LANES == 0 and (n // groups) % LANES == 0
    weight = lambda a, b: pl.BlockSpec((None, None, a, b), lambda s, e: (layer, e, 0, 0))
    return pl.pallas_call(
        functools.partial(_expert_kernel, cap=cap, groups=groups),
        grid=(S, E),
        in_specs=[pl.BlockSpec((None, n, D), lambda s, e: (s, 0, 0)),
                  pl.BlockSpec((None, E, n), lambda s, e: (s, 0, 0)),
                  pl.BlockSpec((None, n, E), lambda s, e: (s, 0, 0)),
                  weight(D, F), weight(D, F), weight(F, D)],
        out_specs=pl.BlockSpec((None, n, D), lambda s, e: (s, 0, 0)),
        out_shape=jax.ShapeDtypeStruct((S, n, D), F32),
        scratch_shapes=[pltpu.VMEM((D, E * cap), BF16), pltpu.VMEM((cap, D), F32),
                        pltpu.VMEM((D, min(n, ROW_TILE)), F32), pltpu.VMEM((E, n), F32)],
        compiler_params=_cparams("parallel", "arbitrary"),
        name="moe_experts",
    )(h, aff, jnp.swapaxes(aff, 1, 2), *weights)


def _ffn_residual_kernel(x_ref, fl_ref, fc_ref, g_ref, mgate_ref, o_ref, *, n_lat_tiles):
    f = _segment_rows(fl_ref, fc_ref, n_lat_tiles)
    o_ref[...] = x_ref[...] + mgate_ref[...] * _rms(f, g_ref[...])


def _ffn_residual(xs, f_lat, f_ctx, g_post, mod4, n_lat_tiles):
    B, R, D = xs.shape
    tm = ROW_TILE
    rows = pl.BlockSpec((None, tm, D), lambda b, t: (b, t, 0))
    return pl.pallas_call(
        functools.partial(_ffn_residual_kernel, n_lat_tiles=n_lat_tiles),
        grid=(B, R // tm),
        in_specs=[rows, *_segment_specs(D, n_lat_tiles, 0), pl.BlockSpec((1, D), lambda b, t: (0, 0)),
                  pl.BlockSpec((None, None, 1, D), lambda b, t: (jnp.where(t >= n_lat_tiles, B, b), 5, 0, 0))],
        out_specs=rows,
        out_shape=jax.ShapeDtypeStruct((B, R, D), F32),
        compiler_params=_cparams("parallel", "parallel"),
        name="ffn_residual",
    )(xs, f_lat, f_lat if f_ctx is None else f_ctx, g_post.reshape(1, D), mod4)


def _moe(h, aff_t, weights, layer, n_lat, n_ctx):
    B, _, D = h.shape
    E = aff_t.shape[1]
    f_l = _experts(h, aff_t[:, :, :n_lat], weights, layer, CAPACITY_FACTOR * n_lat // E, 1)
    if not n_ctx:
        return f_l, None
    a_c = jnp.swapaxes(aff_t[:, :, n_lat:], 0, 1).reshape(1, E, B * n_ctx)
    f_c = _experts(h[:, n_lat:].reshape(1, B * n_ctx, D), a_c, weights, layer,
                   B * (CAPACITY_FACTOR * n_ctx // E), B)
    return f_l, f_c.reshape(B, n_ctx, D)


def _rope_tables(n_lat, n_ctx):
    inv = ROPE_BASE ** (-jnp.arange(ROPE_HALF, dtype=F32) / ROPE_HALF)
    pos = jnp.arange(n_lat)
    ang_r = (pos // GRID_W).astype(F32)[:, None] * inv[None, :]
    ang_c = (pos % GRID_W).astype(F32)[:, None] * inv[None, :]
    cos = jnp.concatenate([jnp.cos(ang_r)] * 2 + [jnp.cos(ang_c)] * 2, axis=1)
    sin = jnp.concatenate([-jnp.sin(ang_r), jnp.sin(ang_r), -jnp.sin(ang_c), jnp.sin(ang_c)], axis=1)
    cos = jnp.concatenate([jnp.tile(cos, (1, LANES // HEAD_DIM)), jnp.ones((n_ctx, LANES), F32)], axis=0)
    sin = jnp.concatenate([jnp.tile(sin, (1, LANES // HEAD_DIM)), jnp.zeros((n_ctx, LANES), F32)], axis=0)
    return cos, sin


def kernel(x, c, ctx, c_ctx, w_mod, b_mod, g_pre_mix, g_post_mix, g_pre_ffn, g_post_ffn, w_in, w_out, q_gain, k_gain, rpb, rk_w0, rk_w_up, rk_a0, rk_a_up, rk_g_up, rk_k_k, rk_k_a, rk_r_k, rk_lnx_w, rk_lnx_b, w_router, w_e_gate, w_e_up, w_e_down):
    B, n_lat, D = x.shape
    n_ctx = ctx.shape[1]
    T = n_lat + n_ctx
    depth = w_mod.shape[0]
    assert n_lat % ROW_TILE == 0 and n_ctx == ROW_TILE and n_lat % n_ctx == 0 and B < MOD_ROWS
    n_lat_tiles = n_lat // ROW_TILE

    cond = jnp.zeros((MOD_ROWS, D), F32).at[:B].set(c).at[B].set(c_ctx)
    mod = _modulation(cond, w_mod, b_mod)
    cos, sin = _rope_tables(n_lat, n_ctx)
    lane = np.arange(HEAD_SUM_TILE)
    bd = jnp.asarray(lane[:, None] // HEAD_DIM == lane[None, :] // HEAD_DIM, BF16)

    weights = (w_e_gate.astype(BF16), w_e_up.astype(BF16), w_e_down.astype(BF16))
    x_lat, x_ctx, ctx_tile = x, ctx, 0
    for i in range(depth):
        need_ctx = i < depth - 1
        n_rows = T if need_ctx else n_lat
        mod4 = mod[i].reshape(MOD_ROWS, 6, 1, D)
        qa, ka, va, rw, qn, kn, vn = _in_projection(
            x_lat, x_ctx, ctx_tile, mod4, g_pre_mix[i], w_in[i], q_gain[i], k_gain[i], cos, sin, bd, n_lat_tiles)
        oa = _gqa_attention(qa, ka, va, n_lat, n_rows)
        on = _neighbourhood_attention(qn, kn, vn, _natten_bias(rpb[i], n_lat // GRID_W), n_lat, need_ctx)
        rk = dict(w0=rk_w0[i], w_up=rk_w_up[i], a0=rk_a0[i], a_up=rk_a_up[i], g_up=rk_g_up[i],
                  k_k=rk_k_k[i], k_a=rk_k_a[i], r_k=rk_r_k[i], lnx_w=rk_lnx_w[i], lnx_b=rk_lnx_b[i])
        rhat, yhat, gt, ht, gate, bonus = _rwkv_chunks(rw, rk, bd)
        y0, y1 = _rwkv_scan(rhat, yhat, gt, ht, n_lat // CHUNK)
        xs = _out_projection(x_lat, x_ctx, ctx_tile, oa, y0, y1, bonus, gate, on, rk, bd, w_out[i], g_post_mix[i], mod4,
                             n_lat_tiles, n_rows)
        h, aff_t = _router(xs, g_pre_ffn[i], mod4, w_router[i], n_lat_tiles)
        f_lat, f_ctx = _moe(h, aff_t, weights, i, n_lat, n_ctx if need_ctx else 0)
        x_lat = x_ctx = _ffn_residual(xs, f_lat, f_ctx, g_post_ffn[i], mod4, n_lat_tiles)
        ctx_tile = n_lat_tiles
    return x_lat
```

```python
import functools

import numpy as np
import jax
import jax.numpy as jnp
from jax import lax
from jax.experimental import pallas as pl
from jax.experimental.pallas import tpu as pltpu

F32 = jnp.float32
BF16 = jnp.bfloat16

HEAD_DIM = 64
GRID_W = 64
A_HEADS = 8
A_KV_HEADS = 2
B_HEADS = 4
C_HEADS = 4
A_WIDTH = A_HEADS * HEAD_DIM
A_KV_WIDTH = A_KV_HEADS * HEAD_DIM
B_WIDTH = B_HEADS * HEAD_DIM
C_WIDTH = C_HEADS * HEAD_DIM
DECAY_LORA = 64
AAA_LORA = 64
GATE_LORA = 128
RW_WIDTH = 3 * B_WIDTH + 2 * DECAY_LORA + 2 * AAA_LORA + GATE_LORA
ROPE_BASE = 10000.0
ROPE_HALF = HEAD_DIM // 4
WIN_ROWS = 8
WIN_COLS = 16
N_EXPERTS = 16
CAPACITY_FACTOR = 2
NORM_EPS = 1e-6
LNX_EPS = 64e-5
ATTN_SCALE = HEAD_DIM ** -0.5
V_EXT = 2 * HEAD_DIM
MASK_VALUE = -1e30

LANES = 128
HEAD_SUM_TILE = 256
ROW_TILE = 256
CHUNK = 64
RWKV_CHUNKS_PER_STEP = 4
NATTEN_ROWS_PER_ITER = 8
MOD_ROWS = 16
VMEM_LIMIT = 56 * 1024 * 1024


def _cparams(*sem):
    return pltpu.CompilerParams(dimension_semantics=sem, vmem_limit_bytes=VMEM_LIMIT)


def _dot(a, b):
    return jnp.dot(a, b, preferred_element_type=F32)


def _dot_nt(a, b):
    return lax.dot_general(a, b, (((1,), (1,)), ((), ())), preferred_element_type=F32)


def _dot_tn(a, b):
    return lax.dot_general(a, b, (((0,), (0,)), ((), ())), preferred_element_type=F32)


def _split2(x):
    hi = x.astype(BF16)
    lo = (x - hi.astype(F32)).astype(BF16)
    return hi, lo


def _split3(x):
    hi = x.astype(BF16)
    r1 = x - hi.astype(F32)
    mid = r1.astype(BF16)
    lo = (r1 - mid.astype(F32)).astype(BF16)
    return hi, mid, lo


def _head_sum(x, bd):
    rows, width = x.shape
    tile = bd.shape[0]
    outs = []
    for c in range(0, width, tile):
        w = min(tile, width - c)
        hi, lo = _split2(x[:, c:c + w])
        p = _dot(jnp.concatenate([hi, lo], axis=0), bd[:w, :w])
        outs.append(p[:rows] + p[rows:])
    return outs[0] if len(outs) == 1 else jnp.concatenate(outs, axis=1)


def _tile_lanes(t, width):
    reps = width // t.shape[1]
    return t if reps == 1 else jnp.concatenate([t] * reps, axis=1)


def _rms(x, g):
    return x * lax.rsqrt(jnp.mean(x * x, axis=-1, keepdims=True) + NORM_EPS) * g


def _sigmoid(x):
    return 1.0 / (1.0 + jnp.exp(-x))


def _segment_specs(width, n_lat_tiles, ctx_tile):
    return (pl.BlockSpec((None, ROW_TILE, width), lambda b, t: (b, jnp.minimum(t, n_lat_tiles - 1), 0)),
            pl.BlockSpec((None, ROW_TILE, width), lambda b, t: (b, ctx_tile, 0)))


def _segment_rows(lat_ref, ctx_ref, n_lat_tiles):
    return jnp.where(pl.program_id(1) >= n_lat_tiles, ctx_ref[...], lat_ref[...])


def _mod_kernel(s_ref, w_ref, b_ref, o_ref):
    s = s_ref[...]
    s = s * _sigmoid(s)
    o_ref[...] = _dot(s, w_ref[...]) + b_ref[...]


def _modulation(cond, w_mod, b_mod):
    L, D, N = w_mod.shape
    tn = 512
    return pl.pallas_call(
        _mod_kernel,
        grid=(L, N // tn),
        in_specs=[pl.BlockSpec((MOD_ROWS, D), lambda l, j: (0, 0)),
                  pl.BlockSpec((None, D, tn), lambda l, j: (l, 0, j)),
                  pl.BlockSpec((None, 1, tn), lambda l, j: (l, 0, j))],
        out_specs=pl.BlockSpec((None, MOD_ROWS, tn), lambda l, j: (l, 0, j)),
        out_shape=jax.ShapeDtypeStruct((L, MOD_ROWS, N), F32),
        compiler_params=_cparams("parallel", "parallel"),
        name="modulation",
    )(cond, w_mod, b_mod.reshape(L, 1, N))


def _rope(x, cos, sin):
    w = x.shape[1]
    lane = lax.broadcasted_iota(jnp.int32, x.shape, 1)
    upper = (lane // ROPE_HALF) % 2 == 1
    partner = jnp.where(upper, pltpu.roll(x, ROPE_HALF, 1), pltpu.roll(x, w - ROPE_HALF, 1))
    return x * _tile_lanes(cos, w) + partner * _tile_lanes(sin, w)


def _inproj_kernel(xl_ref, xc_ref, g_ref, shift_ref, scale_ref, wqa_ref, wka_ref, wva_ref, wrw_ref, wn_ref,
                   qg_ref, kg_ref, cos_ref, sin_ref, bd_ref, vone_ref,
                   qa_ref, ka_ref, va_ref, rw_ref, qn_ref, kn_ref, vn_ref, *, n_lat_tiles):
    x = _segment_rows(xl_ref, xc_ref, n_lat_tiles)
    h = _rms(x, g_ref[...]) * (1.0 + scale_ref[...]) + shift_ref[...]
    hb = h.astype(BF16)
    bd = bd_ref[...]
    cos, sin = cos_ref[...], sin_ref[...]

    def normed(w_ref, gain_ref):
        y = _dot(hb, w_ref[...])
        ms = _head_sum(y * y, bd) * (1.0 / HEAD_DIM)
        return _rope(y * lax.rsqrt(ms + NORM_EPS) * gain_ref[...], cos, sin)

    def put_heads(o_ref, y, width=HEAD_DIM):
        for hd in range(y.shape[1] // width):
            o_ref[hd] = y[:, hd * width:(hd + 1) * width].astype(o_ref.dtype)

    put_heads(qa_ref, normed(wqa_ref, qg_ref) * ATTN_SCALE)
    put_heads(ka_ref, normed(wka_ref, kg_ref))
    vone = vone_ref[...]
    put_heads(va_ref, _dot(hb, wva_ref[...]) + vone[:, :A_KV_HEADS * V_EXT], V_EXT)
    rw_ref[...] = _dot(hb, wrw_ref[...])
    yn = _dot(hb, wn_ref[...])
    put_heads(qn_ref, yn[:, :C_WIDTH] * ATTN_SCALE)
    put_heads(kn_ref, yn[:, C_WIDTH:2 * C_WIDTH])
    put_heads(vn_ref, yn[:, 2 * C_WIDTH:] + vone, V_EXT)


def _extend_values(w, heads):
    D = w.shape[0]
    w = jnp.pad(w.reshape(D, heads, HEAD_DIM), ((0, 0), (0, 0), (0, V_EXT - HEAD_DIM)))
    return w.reshape(D, heads * V_EXT)


def _in_projection(x_lat, x_ctx, ctx_tile, mod4, g_pre, w_in, q_gain, k_gain, cos, sin, bd, n_lat_tiles):
    B, _, D = x_lat.shape
    tm = ROW_TILE
    T = (n_lat_tiles + 1) * tm
    o = np.cumsum([0, A_WIDTH, A_KV_WIDTH, A_KV_WIDTH, RW_WIDTH, 2 * C_WIDTH, C_WIDTH])
    wb = w_in.astype(BF16)
    wqa, wka, wva, wrw, wqk, wvn = [wb[:, o[i]:o[i + 1]] for i in range(6)]
    ws = [wqa, wka, _extend_values(wva, A_KV_HEADS), wrw,
          jnp.concatenate([wqk, _extend_values(wvn, C_HEADS)], axis=1)]
    vone = jnp.asarray(np.arange(C_HEADS * V_EXT) % V_EXT == HEAD_DIM, F32).reshape(1, C_HEADS * V_EXT)
    const = lambda shape: pl.BlockSpec(shape, lambda b, t: (0,) * len(shape))
    mod_spec = lambda k: pl.BlockSpec(
        (None, None, 1, D), lambda b, t: (jnp.where(t >= n_lat_tiles, B, b), k, 0, 0))
    heads = lambda nh, width=HEAD_DIM: pl.BlockSpec((None, nh, tm, width), lambda b, t: (b, 0, t, 0))
    hshape = lambda nh, width=HEAD_DIM: jax.ShapeDtypeStruct((B, nh, T, width), BF16)
    return pl.pallas_call(
        functools.partial(_inproj_kernel, n_lat_tiles=n_lat_tiles),
        grid=(B, T // tm),
        in_specs=[*_segment_specs(D, n_lat_tiles, ctx_tile), const((1, D)), mod_spec(0), mod_spec(1)]
                 + [const(w.shape) for w in ws]
                 + [const((1, A_WIDTH)), const((1, A_KV_WIDTH)),
                    pl.BlockSpec((tm, LANES), lambda b, t: (t, 0)),
                    pl.BlockSpec((tm, LANES), lambda b, t: (t, 0)),
                    const(bd.shape), const(vone.shape)],
        out_specs=[heads(A_HEADS), heads(A_KV_HEADS), heads(A_KV_HEADS, V_EXT),
                   pl.BlockSpec((None, tm, RW_WIDTH), lambda b, t: (b, t, 0)),
                   heads(C_HEADS), heads(C_HEADS), heads(C_HEADS, V_EXT)],
        out_shape=[hshape(A_HEADS), hshape(A_KV_HEADS), hshape(A_KV_HEADS, V_EXT),
                   jax.ShapeDtypeStruct((B, T, RW_WIDTH), F32),
                   hshape(C_HEADS), hshape(C_HEADS), hshape(C_HEADS, V_EXT)],
        compiler_params=_cparams("parallel", "parallel"),
        name="in_projection",
    )(x_lat, x_ctx, g_pre.reshape(1, D), mod4, mod4, *ws,
      jnp.tile(q_gain, A_HEADS).reshape(1, A_WIDTH), jnp.tile(k_gain, A_KV_HEADS).reshape(1, A_KV_WIDTH),
      cos, sin, bd, vone)


def _softmax_weights(s, m):
    return jnp.exp((s - m).astype(BF16))


def _normalised(ov):
    return ov[:, :HEAD_DIM] / ov[:, HEAD_DIM:HEAD_DIM + 1]


def _gqa_kernel(q_ref, k_ref, v_ref, o_ref, *, n_lat, n_lat_tiles):
    group = q_ref.shape[0]

    def attend(k, v):
        scores = [_dot_nt(q_ref[hd], k) for hd in range(group)]
        probs = [_softmax_weights(s, jnp.max(s, axis=-1, keepdims=True)) for s in scores]
        for hd, p in enumerate(probs):
            o_ref[:, hd * HEAD_DIM:(hd + 1) * HEAD_DIM] = _normalised(_dot(p, v)).astype(o_ref.dtype)

    @pl.when(pl.program_id(2) < n_lat_tiles)
    def _():
        attend(k_ref[...], v_ref[...])

    @pl.when(pl.program_id(2) >= n_lat_tiles)
    def _():
        attend(k_ref[n_lat:, :], v_ref[n_lat:, :])


def _gqa_attention(q, k, v, n_lat, n_rows):
    B, _, T, _ = q.shape
    tq = ROW_TILE
    group = A_HEADS // A_KV_HEADS
    kv_spec = lambda width: pl.BlockSpec((None, None, T, width), lambda b, g, t: (b, g, 0, 0))
    return pl.pallas_call(
        functools.partial(_gqa_kernel, n_lat=n_lat, n_lat_tiles=n_lat // tq),
        grid=(B, A_KV_HEADS, n_rows // tq),
        in_specs=[pl.BlockSpec((None, group, tq, HEAD_DIM), lambda b, g, t: (b, g, t, 0)),
                  kv_spec(HEAD_DIM), kv_spec(V_EXT)],
        out_specs=pl.BlockSpec((None, tq, group * HEAD_DIM), lambda b, g, t: (b, t, g)),
        out_shape=jax.ShapeDtypeStruct((B, n_rows, A_WIDTH), BF16),
        compiler_params=_cparams("parallel", "parallel", "parallel"),
        name="gqa_attention",
    )(q, k, v)


def _natten_kernel(q_ref, k_ref, v_ref, bias_ref, o_ref, *, n_lat, rows, need_ctx):
    win = WIN_ROWS * GRID_W
    for hd in range(C_HEADS):
        kc = k_ref[hd, n_lat:, :]
        vc = v_ref[hd, n_lat:, :]

        def row_group(g, carry, hd=hd, kc=kc, vc=vc):
            scores = []
            for u in range(NATTEN_ROWS_PER_ITER):
                r = g * NATTEN_ROWS_PER_ITER + u
                rs = jnp.clip(r - WIN_ROWS // 2, 0, rows - WIN_ROWS)
                q = q_ref[hd, pl.ds(pl.multiple_of(r * GRID_W, GRID_W), GRID_W), :]
                kw = k_ref[hd, pl.ds(pl.multiple_of(rs * GRID_W, GRID_W), win), :]
                scores.append((r, rs, _dot_nt(q, kw) + bias_ref[r - rs, hd], _dot_nt(q, kc)))
            probs = []
            for r, rs, s_w, s_c in scores:
                m = jnp.maximum(jnp.max(s_w, axis=-1, keepdims=True), jnp.max(s_c, axis=-1, keepdims=True))
                probs.append((r, rs, _softmax_weights(s_w, m), _softmax_weights(s_c, m)))
            for r, rs, p_w, p_c in probs:
                vw = v_ref[hd, pl.ds(pl.multiple_of(rs * GRID_W, GRID_W), win), :]
                o = _normalised(_dot(p_w, vw) + _dot(p_c, vc))
                o_ref[pl.ds(pl.multiple_of(r * GRID_W, GRID_W), GRID_W),
                      hd * HEAD_DIM:(hd + 1) * HEAD_DIM] = o.astype(o_ref.dtype)
            return carry

        lax.fori_loop(0, rows // NATTEN_ROWS_PER_ITER, row_group, 0)
        if need_ctx:
            s = _dot_nt(q_ref[hd, n_lat:, :], kc)
            o = _normalised(_dot(_softmax_weights(s, jnp.max(s, axis=-1, keepdims=True)), vc))
            o_ref[n_lat:, hd * HEAD_DIM:(hd + 1) * HEAD_DIM] = o.astype(o_ref.dtype)


def _natten_bias(rpb, rows):
    off = np.arange(WIN_ROWS)[:, None, None]
    jr = np.arange(WIN_ROWS)[None, :, None]
    row_sel = (np.arange(2 * WIN_ROWS - 1)[None, None, :] == jr - off + WIN_ROWS - 1)
    qc = np.arange(GRID_W)[:, None, None]
    kc = np.arange(GRID_W)[None, :, None]
    col_start = np.clip(qc - WIN_COLS // 2, 0, GRID_W - WIN_COLS)
    valid = (kc >= col_start) & (kc < col_start + WIN_COLS)
    col_sel = (np.arange(2 * WIN_COLS - 1)[None, None, :] == kc - qc + WIN_COLS - 1) & valid
    bias = jnp.einsum("hrc,ojr,qkc->ohqjk", rpb.astype(F32), jnp.asarray(row_sel, F32), jnp.asarray(col_sel, F32),
                      precision=lax.Precision.HIGHEST)
    bias = bias + jnp.asarray(np.where(valid[None, None, :, None, :, 0], 0.0, MASK_VALUE), F32)
    return bias.reshape(WIN_ROWS, C_HEADS, GRID_W, WIN_ROWS * GRID_W)


def _neighbourhood_attention(q, k, v, bias, n_lat, need_ctx):
    B, _, T, _ = q.shape
    n_rows = T if need_ctx else n_lat
    full = lambda width: pl.BlockSpec((None, C_HEADS, T, width), lambda b: (b, 0, 0, 0))
    return pl.pallas_call(
        functools.partial(_natten_kernel, n_lat=n_lat, rows=n_lat // GRID_W, need_ctx=need_ctx),
        grid=(B,),
        in_specs=[full(HEAD_DIM), full(HEAD_DIM), full(V_EXT), pl.BlockSpec(bias.shape, lambda b: (0, 0, 0, 0))],
        out_specs=pl.BlockSpec((None, n_rows, C_WIDTH), lambda b: (b, 0, 0)),
        out_shape=jax.ShapeDtypeStruct((B, n_rows, C_WIDTH), BF16),
        compiler_params=_cparams("parallel"),
        name="neighbourhood_attention",
    )(q, k, v, bias)


def _rwkv_chunk_kernel(rw_ref, w0_ref, a0_ref, lora_ref, kk_ref, ka_ref, rk_ref, bdw_ref,
                       rhat_ref, yhat_ref, gt_ref, ht_ref, gate_ref, bonus_ref):
    C, W, NH = CHUNK, B_WIDTH, B_HEADS
    S = NH * C
    R = rw_ref.shape[0]
    rw = rw_ref[...]
    r, k, v = rw[:, :W], rw[:, W:2 * W], rw[:, 2 * W:3 * W]
    wl = rw[:, 3 * W:3 * W + 2 * DECAY_LORA]
    al = rw[:, 3 * W + 2 * DECAY_LORA:3 * W + 2 * DECAY_LORA + 2 * AAA_LORA]
    gl = rw[:, 3 * W + 2 * DECAY_LORA + 2 * AAA_LORA:]
    bdw = bdw_ref[...]
    kkv = k * kk_ref[...]
    kkv = kkv / jnp.maximum(jnp.sqrt(_head_sum(kkv * kkv, bdw)), 1e-12)
    lora = _dot(jnp.concatenate([jnp.tanh(wl), al, _sigmoid(gl)], axis=1), lora_ref[...])
    gate_ref[...] = lora[:, 4 * W:]
    lws, asigs = [], []
    for d in range(2):
        w = w0_ref[d] + lora[:, d * W:(d + 1) * W]
        w = -(jnp.maximum(-w, 0.0) + jnp.log(1.0 + jnp.exp(-jnp.abs(w)))) - 0.5
        lws.append(-jnp.exp(w))
        asigs.append(_sigmoid(a0_ref[d] + lora[:, (2 + d) * W:(3 + d) * W]))

    prow = lax.broadcasted_iota(jnp.int32, (R, R), 0)
    pcol = lax.broadcasted_iota(jnp.int32, (R, R), 1)
    tri = ((prow >= pcol) & (prow // C == pcol // C)).astype(BF16)
    cs = _dot(tri, jnp.concatenate([p for lw in lws for p in _split3(lw)], axis=1))
    prefix = [cs[:, 3 * d * W:(3 * d + 1) * W] + cs[:, (3 * d + 1) * W:(3 * d + 2) * W]
              + cs[:, (3 * d + 2) * W:(3 * d + 3) * W] for d in range(2)]

    srow = lax.broadcasted_iota(jnp.int32, (S, W), 0)
    scol = lax.broadcasted_iota(jnp.int32, (S, W), 1)
    same = (srow // C) == (scol // HEAD_DIM)
    crow = lax.broadcasted_iota(jnp.int32, (C, W), 0)
    cpos = lax.broadcasted_iota(jnp.int32, (C, W), 1) % HEAD_DIM
    eye_c = crow == cpos
    eye_b = eye_c.astype(BF16)

    def half_block(s):
        return (crow // (2 * s) == cpos // (2 * s)) & (crow // s != cpos // s)

    def stack(x):
        return jnp.where(same, jnp.concatenate([x] * NH, axis=0), 0.0).astype(BF16)

    kds = [k * (1.0 + (asig - 1.0) * ka_ref[...]) for asig in asigs]
    bonus_ref[...] = _head_sum(r * (kds[0] + kds[1]) * rk_ref[...], bdw) * v

    class Chain:
        pass

    chains = []
    for ci in range(R // C):
        v_s = stack(v[ci * C:(ci + 1) * C])
        for d in range(2):
            ch = Chain()
            ch.d, ch.ci, ch.rows, ch.v_s = d, ci, slice(ci * C, (ci + 1) * C), v_s
            chains.append(ch)

    for ch in chains:
        d, rows = ch.d, ch.rows
        lw, kd, b, a = lws[d][rows], kds[d][rows], (kkv * asigs[d])[rows], -kkv[rows]
        pre = prefix[d][rows]
        ch.tot = tot = pre[C - 1:C, :]
        cum_incl = pre if d == 0 else tot - pre + lw
        cum_excl = cum_incl - lw
        rho = 0.5 * tot
        e_in = jnp.exp(rho - cum_incl)
        e_end = jnp.exp(tot - cum_incl)
        ch.e_rho = jnp.exp(rho)
        ch.a_rho = a * jnp.exp(cum_excl - rho)
        ch.r_rho = r[rows] * jnp.exp(cum_incl - rho)
        ch.ends = jnp.concatenate([stack(b * e_end), stack(kd * e_end)], axis=0)
        ch.m = _dot_nt(jnp.concatenate([ch.a_rho, ch.r_rho], axis=0).astype(BF16),
                       jnp.concatenate([stack(b * e_in), stack(kd * e_in)], axis=0))

    for ch in chains:
        before, upto = (crow > cpos, crow >= cpos) if ch.d == 0 else (crow < cpos, crow <= cpos)
        m = ch.m
        l_ab = jnp.where(before, m[:C, :W], 0.0)
        ch.a_rb = jnp.where(upto, m[C:, :W], 0.0).astype(BF16)
        ch.kv = jnp.concatenate([jnp.where(before, m[:C, W:], 0.0), jnp.where(upto, m[C:, W:], 0.0)],
                                axis=0).astype(BF16)
        ch.l_ab = l_ab
        ch.tinv = jnp.where(eye_c, 1.0, jnp.where(half_block(1), l_ab, 0.0))
    for s in [2 ** i for i in range(1, int(np.log2(C)))]:
        for ch in chains:
            ch.p = _dot(jnp.where(half_block(s), ch.l_ab, 0.0).astype(BF16), stack(ch.tinv))
        for ch in chains:
            ch.tinv = ch.tinv + _dot(ch.tinv.astype(BF16), stack(ch.p))
    for ch in chains:
        ch.wv = _dot(ch.kv, ch.v_s)
    for ch in chains:
        ch.z = _dot(ch.tinv.astype(BF16),
                    jnp.concatenate([stack(ch.a_rho), stack(ch.wv[:C])], axis=1))
    for ch in chains:
        ch.ahat = ch.z[:, :W] * ch.e_rho
        ch.av = jnp.concatenate([stack(ch.z[:, :W]), stack(ch.z[:, W:])], axis=1)
        rz = _dot(ch.a_rb, ch.av)
        rhat_ref[ch.d, ch.rows] = (ch.r_rho + rz[:, :W]) * ch.e_rho
        yhat_ref[ch.d, ch.rows] = rz[:, W:] + ch.wv[C:]
    for ch in chains:
        ch.ends_t = _dot_nt(eye_b, ch.ends).astype(BF16)
    for ch in chains:
        gh = _dot(ch.ends_t[:, :S], jnp.concatenate([stack(ch.ahat), ch.av[:, W:]], axis=1))
        gt_ref[ch.d, ch.ci] = gh[:, :W] + jnp.where(eye_c, jnp.exp(ch.tot), 0.0)
        ht_ref[ch.d, ch.ci] = gh[:, W:] + _dot(ch.ends_t[:, S:], ch.v_s)


def _rwkv_chunks(rw, p, bdw):
    B, T, _ = rw.shape
    C, W = CHUNK, B_WIDTH
    nc = T // C
    per = RWKV_CHUNKS_PER_STEP
    assert C == HEAD_DIM and nc % per == 0
    const = lambda shape: pl.BlockSpec(shape, lambda b, c: (0,) * len(shape))
    rows = pl.BlockSpec((None, 2, per * C, W), lambda b, c: (b, 0, c, 0))
    mats = pl.BlockSpec((None, 2, per, C, W), lambda b, c: (b, 0, c, 0, 0))
    flat = pl.BlockSpec((None, per * C, W), lambda b, c: (b, c, 0))
    vec = lambda a: a.reshape(1, W)
    lora_w = jnp.zeros((2 * DECAY_LORA + 2 * AAA_LORA + GATE_LORA, 5 * W), F32)
    for d in range(2):
        lora_w = lora_w.at[d * DECAY_LORA:(d + 1) * DECAY_LORA, d * W:(d + 1) * W].set(p["w_up"][d])
        lora_w = lora_w.at[2 * DECAY_LORA + d * AAA_LORA:2 * DECAY_LORA + (d + 1) * AAA_LORA,
                           (2 + d) * W:(3 + d) * W].set(p["a_up"][d])
    lora_w = lora_w.at[2 * DECAY_LORA + 2 * AAA_LORA:, 4 * W:].set(p["g_up"])
    return pl.pallas_call(
        _rwkv_chunk_kernel,
        grid=(B, nc // per),
        in_specs=[pl.BlockSpec((None, per * C, RW_WIDTH), lambda b, c: (b, c, 0)),
                  const((2, 1, W)), const((2, 1, W)), const(lora_w.shape),
                  const((1, W)), const((1, W)), const((1, W)), const(bdw.shape)],
        out_specs=[rows, rows, mats, mats, flat, flat],
        out_shape=[jax.ShapeDtypeStruct((B, 2, T, W), F32), jax.ShapeDtypeStruct((B, 2, T, W), F32),
                   jax.ShapeDtypeStruct((B, 2, nc, C, W), F32), jax.ShapeDtypeStruct((B, 2, nc, C, W), F32),
                   jax.ShapeDtypeStruct((B, T, W), F32), jax.ShapeDtypeStruct((B, T, W), F32)],
        compiler_params=_cparams("parallel", "parallel"),
        name="rwkv_chunks",
    )(rw, p["w0"].reshape(2, 1, W), p["a0"].reshape(2, 1, W), lora_w,
      vec(p["k_k"]), vec(p["k_a"]), vec(p["r_k"]), bdw)


def _rwkv_scan_kernel(rhat0_ref, yhat0_ref, gt0_ref, ht0_ref, rhat1_ref, yhat1_ref, gt1_ref, ht1_ref,
                      y0_ref, y1_ref, st_ref):
    @pl.when(pl.program_id(0) == 0)
    def _():
        st_ref[...] = jnp.zeros_like(st_ref)

    W = B_WIDTH
    same = (lax.broadcasted_iota(jnp.int32, (W, W), 0) // HEAD_DIM
            == lax.broadcasted_iota(jnp.int32, (W, W), 1) // HEAD_DIM)
    expand = lambda x: jnp.where(same, jnp.concatenate([x] * B_HEADS, axis=0), 0.0)
    dirs = ((rhat0_ref, yhat0_ref, gt0_ref, ht0_ref, y0_ref), (rhat1_ref, yhat1_ref, gt1_ref, ht1_ref, y1_ref))
    for b in range(st_ref.shape[1]):
        for d, (rhat_ref, yhat_ref, gt_ref, ht_ref, y_ref) in enumerate(dirs):
            stb = st_ref[d, b].astype(BF16)
            y_ref[b] = _dot(rhat_ref[b].astype(BF16), stb) + yhat_ref[b]
            st_ref[d, b] = _dot(expand(gt_ref[b]).astype(BF16), stb) + expand(ht_ref[b])


def _rwkv_scan(rhat, yhat, gt, ht, n_lat_chunks):
    B, _, T, W = rhat.shape
    C = CHUNK
    nc = T // C
    chunk = (lambda s: (s + n_lat_chunks) % nc, lambda s: nc - 1 - s)
    rows = lambda d: pl.BlockSpec((B, None, C, W), lambda s: (0, d, chunk[d](s), 0))
    mats = lambda d: pl.BlockSpec((B, None, None, C, W), lambda s: (0, d, chunk[d](s), 0, 0))
    outs = lambda d: pl.BlockSpec((B, C, W), lambda s: (0, chunk[d](s), 0))
    return pl.pallas_call(
        _rwkv_scan_kernel,
        grid=(nc,),
        in_specs=[rows(0), rows(0), mats(0), mats(0), rows(1), rows(1), mats(1), mats(1)],
        out_specs=[outs(0), outs(1)],
        out_shape=[jax.ShapeDtypeStruct((B, T, W), F32)] * 2,
        scratch_shapes=[pltpu.VMEM((2, B, W, W), F32)],
        compiler_params=_cparams("arbitrary"),
        name="rwkv_scan",
    )(rhat, yhat, gt, ht, rhat, yhat, gt, ht)


def _outproj_kernel(xl_ref, xc_ref, oa_ref, y0_ref, y1_ref, bonus_ref, gate_ref, on_ref, lnw_ref, lnb_ref, bd_ref,
                    wa_ref, wb_ref, wn_ref, g_ref, mgate_ref, o_ref, *, n_lat_tiles):
    y = y0_ref[...] + y1_ref[...]
    bd = bd_ref[...]
    mu = _head_sum(y, bd) * (1.0 / HEAD_DIM)
    yc = y - mu
    var = _head_sum(yc * yc, bd) * (1.0 / HEAD_DIM)
    ob = (yc * lax.rsqrt(var + LNX_EPS) * lnw_ref[...] + lnb_ref[...] + bonus_ref[...]) * gate_ref[...]
    out = _dot(oa_ref[...], wa_ref[...]) + _dot(ob.astype(BF16), wb_ref[...]) + _dot(on_ref[...], wn_ref[...])
    o_ref[...] = _segment_rows(xl_ref, xc_ref, n_lat_tiles) + mgate_ref[...] * _rms(out, g_ref[...])


def _out_projection(x_lat, x_ctx, ctx_tile, oa, y0, y1, bonus, gate, on, p, bd, w_out, g_post, mod4,
                    n_lat_tiles, n_rows):
    B, _, D = x_lat.shape
    tm = ROW_TILE
    W = B_WIDTH
    wb = w_out.astype(BF16)
    const = lambda shape: pl.BlockSpec(shape, lambda b, t: (0,) * len(shape))
    rows = lambda width: pl.BlockSpec((None, tm, width), lambda b, t: (b, t, 0))
    return pl.pallas_call(
        functools.partial(_outproj_kernel, n_lat_tiles=n_lat_tiles),
        grid=(B, n_rows // tm),
        in_specs=[*_segment_specs(D, n_lat_tiles, ctx_tile),
                  rows(A_WIDTH), rows(W), rows(W), rows(W), rows(W), rows(C_WIDTH),
                  const((1, W)), const((1, W)), const(bd.shape),
                  const((A_WIDTH, D)), const((W, D)), const((C_WIDTH, D)), const((1, D)),
                  pl.BlockSpec((None, None, 1, D), lambda b, t: (jnp.where(t >= n_lat_tiles, B, b), 2, 0, 0))],
        out_specs=rows(D),
        out_shape=jax.ShapeDtypeStruct((B, n_rows, D), F32),
        compiler_params=_cparams("parallel", "parallel"),
        name="out_projection",
    )(x_lat, x_ctx, oa, y0, y1, bonus, gate, on, p["lnx_w"].reshape(1, W), p["lnx_b"].reshape(1, W), bd,
      wb[:A_WIDTH], wb[A_WIDTH:A_WIDTH + W], wb[A_WIDTH + W:], g_post.reshape(1, D), mod4)


def _router_kernel(x_ref, g_ref, shift_ref, scale_ref, wr_ref, h_ref, aff_ref):
    h = _rms(x_ref[...], g_ref[...]) * (1.0 + scale_ref[...]) + shift_ref[...]
    h_ref[...] = h.astype(h_ref.dtype)
    logits = _dot_nt(wr_ref[...], h)
    p = jnp.exp(logits - jnp.max(logits, axis=0, keepdims=True))
    aff_ref[...] = p / jnp.sum(p, axis=0, keepdims=True)


def _router(xs, g_pre, mod4, w_router, n_lat_tiles):
    B, R, D = xs.shape
    tm = ROW_TILE
    E = w_router.shape[1]
    mod_spec = lambda k: pl.BlockSpec(
        (None, None, 1, D), lambda b, t: (jnp.where(t >= n_lat_tiles, B, b), k, 0, 0))
    return pl.pallas_call(
        _router_kernel,
        grid=(B, R // tm),
        in_specs=[pl.BlockSpec((None, tm, D), lambda b, t: (b, t, 0)),
                  pl.BlockSpec((1, D), lambda b, t: (0, 0)), mod_spec(3), mod_spec(4),
                  pl.BlockSpec((E, D), lambda b, t: (0, 0))],
        out_specs=[pl.BlockSpec((None, tm, D), lambda b, t: (b, t, 0)),
                   pl.BlockSpec((None, E, tm), lambda b, t: (b, 0, t))],
        out_shape=[jax.ShapeDtypeStruct((B, R, D), BF16), jax.ShapeDtypeStruct((B, E, R), F32)],
        compiler_params=_cparams("parallel", "parallel"),
        name="moe_router",
    )(xs, g_pre.reshape(1, D), mod4, mod4, w_router.T)


def _column(x, e):
    lane = lax.broadcasted_iota(jnp.int32, x.shape, 1)
    return jnp.sum(jnp.where(lane == e, x, 0.0), axis=1, keepdims=True)


RANK_BLOCK = 256


def _rank_partials(a_row, a_col):
    n = a_row.shape[-1]
    tj = min(n, RANK_BLOCK)
    as_count = lambda mask: mask.astype(F32).astype(BF16)

    def partial(jc):
        lo, hi = jc * tj, (jc + 1) * tj
        aj = a_col[lo:hi, :]
        ai = a_row[:, lo:hi]
        j_first = (lax.broadcasted_iota(jnp.int32, (tj, tj), 0) < lax.broadcasted_iota(jnp.int32, (tj, tj), 1))
        parts = [as_count(aj > a_row[:, :lo])] if lo else []
        parts.append(as_count((aj > ai) | ((aj == ai) & j_first)))
        if hi < n:
            parts.append(as_count(aj >= a_row[:, hi:]))
        return _dot(jnp.ones((8, tj), BF16), parts[0] if len(parts) == 1 else jnp.concatenate(parts, axis=1))

    return [functools.partial(partial, jc) for jc in range(n // tj)]


def _token_ranks(a_row, a_col):
    cnt = None
    for thunk in _rank_partials(a_row, a_col):
        cnt = thunk() if cnt is None else cnt + thunk()
    return cnt[0:1, :]


def _propose_threshold(aff, cap):
    keys = lax.bitcast_convert_type(aff, jnp.int32)
    t = jnp.zeros((aff.shape[0], 1), jnp.int32)
    for bit in range(30, -1, -1):
        cand = t | (1 << bit)
        cnt = jnp.sum((keys >= cand).astype(F32), axis=1, keepdims=True)
        t = jnp.where(cnt >= cap, cand, t)
    return lax.bitcast_convert_type(t, F32)


def _prefix_count(mask):
    n = mask.shape[1]
    blk = min(n, RANK_BLOCK)
    tri = (lax.broadcasted_iota(jnp.int32, (blk, blk), 0) <= lax.broadcasted_iota(jnp.int32, (blk, blk), 1))
    tri = tri.astype(F32).astype(BF16)
    ones = mask.astype(F32).astype(BF16)
    outs = []
    for c in range(n // blk):
        p = _dot(ones[:, c * blk:(c + 1) * blk], tri)
        outs.append(p + outs[-1][:, blk - 1:blk] if outs else p)
    return outs[0] if len(outs) == 1 else jnp.concatenate(outs, axis=1)


def _slots_from_threshold(aff, tau, cap):
    above, equal = aff > tau, aff == tau
    n_above = jnp.sum(above.astype(F32), axis=1, keepdims=True)
    n_equal = jnp.sum(equal.astype(F32), axis=1, keepdims=True)
    ok = (n_above < cap) & (n_above + n_equal >= cap)
    take = above | (equal & (_prefix_count(equal) <= cap - n_above))
    return jnp.where(take, _prefix_count(take) - 1.0, -1.0), ok


def _expert_kernel(h_ref, aff_ref, affc_ref, wg_ref, wu_ref, wd_ref, f_ref, yet_ref, ye_ref, ft_ref, rank_ref,
                   *, cap, groups):
    n = h_ref.shape[0]
    E = aff_ref.shape[0]
    e = pl.program_id(1)

    @pl.when(e == 0)
    def _():
        seg, gcap = n // groups, cap // groups
        bad = 0.0
        for g in range(groups):
            cols = slice(g * seg, (g + 1) * seg)
            aff = aff_ref[:, cols]
            slots, ok = _slots_from_threshold(aff, _propose_threshold(aff, gcap), gcap)
            rank_ref[:, cols] = jnp.where(slots >= 0.0, slots + g * gcap, -1.0)
            bad = bad + jnp.sum(jnp.where(ok, 0.0, 1.0))

        @pl.when(bad > 0.0)
        def _():
            for g in range(groups):
                cols = slice(g * seg, (g + 1) * seg)

                def rerank(x, carry, g=g, cols=cols):
                    r = _token_ranks(aff_ref[pl.ds(x, 1), cols], _column(affc_ref[cols, :], x))
                    rank_ref[pl.ds(x, 1), cols] = jnp.where(r < gcap, r + g * gcap, -1.0)
                    return carry
                lax.fori_loop(0, E, rerank, 0)

    slot = lax.broadcasted_iota(jnp.int32, (cap, n), 0).astype(F32)
    pick = (rank_ref[pl.ds(e, 1), :] == slot).astype(F32).astype(BF16)
    xe = _dot(pick, h_ref[...]).astype(BF16)
    gte = _dot(xe, wg_ref[...])
    hid = (gte * _sigmoid(gte) * _dot(xe, wu_ref[...])).astype(BF16)
    ye_ref[...] = _dot(hid, wd_ref[...])
    yet_ref[:, pl.ds(pl.multiple_of(e * cap, cap), cap)] = ye_ref[...].T.astype(BF16)

    @pl.when(e == E - 1)
    def _():
        tn = min(n, ROW_TILE)
        slot_c = lax.broadcasted_iota(jnp.int32, (cap, tn), 0).astype(F32)
        for c in range(n // tn):
            cols = slice(c * tn, (c + 1) * tn)
            put = jnp.concatenate(
                [jnp.where(rank_ref[x:x + 1, cols] == slot_c, aff_ref[x:x + 1, cols], 0.0).astype(BF16)
                 for x in range(E)], axis=0)
            ft_ref[...] = _dot(yet_ref[...], put)
            f_ref[cols, :] = ft_ref[...].T


def _experts(h, aff, weights, layer, cap, groups):
    S, E, n = aff.shape
    D = h.shape[-1]
    F = weights[0].shape[-1]
    assert cap % LANES == 0 and (n // groups) % LANES == 0
    weight = lambda a, b: pl.BlockSpec((None, None, a, b), lambda s, e: (layer, e, 0, 0))
    return pl.pallas_call(
        functools.partial(_expert_kernel, cap=cap, groups=groups),
        grid=(S, E),
        in_specs=[pl.BlockSpec((None, n, D), lambda s, e: (s, 0, 0)),
                  pl.BlockSpec((None, E, n), lambda s, e: (s, 0, 0)),
                  pl.BlockSpec((None, n, E), lambda s, e: (s, 0, 0)),
                  weight(D, F), weight(D, F), weight(F, D)],
        out_specs=pl.BlockSpec((None, n, D), lambda s, e: (s, 0, 0)),
        out_shape=jax.ShapeDtypeStruct((S, n, D), F32),
        scratch_shapes=[pltpu.VMEM((D, E * cap), BF16), pltpu.VMEM((cap, D), F32),
                        pltpu.VMEM((D, min(n, ROW_TILE)), F32), pltpu.VMEM((E, n), F32)],
        compiler_params=_cparams("parallel", "arbitrary"),
        name="moe_experts",
    )(h, aff, jnp.swapaxes(aff, 1, 2), *weights)


def _ffn_residual_kernel(x_ref, fl_ref, fc_ref, g_ref, mgate_ref, o_ref, *, n_lat_tiles):
    f = _segment_rows(fl_ref, fc_ref, n_lat_tiles)
    o_ref[...] = x_ref[...] + mgate_ref[...] * _rms(f, g_ref[...])


def _ffn_residual(xs, f_lat, f_ctx, g_post, mod4, n_lat_tiles):
    B, R, D = xs.shape
    tm = ROW_TILE
    rows = pl.BlockSpec((None, tm, D), lambda b, t: (b, t, 0))
    return pl.pallas_call(
        functools.partial(_ffn_residual_kernel, n_lat_tiles=n_lat_tiles),
        grid=(B, R // tm),
        in_specs=[rows, *_segment_specs(D, n_lat_tiles, 0), pl.BlockSpec((1, D), lambda b, t: (0, 0)),
                  pl.BlockSpec((None, None, 1, D), lambda b, t: (jnp.where(t >= n_lat_tiles, B, b), 5, 0, 0))],
        out_specs=rows,
        out_shape=jax.ShapeDtypeStruct((B, R, D), F32),
        compiler_params=_cparams("parallel", "parallel"),
        name="ffn_residual",
    )(xs, f_lat, f_lat if f_ctx is None else f_ctx, g_post.reshape(1, D), mod4)


def _moe(h, aff_t, weights, layer, n_lat, n_ctx):
    B, _, D = h.shape
    E = aff_t.shape[1]
    f_l = _experts(h, aff_t[:, :, :n_lat], weights, layer, CAPACITY_FACTOR * n_lat // E, 1)
    if not n_ctx:
        return f_l, None
    a_c = jnp.swapaxes(aff_t[:, :, n_lat:], 0, 1).reshape(1, E, B * n_ctx)
    f_c = _experts(h[:, n_lat:].reshape(1, B * n_ctx, D), a_c, weights, layer,
                   B * (CAPACITY_FACTOR * n_ctx // E), B)
    return f_l, f_c.reshape(B, n_ctx, D)


def _rope_tables(n_lat, n_ctx):
    inv = ROPE_BASE ** (-jnp.arange(ROPE_HALF, dtype=F32) / ROPE_HALF)
    pos = jnp.arange(n_lat)
    ang_r = (pos // GRID_W).astype(F32)[:, None] * inv[None, :]
    ang_c = (pos % GRID_W).astype(F32)[:, None] * inv[None, :]
    cos = jnp.concatenate([jnp.cos(ang_r)] * 2 + [jnp.cos(ang_c)] * 2, axis=1)
    sin = jnp.concatenate([-jnp.sin(ang_r), jnp.sin(ang_r), -jnp.sin(ang_c), jnp.sin(ang_c)], axis=1)
    cos = jnp.concatenate([jnp.tile(cos, (1, LANES // HEAD_DIM)), jnp.ones((n_ctx, LANES), F32)], axis=0)
    sin = jnp.concatenate([jnp.tile(sin, (1, LANES // HEAD_DIM)), jnp.zeros((n_ctx, LANES), F32)], axis=0)
    return cos, sin


def kernel(x, c, ctx, c_ctx, w_mod, b_mod, g_pre_mix, g_post_mix, g_pre_ffn, g_post_ffn, w_in, w_out, q_gain, k_gain, rpb, rk_w0, rk_w_up, rk_a0, rk_a_up, rk_g_up, rk_k_k, rk_k_a, rk_r_k, rk_lnx_w, rk_lnx_b, w_router, w_e_gate, w_e_up, w_e_down):
    B, n_lat, D = x.shape
    n_ctx = ctx.shape[1]
    T = n_lat + n_ctx
    depth = w_mod.shape[0]
    assert n_lat % ROW_TILE == 0 and n_ctx == ROW_TILE and n_lat % n_ctx == 0 and B < MOD_ROWS
    n_lat_tiles = n_lat // ROW_TILE

    cond = jnp.zeros((MOD_ROWS, D), F32).at[:B].set(c).at[B].set(c_ctx)
    mod = _modulation(cond, w_mod, b_mod)
    cos, sin = _rope_tables(n_lat, n_ctx)
    lane = np.arange(HEAD_SUM_TILE)
    bd = jnp.asarray(lane[:, None] // HEAD_DIM == lane[None, :] // HEAD_DIM, BF16)

    weights = (w_e_gate.astype(BF16), w_e_up.astype(BF16), w_e_down.astype(BF16))
    x_lat, x_ctx, ctx_tile = x, ctx, 0
    for i in range(depth):
        need_ctx = i < depth - 1
        n_rows = T if need_ctx else n_lat
        mod4 = mod[i].reshape(MOD_ROWS, 6, 1, D)
        qa, ka, va, rw, qn, kn, vn = _in_projection(
            x_lat, x_ctx, ctx_tile, mod4, g_pre_mix[i], w_in[i], q_gain[i], k_gain[i], cos, sin, bd, n_lat_tiles)
        oa = _gqa_attention(qa, ka, va, n_lat, n_rows)
        on = _neighbourhood_attention(qn, kn, vn, _natten_bias(rpb[i], n_lat // GRID_W), n_lat, need_ctx)
        rk = dict(w0=rk_w0[i], w_up=rk_w_up[i], a0=rk_a0[i], a_up=rk_a_up[i], g_up=rk_g_up[i],
                  k_k=rk_k_k[i], k_a=rk_k_a[i], r_k=rk_r_k[i], lnx_w=rk_lnx_w[i], lnx_b=rk_lnx_b[i])
        rhat, yhat, gt, ht, gate, bonus = _rwkv_chunks(rw, rk, bd)
        y0, y1 = _rwkv_scan(rhat, yhat, gt, ht, n_lat // CHUNK)
        xs = _out_projection(x_lat, x_ctx, ctx_tile, oa, y0, y1, bonus, gate, on, rk, bd, w_out[i], g_post_mix[i], mod4,
                             n_lat_tiles, n_rows)
        h, aff_t = _router(xs, g_pre_ffn[i], mod4, w_router[i], n_lat_tiles)
        f_lat, f_ctx = _moe(h, aff_t, weights, i, n_lat, n_ctx if need_ctx else 0)
        x_lat = x_ctx = _ffn_residual(xs, f_lat, f_ctx, g_post_ffn[i], mod4, n_lat_tiles)
        ctx_tile = n_lat_tiles
    return x_lat
```

```python
import functools

import numpy as np
import jax
import jax.numpy as jnp
from jax import lax
from jax.experimental import pallas as pl
from jax.experimental.pallas import tpu as pltpu

F32 = jnp.float32
BF16 = jnp.bfloat16

HEAD_DIM = 64
GRID_W = 64
A_HEADS = 8
A_KV_HEADS = 2
B_HEADS = 4
C_HEADS = 4
A_WIDTH = A_HEADS * HEAD_DIM
A_KV_WIDTH = A_KV_HEADS * HEAD_DIM
B_WIDTH = B_HEADS * HEAD_DIM
C_WIDTH = C_HEADS * HEAD_DIM
DECAY_LORA = 64
AAA_LORA = 64
GATE_LORA = 128
RW_WIDTH = 3 * B_WIDTH + 2 * DECAY_LORA + 2 * AAA_LORA + GATE_LORA
ROPE_BASE = 10000.0
ROPE_HALF = HEAD_DIM // 4
WIN_ROWS = 8
WIN_COLS = 16
N_EXPERTS = 16
CAPACITY_FACTOR = 2
NORM_EPS = 1e-6
LNX_EPS = 64e-5
ATTN_SCALE = HEAD_DIM ** -0.5
V_EXT = 2 * HEAD_DIM
MASK_VALUE = -1e30

LANES = 128
HEAD_SUM_TILE = 256
ROW_TILE = 256
CHUNK = 64
RWKV_CHUNKS_PER_STEP = 4
NATTEN_ROWS_PER_ITER = 8
MOD_ROWS = 16
VMEM_LIMIT = 56 * 1024 * 1024


def _cparams(*sem):
    return pltpu.CompilerParams(dimension_semantics=sem, vmem_limit_bytes=VMEM_LIMIT)


def _dot(a, b):
    return jnp.dot(a, b, preferred_element_type=F32)


def _dot_nt(a, b):
    return lax.dot_general(a, b, (((1,), (1,)), ((), ())), preferred_element_type=F32)


def _dot_tn(a, b):
    return lax.dot_general(a, b, (((0,), (0,)), ((), ())), preferred_element_type=F32)


def _split2(x):
    hi = x.astype(BF16)
    lo = (x - hi.astype(F32)).astype(BF16)
    return hi, lo


def _split3(x):
    hi = x.astype(BF16)
    r1 = x - hi.astype(F32)
    mid = r1.astype(BF16)
    lo = (r1 - mid.astype(F32)).astype(BF16)
    return hi, mid, lo


def _head_sum(x, bd):
    rows, width = x.shape
    tile = bd.shape[0]
    outs = []
    for c in range(0, width, tile):
        w = min(tile, width - c)
        hi, lo = _split2(x[:, c:c + w])
        p = _dot(jnp.concatenate([hi, lo], axis=0), bd[:w, :w])
        outs.append(p[:rows] + p[rows:])
    return outs[0] if len(outs) == 1 else jnp.concatenate(outs, axis=1)


def _tile_lanes(t, width):
    reps = width // t.shape[1]
    return t if reps == 1 else jnp.concatenate([t] * reps, axis=1)


def _rms(x, g):
    return x * lax.rsqrt(jnp.mean(x * x, axis=-1, keepdims=True) + NORM_EPS) * g


def _sigmoid(x):
    return 1.0 / (1.0 + jnp.exp(-x))


def _segment_specs(width, n_lat_tiles, ctx_tile):
    return (pl.BlockSpec((None, ROW_TILE, width), lambda b, t: (b, jnp.minimum(t, n_lat_tiles - 1), 0)),
            pl.BlockSpec((None, ROW_TILE, width), lambda b, t: (b, ctx_tile, 0)))


def _segment_rows(lat_ref, ctx_ref, n_lat_tiles):
    return jnp.where(pl.program_id(1) >= n_lat_tiles, ctx_ref[...], lat_ref[...])


def _mod_kernel(s_ref, w_ref, b_ref, o_ref):
    s = s_ref[...]
    s = s * _sigmoid(s)
    o_ref[...] = _dot(s, w_ref[...]) + b_ref[...]


def _modulation(cond, w_mod, b_mod):
    L, D, N = w_mod.shape
    tn = 512
    return pl.pallas_call(
        _mod_kernel,
        grid=(L, N // tn),
        in_specs=[pl.BlockSpec((MOD_ROWS, D), lambda l, j: (0, 0)),
                  pl.BlockSpec((None, D, tn), lambda l, j: (l, 0, j)),
                  pl.BlockSpec((None, 1, tn), lambda l, j: (l, 0, j))],
        out_specs=pl.BlockSpec((None, MOD_ROWS, tn), lambda l, j: (l, 0, j)),
        out_shape=jax.ShapeDtypeStruct((L, MOD_ROWS, N), F32),
        compiler_params=_cparams("parallel", "parallel"),
        name="modulation",
    )(cond, w_mod, b_mod.reshape(L, 1, N))


def _rope(x, cos, sin):
    w = x.shape[1]
    lane = lax.broadcasted_iota(jnp.int32, x.shape, 1)
    upper = (lane // ROPE_HALF) % 2 == 1
    partner = jnp.where(upper, pltpu.roll(x, ROPE_HALF, 1), pltpu.roll(x, w - ROPE_HALF, 1))
    return x * _tile_lanes(cos, w) + partner * _tile_lanes(sin, w)


def _inproj_kernel(xl_ref, xc_ref, g_ref, shift_ref, scale_ref, wqa_ref, wka_ref, wva_ref, wrw_ref, wn_ref,
                   qg_ref, kg_ref, cos_ref, sin_ref, bd_ref, vone_ref,
                   qa_ref, ka_ref, va_ref, rw_ref, qn_ref, kn_ref, vn_ref, *, n_lat_tiles):
    x = _segment_rows(xl_ref, xc_ref, n_lat_tiles)
    h = _rms(x, g_ref[...]) * (1.0 + scale_ref[...]) + shift_ref[...]
    hb = h.astype(BF16)
    bd = bd_ref[...]
    cos, sin = cos_ref[...], sin_ref[...]

    def normed(w_ref, gain_ref):
        y = _dot(hb, w_ref[...])
        ms = _head_sum(y * y, bd) * (1.0 / HEAD_DIM)
        return _rope(y * lax.rsqrt(ms + NORM_EPS) * gain_ref[...], cos, sin)

    def put_heads(o_ref, y, width=HEAD_DIM):
        for hd in range(y.shape[1] // width):
            o_ref[hd] = y[:, hd * width:(hd + 1) * width].astype(o_ref.dtype)

    put_heads(qa_ref, normed(wqa_ref, qg_ref) * ATTN_SCALE)
    put_heads(ka_ref, normed(wka_ref, kg_ref))
    vone = vone_ref[...]
    put_heads(va_ref, _dot(hb, wva_ref[...]) + vone[:, :A_KV_HEADS * V_EXT], V_EXT)
    rw_ref[...] = _dot(hb, wrw_ref[...])
    yn = _dot(hb, wn_ref[...])
    put_heads(qn_ref, yn[:, :C_WIDTH] * ATTN_SCALE)
    put_heads(kn_ref, yn[:, C_WIDTH:2 * C_WIDTH])
    put_heads(vn_ref, yn[:, 2 * C_WIDTH:] + vone, V_EXT)


def _extend_values(w, heads):
    D = w.shape[0]
    w = jnp.pad(w.reshape(D, heads, HEAD_DIM), ((0, 0), (0, 0), (0, V_EXT - HEAD_DIM)))
    return w.reshape(D, heads * V_EXT)


def _in_projection(x_lat, x_ctx, ctx_tile, mod4, g_pre, w_in, q_gain, k_gain, cos, sin, bd, n_lat_tiles):
    B, _, D = x_lat.shape
    tm = ROW_TILE
    T = (n_lat_tiles + 1) * tm
    o = np.cumsum([0, A_WIDTH, A_KV_WIDTH, A_KV_WIDTH, RW_WIDTH, 2 * C_WIDTH, C_WIDTH])
    wb = w_in.astype(BF16)
    wqa, wka, wva, wrw, wqk, wvn = [wb[:, o[i]:o[i + 1]] for i in range(6)]
    ws = [wqa, wka, _extend_values(wva, A_KV_HEADS), wrw,
          jnp.concatenate([wqk, _extend_values(wvn, C_HEADS)], axis=1)]
    vone = jnp.asarray(np.arange(C_HEADS * V_EXT) % V_EXT == HEAD_DIM, F32).reshape(1, C_HEADS * V_EXT)
    const = lambda shape: pl.BlockSpec(shape, lambda b, t: (0,) * len(shape))
    mod_spec = lambda k: pl.BlockSpec(
        (None, None, 1, D), lambda b, t: (jnp.where(t >= n_lat_tiles, B, b), k, 0, 0))
    heads = lambda nh, width=HEAD_DIM: pl.BlockSpec((None, nh, tm, width), lambda b, t: (b, 0, t, 0))
    hshape = lambda nh, width=HEAD_DIM: jax.ShapeDtypeStruct((B, nh, T, width), BF16)
    return pl.pallas_call(
        functools.partial(_inproj_kernel, n_lat_tiles=n_lat_tiles),
        grid=(B, T // tm),
        in_specs=[*_segment_specs(D, n_lat_tiles, ctx_tile), const((1, D)), mod_spec(0), mod_spec(1)]
                 + [const(w.shape) for w in ws]
                 + [const((1, A_WIDTH)), const((1, A_KV_WIDTH)),
                    pl.BlockSpec((tm, LANES), lambda b, t: (t, 0)),
                    pl.BlockSpec((tm, LANES), lambda b, t: (t, 0)),
                    const(bd.shape), const(vone.shape)],
        out_specs=[heads(A_HEADS), heads(A_KV_HEADS), heads(A_KV_HEADS, V_EXT),
                   pl.BlockSpec((None, tm, RW_WIDTH), lambda b, t: (b, t, 0)),
                   heads(C_HEADS), heads(C_HEADS), heads(C_HEADS, V_EXT)],
        out_shape=[hshape(A_HEADS), hshape(A_KV_HEADS), hshape(A_KV_HEADS, V_EXT),
                   jax.ShapeDtypeStruct((B, T, RW_WIDTH), F32),
                   hshape(C_HEADS), hshape(C_HEADS), hshape(C_HEADS, V_EXT)],
        compiler_params=_cparams("parallel", "parallel"),
        name="in_projection",
    )(x_lat, x_ctx, g_pre.reshape(1, D), mod4, mod4, *ws,
      jnp.tile(q_gain, A_HEADS).reshape(1, A_WIDTH), jnp.tile(k_gain, A_KV_HEADS).reshape(1, A_KV_WIDTH),
      cos, sin, bd, vone)


def _softmax_weights(s, m):
    return jnp.exp((s - m).astype(BF16))


def _normalised(ov):
    return ov[:, :HEAD_DIM] / ov[:, HEAD_DIM:HEAD_DIM + 1]


def _gqa_kernel(q_ref, k_ref, v_ref, o_ref, *, n_lat, n_lat_tiles):
    group = q_ref.shape[0]

    def attend(k, v):
        scores = [_dot_nt(q_ref[hd], k) for hd in range(group)]
        probs = [_softmax_weights(s, jnp.max(s, axis=-1, keepdims=True)) for s in scores]
        for hd, p in enumerate(probs):
            o_ref[:, hd * HEAD_DIM:(hd + 1) * HEAD_DIM] = _normalised(_dot(p, v)).astype(o_ref.dtype)

    @pl.when(pl.program_id(2) < n_lat_tiles)
    def _():
        attend(k_ref[...], v_ref[...])

    @pl.when(pl.program_id(2) >= n_lat_tiles)
    def _():
        attend(k_ref[n_lat:, :], v_ref[n_lat:, :])


def _gqa_attention(q, k, v, n_lat, n_rows):
    B, _, T, _ = q.shape
    tq = ROW_TILE
    group = A_HEADS // A_KV_HEADS
    kv_spec = lambda width: pl.BlockSpec((None, None, T, width), lambda b, g, t: (b, g, 0, 0))
    return pl.pallas_call(
        functools.partial(_gqa_kernel, n_lat=n_lat, n_lat_tiles=n_lat // tq),
        grid=(B, A_KV_HEADS, n_rows // tq),
        in_specs=[pl.BlockSpec((None, group, tq, HEAD_DIM), lambda b, g, t: (b, g, t, 0)),
                  kv_spec(HEAD_DIM), kv_spec(V_EXT)],
        out_specs=pl.BlockSpec((None, tq, group * HEAD_DIM), lambda b, g, t: (b, t, g)),
        out_shape=jax.ShapeDtypeStruct((B, n_rows, A_WIDTH), BF16),
        compiler_params=_cparams("parallel", "parallel", "parallel"),
        name="gqa_attention",
    )(q, k, v)


def _natten_kernel(q_ref, k_ref, v_ref, bias_ref, o_ref, *, n_lat, rows, need_ctx):
    win = WIN_ROWS * GRID_W
    for hd in range(C_HEADS):
        kc = k_ref[hd, n_lat:, :]
        vc = v_ref[hd, n_lat:, :]

        def row_group(g, carry, hd=hd, kc=kc, vc=vc):
            group_rows = NATTEN_ROWS_PER_ITER * GRID_W
            q_all = q_ref[hd, pl.ds(pl.multiple_of(g * group_rows, group_rows), group_rows), :]
            s_c_all = _dot_nt(q_all, kc)
            scores = []
            for u in range(NATTEN_ROWS_PER_ITER):
                r = g * NATTEN_ROWS_PER_ITER + u
                rs = jnp.clip(r - WIN_ROWS // 2, 0, rows - WIN_ROWS)
                kw = k_ref[hd, pl.ds(pl.multiple_of(rs * GRID_W, GRID_W), win), :]
                scores.append((r, rs, _dot_nt(q_all[u * GRID_W:(u + 1) * GRID_W], kw) + bias_ref[r - rs, hd]))
            probs = []
            for u, (r, rs, s_w) in enumerate(scores):
                s_c = s_c_all[u * GRID_W:(u + 1) * GRID_W]
                m = jnp.maximum(jnp.max(s_w, axis=-1, keepdims=True), jnp.max(s_c, axis=-1, keepdims=True))
                probs.append((r, rs, _softmax_weights(s_w, m), _softmax_weights(s_c, m)))
            o_c_all = _dot(jnp.concatenate([p_c for _, _, _, p_c in probs], axis=0), vc)
            for u, (r, rs, p_w, _) in enumerate(probs):
                vw = v_ref[hd, pl.ds(pl.multiple_of(rs * GRID_W, GRID_W), win), :]
                o = _normalised(_dot(p_w, vw) + o_c_all[u * GRID_W:(u + 1) * GRID_W])
                o_ref[pl.ds(pl.multiple_of(r * GRID_W, GRID_W), GRID_W),
                      hd * HEAD_DIM:(hd + 1) * HEAD_DIM] = o.astype(o_ref.dtype)
            return carry

        lax.fori_loop(0, rows // NATTEN_ROWS_PER_ITER, row_group, 0)
        if need_ctx:
            s = _dot_nt(q_ref[hd, n_lat:, :], kc)
            o = _normalised(_dot(_softmax_weights(s, jnp.max(s, axis=-1, keepdims=True)), vc))
            o_ref[n_lat:, hd * HEAD_DIM:(hd + 1) * HEAD_DIM] = o.astype(o_ref.dtype)


def _natten_bias(rpb, rows):
    off = np.arange(WIN_ROWS)[:, None, None]
    jr = np.arange(WIN_ROWS)[None, :, None]
    row_sel = (np.arange(2 * WIN_ROWS - 1)[None, None, :] == jr - off + WIN_ROWS - 1)
    qc = np.arange(GRID_W)[:, None, None]
    kc = np.arange(GRID_W)[None, :, None]
    col_start = np.clip(qc - WIN_COLS // 2, 0, GRID_W - WIN_COLS)
    valid = (kc >= col_start) & (kc < col_start + WIN_COLS)
    col_sel = (np.arange(2 * WIN_COLS - 1)[None, None, :] == kc - qc + WIN_COLS - 1) & valid
    bias = jnp.einsum("hrc,ojr,qkc->ohqjk", rpb.astype(F32), jnp.asarray(row_sel, F32), jnp.asarray(col_sel, F32),
                      precision=lax.Precision.HIGHEST)
    bias = bias + jnp.asarray(np.where(valid[None, None, :, None, :, 0], 0.0, MASK_VALUE), F32)
    return bias.reshape(WIN_ROWS, C_HEADS, GRID_W, WIN_ROWS * GRID_W)


def _neighbourhood_attention(q, k, v, bias, n_lat, need_ctx):
    B, _, T, _ = q.shape
    n_rows = T if need_ctx else n_lat
    full = lambda width: pl.BlockSpec((None, C_HEADS, T, width), lambda b: (b, 0, 0, 0))
    return pl.pallas_call(
        functools.partial(_natten_kernel, n_lat=n_lat, rows=n_lat // GRID_W, need_ctx=need_ctx),
        grid=(B,),
        in_specs=[full(HEAD_DIM), full(HEAD_DIM), full(V_EXT), pl.BlockSpec(bias.shape, lambda b: (0, 0, 0, 0))],
        out_specs=pl.BlockSpec((None, n_rows, C_WIDTH), lambda b: (b, 0, 0)),
        out_shape=jax.ShapeDtypeStruct((B, n_rows, C_WIDTH), BF16),
        compiler_params=_cparams("parallel"),
        name="neighbourhood_attention",
    )(q, k, v, bias)


def _rwkv_chunk_kernel(rw_ref, w0_ref, a0_ref, lora_ref, kk_ref, ka_ref, rk_ref, bdw_ref,
                       rhat_ref, yhat_ref, gt_ref, ht_ref, gate_ref, bonus_ref):
    C, W, NH = CHUNK, B_WIDTH, B_HEADS
    S = NH * C
    R = rw_ref.shape[0]
    rw = rw_ref[...]
    r, k, v = rw[:, :W], rw[:, W:2 * W], rw[:, 2 * W:3 * W]
    wl = rw[:, 3 * W:3 * W + 2 * DECAY_LORA]
    al = rw[:, 3 * W + 2 * DECAY_LORA:3 * W + 2 * DECAY_LORA + 2 * AAA_LORA]
    gl = rw[:, 3 * W + 2 * DECAY_LORA + 2 * AAA_LORA:]
    bdw = bdw_ref[...]
    kkv = k * kk_ref[...]
    kkv = kkv / jnp.maximum(jnp.sqrt(_head_sum(kkv * kkv, bdw)), 1e-12)
    lora = _dot(jnp.concatenate([jnp.tanh(wl), al, _sigmoid(gl)], axis=1), lora_ref[...])
    gate_ref[...] = lora[:, 4 * W:]
    lws, asigs = [], []
    for d in range(2):
        w = w0_ref[d] + lora[:, d * W:(d + 1) * W]
        w = -(jnp.maximum(-w, 0.0) + jnp.log(1.0 + jnp.exp(-jnp.abs(w)))) - 0.5
        lws.append(-jnp.exp(w))
        asigs.append(_sigmoid(a0_ref[d] + lora[:, (2 + d) * W:(3 + d) * W]))

    prow = lax.broadcasted_iota(jnp.int32, (R, R), 0)
    pcol = lax.broadcasted_iota(jnp.int32, (R, R), 1)
    tri = ((prow >= pcol) & (prow // C == pcol // C)).astype(BF16)
    cs = _dot(tri, jnp.concatenate([p for lw in lws for p in _split3(lw)], axis=1))
    prefix = [cs[:, 3 * d * W:(3 * d + 1) * W] + cs[:, (3 * d + 1) * W:(3 * d + 2) * W]
              + cs[:, (3 * d + 2) * W:(3 * d + 3) * W] for d in range(2)]

    srow = lax.broadcasted_iota(jnp.int32, (S, W), 0)
    scol = lax.broadcasted_iota(jnp.int32, (S, W), 1)
    same = (srow // C) == (scol // HEAD_DIM)
    crow = lax.broadcasted_iota(jnp.int32, (C, W), 0)
    cpos = lax.broadcasted_iota(jnp.int32, (C, W), 1) % HEAD_DIM
    eye_c = crow == cpos
    eye_b = eye_c.astype(BF16)

    def half_block(s):
        return (crow // (2 * s) == cpos // (2 * s)) & (crow // s != cpos // s)

    def stack(x):
        return jnp.where(same, jnp.concatenate([x] * NH, axis=0), 0.0).astype(BF16)

    kds = [k * (1.0 + (asig - 1.0) * ka_ref[...]) for asig in asigs]
    bonus_ref[...] = _head_sum(r * (kds[0] + kds[1]) * rk_ref[...], bdw) * v

    class Chain:
        pass

    chains = []
    for ci in range(R // C):
        v_s = stack(v[ci * C:(ci + 1) * C])
        for d in range(2):
            ch = Chain()
            ch.d, ch.ci, ch.rows, ch.v_s = d, ci, slice(ci * C, (ci + 1) * C), v_s
            chains.append(ch)

    for ch in chains:
        d, rows = ch.d, ch.rows
        lw, kd, b, a = lws[d][rows], kds[d][rows], (kkv * asigs[d])[rows], -kkv[rows]
        pre = prefix[d][rows]
        ch.tot = tot = pre[C - 1:C, :]
        cum_incl = pre if d == 0 else tot - pre + lw
        cum_excl = cum_incl - lw
        rho = 0.5 * tot
        e_in = jnp.exp(rho - cum_incl)
        e_end = jnp.exp(tot - cum_incl)
        ch.e_rho = jnp.exp(rho)
        ch.a_rho = a * jnp.exp(cum_excl - rho)
        ch.r_rho = r[rows] * jnp.exp(cum_incl - rho)
        ch.ends = jnp.concatenate([stack(b * e_end), stack(kd * e_end)], axis=0)
        ch.m = _dot_nt(jnp.concatenate([ch.a_rho, ch.r_rho], axis=0).astype(BF16),
                       jnp.concatenate([stack(b * e_in), stack(kd * e_in)], axis=0))

    for ch in chains:
        before, upto = (crow > cpos, crow >= cpos) if ch.d == 0 else (crow < cpos, crow <= cpos)
        m = ch.m
        l_ab = jnp.where(before, m[:C, :W], 0.0)
        ch.a_rb = jnp.where(upto, m[C:, :W], 0.0).astype(BF16)
        ch.kv = jnp.concatenate([jnp.where(before, m[:C, W:], 0.0), jnp.where(upto, m[C:, W:], 0.0)],
                                axis=0).astype(BF16)
        ch.l_ab = l_ab
        ch.tinv = jnp.where(eye_c, 1.0, jnp.where(half_block(1), l_ab, 0.0))
    for s in [2 ** i for i in range(1, int(np.log2(C)))]:
        for ch in chains:
            ch.p = _dot(jnp.where(half_block(s), ch.l_ab, 0.0).astype(BF16), stack(ch.tinv))
        for ch in chains:
            ch.tinv = ch.tinv + _dot(ch.tinv.astype(BF16), stack(ch.p))
    for ch in chains:
        ch.wv = _dot(ch.kv, ch.v_s)
    for ch in chains:
        ch.z = _dot(ch.tinv.astype(BF16),
                    jnp.concatenate([stack(ch.a_rho), stack(ch.wv[:C])], axis=1))
    for ch in chains:
        ch.ahat = ch.z[:, :W] * ch.e_rho
        ch.av = jnp.concatenate([stack(ch.z[:, :W]), stack(ch.z[:, W:])], axis=1)
        rz = _dot(ch.a_rb, ch.av)
        rhat_ref[ch.d, ch.rows] = (ch.r_rho + rz[:, :W]) * ch.e_rho
        yhat_ref[ch.d, ch.rows] = rz[:, W:] + ch.wv[C:]
    for ch in chains:
        ch.ends_t = _dot_nt(eye_b, ch.ends).astype(BF16)
    for ch in chains:
        gh = _dot(ch.ends_t[:, :S], jnp.concatenate([stack(ch.ahat), ch.av[:, W:]], axis=1))
        gt_ref[ch.d, ch.ci] = gh[:, :W] + jnp.where(eye_c, jnp.exp(ch.tot), 0.0)
        ht_ref[ch.d, ch.ci] = gh[:, W:] + _dot(ch.ends_t[:, S:], ch.v_s)


def _rwkv_chunks(rw, p, bdw):
    B, T, _ = rw.shape
    C, W = CHUNK, B_WIDTH
    nc = T // C
    per = RWKV_CHUNKS_PER_STEP
    assert C == HEAD_DIM and nc % per == 0
    const = lambda shape: pl.BlockSpec(shape, lambda b, c: (0,) * len(shape))
    rows = pl.BlockSpec((None, 2, per * C, W), lambda b, c: (b, 0, c, 0))
    mats = pl.BlockSpec((None, 2, per, C, W), lambda b, c: (b, 0, c, 0, 0))
    flat = pl.BlockSpec((None, per * C, W), lambda b, c: (b, c, 0))
    vec = lambda a: a.reshape(1, W)
    lora_w = jnp.zeros((2 * DECAY_LORA + 2 * AAA_LORA + GATE_LORA, 5 * W), F32)
    for d in range(2):
        lora_w = lora_w.at[d * DECAY_LORA:(d + 1) * DECAY_LORA, d * W:(d + 1) * W].set(p["w_up"][d])
        lora_w = lora_w.at[2 * DECAY_LORA + d * AAA_LORA:2 * DECAY_LORA + (d + 1) * AAA_LORA,
                           (2 + d) * W:(3 + d) * W].set(p["a_up"][d])
    lora_w = lora_w.at[2 * DECAY_LORA + 2 * AAA_LORA:, 4 * W:].set(p["g_up"])
    return pl.pallas_call(
        _rwkv_chunk_kernel,
        grid=(B, nc // per),
        in_specs=[pl.BlockSpec((None, per * C, RW_WIDTH), lambda b, c: (b, c, 0)),
                  const((2, 1, W)), const((2, 1, W)), const(lora_w.shape),
                  const((1, W)), const((1, W)), const((1, W)), const(bdw.shape)],
        out_specs=[rows, rows, mats, mats, flat, flat],
        out_shape=[jax.ShapeDtypeStruct((B, 2, T, W), F32), jax.ShapeDtypeStruct((B, 2, T, W), F32),
                   jax.ShapeDtypeStruct((B, 2, nc, C, W), F32), jax.ShapeDtypeStruct((B, 2, nc, C, W), F32),
                   jax.ShapeDtypeStruct((B, T, W), F32), jax.ShapeDtypeStruct((B, T, W), F32)],
        compiler_params=_cparams("parallel", "parallel"),
        name="rwkv_chunks",
    )(rw, p["w0"].reshape(2, 1, W), p["a0"].reshape(2, 1, W), lora_w,
      vec(p["k_k"]), vec(p["k_a"]), vec(p["r_k"]), bdw)


def _rwkv_scan_kernel(rhat0_ref, yhat0_ref, gt0_ref, ht0_ref, rhat1_ref, yhat1_ref, gt1_ref, ht1_ref,
                      y0_ref, y1_ref, st_ref):
    @pl.when(pl.program_id(0) == 0)
    def _():
        st_ref[...] = jnp.zeros_like(st_ref)

    W = B_WIDTH
    same = (lax.broadcasted_iota(jnp.int32, (W, W), 0) // HEAD_DIM
            == lax.broadcasted_iota(jnp.int32, (W, W), 1) // HEAD_DIM)
    expand = lambda x: jnp.where(same, jnp.concatenate([x] * B_HEADS, axis=0), 0.0)
    dirs = ((rhat0_ref, yhat0_ref, gt0_ref, ht0_ref, y0_ref), (rhat1_ref, yhat1_ref, gt1_ref, ht1_ref, y1_ref))
    for b in range(st_ref.shape[1]):
        for d, (rhat_ref, yhat_ref, gt_ref, ht_ref, y_ref) in enumerate(dirs):
            stb = st_ref[d, b].astype(BF16)
            y_ref[b] = _dot(rhat_ref[b].astype(BF16), stb) + yhat_ref[b]
            st_ref[d, b] = _dot(expand(gt_ref[b]).astype(BF16), stb) + expand(ht_ref[b])


def _rwkv_scan(rhat, yhat, gt, ht, n_lat_chunks):
    B, _, T, W = rhat.shape
    C = CHUNK
    nc = T // C
    chunk = (lambda s: (s + n_lat_chunks) % nc, lambda s: nc - 1 - s)
    rows = lambda d: pl.BlockSpec((B, None, C, W), lambda s: (0, d, chunk[d](s), 0))
    mats = lambda d: pl.BlockSpec((B, None, None, C, W), lambda s: (0, d, chunk[d](s), 0, 0))
    outs = lambda d: pl.BlockSpec((B, C, W), lambda s: (0, chunk[d](s), 0))
    return pl.pallas_call(
        _rwkv_scan_kernel,
        grid=(nc,),
        in_specs=[rows(0), rows(0), mats(0), mats(0), rows(1), rows(1), mats(1), mats(1)],
        out_specs=[outs(0), outs(1)],
        out_shape=[jax.ShapeDtypeStruct((B, T, W), F32)] * 2,
        scratch_shapes=[pltpu.VMEM((2, B, W, W), F32)],
        compiler_params=_cparams("arbitrary"),
        name="rwkv_scan",
    )(rhat, yhat, gt, ht, rhat, yhat, gt, ht)


def _outproj_kernel(xl_ref, xc_ref, oa_ref, y0_ref, y1_ref, bonus_ref, gate_ref, on_ref, lnw_ref, lnb_ref, bd_ref,
                    wa_ref, wb_ref, wn_ref, g_ref, mgate_ref, gf_ref, shiftf_ref, scalef_ref, wr_ref,
                    o_ref, h_ref, aff_ref, *, n_lat_tiles):
    y = y0_ref[...] + y1_ref[...]
    bd = bd_ref[...]
    mu = _head_sum(y, bd) * (1.0 / HEAD_DIM)
    yc = y - mu
    var = _head_sum(yc * yc, bd) * (1.0 / HEAD_DIM)
    ob = (yc * lax.rsqrt(var + LNX_EPS) * lnw_ref[...] + lnb_ref[...] + bonus_ref[...]) * gate_ref[...]
    out = _dot(oa_ref[...], wa_ref[...]) + _dot(ob.astype(BF16), wb_ref[...]) + _dot(on_ref[...], wn_ref[...])
    x = _segment_rows(xl_ref, xc_ref, n_lat_tiles) + mgate_ref[...] * _rms(out, g_ref[...])
    o_ref[...] = x
    h = _rms(x, gf_ref[...]) * (1.0 + scalef_ref[...]) + shiftf_ref[...]
    h_ref[...] = h.astype(h_ref.dtype)
    logits = _dot_nt(wr_ref[...], h)
    p = jnp.exp(logits - jnp.max(logits, axis=0, keepdims=True))
    aff_ref[...] = p / jnp.sum(p, axis=0, keepdims=True)


def _out_projection(x_lat, x_ctx, ctx_tile, oa, y0, y1, bonus, gate, on, p, bd, w_out, g_post, g_pre_ffn, w_router,
                    mod4, n_lat_tiles, n_rows):
    B, _, D = x_lat.shape
    tm = ROW_TILE
    W = B_WIDTH
    E = w_router.shape[1]
    wb = w_out.astype(BF16)
    const = lambda shape: pl.BlockSpec(shape, lambda b, t: (0,) * len(shape))
    rows = lambda width: pl.BlockSpec((None, tm, width), lambda b, t: (b, t, 0))
    mod_spec = lambda k: pl.BlockSpec(
        (None, None, 1, D), lambda b, t: (jnp.where(t >= n_lat_tiles, B, b), k, 0, 0))
    return pl.pallas_call(
        functools.partial(_outproj_kernel, n_lat_tiles=n_lat_tiles),
        grid=(B, n_rows // tm),
        in_specs=[*_segment_specs(D, n_lat_tiles, ctx_tile),
                  rows(A_WIDTH), rows(W), rows(W), rows(W), rows(W), rows(C_WIDTH),
                  const((1, W)), const((1, W)), const(bd.shape),
                  const((A_WIDTH, D)), const((W, D)), const((C_WIDTH, D)), const((1, D)), mod_spec(2),
                  const((1, D)), mod_spec(3), mod_spec(4), const((E, D))],
        out_specs=[rows(D), rows(D), pl.BlockSpec((None, E, tm), lambda b, t: (b, 0, t))],
        out_shape=[jax.ShapeDtypeStruct((B, n_rows, D), F32), jax.ShapeDtypeStruct((B, n_rows, D), BF16),
                   jax.ShapeDtypeStruct((B, E, n_rows), F32)],
        compiler_params=_cparams("parallel", "parallel"),
        name="out_projection",
    )(x_lat, x_ctx, oa, y0, y1, bonus, gate, on, p["lnx_w"].reshape(1, W), p["lnx_b"].reshape(1, W), bd,
      wb[:A_WIDTH], wb[A_WIDTH:A_WIDTH + W], wb[A_WIDTH + W:], g_post.reshape(1, D), mod4,
      g_pre_ffn.reshape(1, D), mod4, mod4, w_router.T)


def _column(x, e):
    lane = lax.broadcasted_iota(jnp.int32, x.shape, 1)
    return jnp.sum(jnp.where(lane == e, x, 0.0), axis=1, keepdims=True)


RANK_BLOCK = 256


def _rank_partials(a_row, a_col):
    n = a_row.shape[-1]
    tj = min(n, RANK_BLOCK)
    as_count = lambda mask: mask.astype(F32).astype(BF16)

    def partial(jc):
        lo, hi = jc * tj, (jc + 1) * tj
        aj = a_col[lo:hi, :]
        ai = a_row[:, lo:hi]
        j_first = (lax.broadcasted_iota(jnp.int32, (tj, tj), 0) < lax.broadcasted_iota(jnp.int32, (tj, tj), 1))
        parts = [as_count(aj > a_row[:, :lo])] if lo else []
        parts.append(as_count((aj > ai) | ((aj == ai) & j_first)))
        if hi < n:
            parts.append(as_count(aj >= a_row[:, hi:]))
        return _dot(jnp.ones((8, tj), BF16), parts[0] if len(parts) == 1 else jnp.concatenate(parts, axis=1))

    return [functools.partial(partial, jc) for jc in range(n // tj)]


def _token_ranks(a_row, a_col):
    cnt = None
    for thunk in _rank_partials(a_row, a_col):
        cnt = thunk() if cnt is None else cnt + thunk()
    return cnt[0:1, :]


def _propose_threshold(aff, cap):
    keys = lax.bitcast_convert_type(aff, jnp.int32)
    t = jnp.zeros((aff.shape[0], 1), jnp.int32)
    for bit in range(30, -1, -1):
        cand = t | (1 << bit)
        cnt = jnp.sum((keys >= cand).astype(F32), axis=1, keepdims=True)
        t = jnp.where(cnt >= cap, cand, t)
    return lax.bitcast_convert_type(t, F32)


def _prefix_count(mask):
    n = mask.shape[1]
    blk = min(n, RANK_BLOCK)
    tri = (lax.broadcasted_iota(jnp.int32, (blk, blk), 0) <= lax.broadcasted_iota(jnp.int32, (blk, blk), 1))
    tri = tri.astype(F32).astype(BF16)
    ones = mask.astype(F32).astype(BF16)
    outs = []
    for c in range(n // blk):
        p = _dot(ones[:, c * blk:(c + 1) * blk], tri)
        outs.append(p + outs[-1][:, blk - 1:blk] if outs else p)
    return outs[0] if len(outs) == 1 else jnp.concatenate(outs, axis=1)


def _slots_from_threshold(aff, tau, cap):
    above, equal = aff > tau, aff == tau
    n_above = jnp.sum(above.astype(F32), axis=1, keepdims=True)
    n_equal = jnp.sum(equal.astype(F32), axis=1, keepdims=True)
    ok = (n_above < cap) & (n_above + n_equal >= cap)
    take = above | (equal & (_prefix_count(equal) <= cap - n_above))
    return jnp.where(take, _prefix_count(take) - 1.0, -1.0), ok


def _expert_kernel(h_ref, aff_ref, affc_ref, wg_ref, wu_ref, wd_ref, f_ref, yet_ref, ye_ref, ft_ref, rank_ref,
                   *, cap, groups):
    n = h_ref.shape[0]
    E = aff_ref.shape[0]
    e = pl.program_id(1)

    @pl.when(e == 0)
    def _():
        seg, gcap = n // groups, cap // groups
        bad = 0.0
        for g in range(groups):
            cols = slice(g * seg, (g + 1) * seg)
            aff = aff_ref[:, cols]
            slots, ok = _slots_from_threshold(aff, _propose_threshold(aff, gcap), gcap)
            rank_ref[:, cols] = jnp.where(slots >= 0.0, slots + g * gcap, -1.0)
            bad = bad + jnp.sum(jnp.where(ok, 0.0, 1.0))

        @pl.when(bad > 0.0)
        def _():
            for g in range(groups):
                cols = slice(g * seg, (g + 1) * seg)

                def rerank(x, carry, g=g, cols=cols):
                    r = _token_ranks(aff_ref[pl.ds(x, 1), cols], _column(affc_ref[cols, :], x))
                    rank_ref[pl.ds(x, 1), cols] = jnp.where(r < gcap, r + g * gcap, -1.0)
                    return carry
                lax.fori_loop(0, E, rerank, 0)

    slot = lax.broadcasted_iota(jnp.int32, (cap, n), 0).astype(F32)
    pick = (rank_ref[pl.ds(e, 1), :] == slot).astype(F32).astype(BF16)
    xe = _dot(pick, h_ref[...]).astype(BF16)
    gte = _dot(xe, wg_ref[...])
    hid = (gte * _sigmoid(gte) * _dot(xe, wu_ref[...])).astype(BF16)
    ye_ref[...] = _dot(hid, wd_ref[...])
    yet_ref[:, pl.ds(pl.multiple_of(e * cap, cap), cap)] = ye_ref[...].T.astype(BF16)

    @pl.when(e == E - 1)
    def _():
        tn = min(n, ROW_TILE)
        slot_c = lax.broadcasted_iota(jnp.int32, (cap, tn), 0).astype(F32)
        for c in range(n // tn):
            cols = slice(c * tn, (c + 1) * tn)
            put = jnp.concatenate(
                [jnp.where(rank_ref[x:x + 1, cols] == slot_c, aff_ref[x:x + 1, cols], 0.0).astype(BF16)
                 for x in range(E)], axis=0)
            ft_ref[...] = _dot(yet_ref[...], put)
            f_ref[cols, :] = ft_ref[...].T


def _experts(h, aff, weights, layer, cap, groups):
    S, E, n = aff.shape
    D = h.shape[-1]
    F = weights[0].shape[-1]
    assert cap % LANES == 0 and (n // groups) % LANES == 0
    weight = lambda a, b: pl.BlockSpec((None, None, a, b), lambda s, e: (layer, e, 0, 0))
    return pl.pallas_call(
        functools.partial(_expert_kernel, cap=cap, groups=groups),
        grid=(S, E),
        in_specs=[pl.BlockSpec((None, n, D), lambda s, e: (s, 0, 0)),
                  pl.BlockSpec((None, E, n), lambda s, e: (s, 0, 0)),
                  pl.BlockSpec((None, n, E), lambda s, e: (s, 0, 0)),
                  weight(D, F), weight(D, F), weight(F, D)],
        out_specs=pl.BlockSpec((None, n, D), lambda s, e: (s, 0, 0)),
        out_shape=jax.ShapeDtypeStruct((S, n, D), F32),
        scratch_shapes=[pltpu.VMEM((D, E * cap), BF16), pltpu.VMEM((cap, D), F32),
                        pltpu.VMEM((D, min(n, ROW_TILE)), F32), pltpu.VMEM((E, n), F32)],
        compiler_params=_cparams("parallel", "arbitrary"),
        name="moe_experts",
    )(h, aff, jnp.swapaxes(aff, 1, 2), *weights)


def _ffn_residual_kernel(x_ref, fl_ref, fc_ref, g_ref, mgate_ref, o_ref, *, n_lat_tiles):
    f = _segment_rows(fl_ref, fc_ref, n_lat_tiles)
    o_ref[...] = x_ref[...] + mgate_ref[...] * _rms(f, g_ref[...])


def _ffn_residual(xs, f_lat, f_ctx, g_post, mod4, n_lat_tiles):
    B, R, D = xs.shape
    tm = ROW_TILE
    rows = pl.BlockSpec((None, tm, D), lambda b, t: (b, t, 0))
    return pl.pallas_call(
        functools.partial(_ffn_residual_kernel, n_lat_tiles=n_lat_tiles),
        grid=(B, R // tm),
        in_specs=[rows, *_segment_specs(D, n_lat_tiles, 0), pl.BlockSpec((1, D), lambda b, t: (0, 0)),
                  pl.BlockSpec((None, None, 1, D), lambda b, t: (jnp.where(t >= n_lat_tiles, B, b), 5, 0, 0))],
        out_specs=rows,
        out_shape=jax.ShapeDtypeStruct((B, R, D), F32),
        compiler_params=_cparams("parallel", "parallel"),
        name="ffn_residual",
    )(xs, f_lat, f_lat if f_ctx is None else f_ctx, g_post.reshape(1, D), mod4)


def _moe(h, aff_t, weights, layer, n_lat, n_ctx):
    B, _, D = h.shape
    E = aff_t.shape[1]
    f_l = _experts(h, aff_t[:, :, :n_lat], weights, layer, CAPACITY_FACTOR * n_lat // E, 1)
    if not n_ctx:
        return f_l, None
    a_c = jnp.swapaxes(aff_t[:, :, n_lat:], 0, 1).reshape(1, E, B * n_ctx)
    f_c = _experts(h[:, n_lat:].reshape(1, B * n_ctx, D), a_c, weights, layer,
                   B * (CAPACITY_FACTOR * n_ctx // E), B)
    return f_l, f_c.reshape(B, n_ctx, D)


def _rope_tables(n_lat, n_ctx):
    inv = ROPE_BASE ** (-jnp.arange(ROPE_HALF, dtype=F32) / ROPE_HALF)
    pos = jnp.arange(n_lat)
    ang_r = (pos // GRID_W).astype(F32)[:, None] * inv[None, :]
    ang_c = (pos % GRID_W).astype(F32)[:, None] * inv[None, :]
    cos = jnp.concatenate([jnp.cos(ang_r)] * 2 + [jnp.cos(ang_c)] * 2, axis=1)
    sin = jnp.concatenate([-jnp.sin(ang_r), jnp.sin(ang_r), -jnp.sin(ang_c), jnp.sin(ang_c)], axis=1)
    cos = jnp.concatenate([jnp.tile(cos, (1, LANES // HEAD_DIM)), jnp.ones((n_ctx, LANES), F32)], axis=0)
    sin = jnp.concatenate([jnp.tile(sin, (1, LANES // HEAD_DIM)), jnp.zeros((n_ctx, LANES), F32)], axis=0)
    return cos, sin


def kernel(x, c, ctx, c_ctx, w_mod, b_mod, g_pre_mix, g_post_mix, g_pre_ffn, g_post_ffn, w_in, w_out, q_gain, k_gain, rpb, rk_w0, rk_w_up, rk_a0, rk_a_up, rk_g_up, rk_k_k, rk_k_a, rk_r_k, rk_lnx_w, rk_lnx_b, w_router, w_e_gate, w_e_up, w_e_down):
    B, n_lat, D = x.shape
    n_ctx = ctx.shape[1]
    T = n_lat + n_ctx
    depth = w_mod.shape[0]
    assert n_lat % ROW_TILE == 0 and n_ctx == ROW_TILE and n_lat % n_ctx == 0 and B < MOD_ROWS
    n_lat_tiles = n_lat // ROW_TILE

    cond = jnp.zeros((MOD_ROWS, D), F32).at[:B].set(c).at[B].set(c_ctx)
    mod = _modulation(cond, w_mod, b_mod)
    cos, sin = _rope_tables(n_lat, n_ctx)
    lane = np.arange(HEAD_SUM_TILE)
    bd = jnp.asarray(lane[:, None] // HEAD_DIM == lane[None, :] // HEAD_DIM, BF16)

    weights = (w_e_gate.astype(BF16), w_e_up.astype(BF16), w_e_down.astype(BF16))
    x_lat, x_ctx, ctx_tile = x, ctx, 0
    for i in range(depth):
        need_ctx = i < depth - 1
        n_rows = T if need_ctx else n_lat
        mod4 = mod[i].reshape(MOD_ROWS, 6, 1, D)
        qa, ka, va, rw, qn, kn, vn = _in_projection(
            x_lat, x_ctx, ctx_tile, mod4, g_pre_mix[i], w_in[i], q_gain[i], k_gain[i], cos, sin, bd, n_lat_tiles)
        oa = _gqa_attention(qa, ka, va, n_lat, n_rows)
        on = _neighbourhood_attention(qn, kn, vn, _natten_bias(rpb[i], n_lat // GRID_W), n_lat, need_ctx)
        rk = dict(w0=rk_w0[i], w_up=rk_w_up[i], a0=rk_a0[i], a_up=rk_a_up[i], g_up=rk_g_up[i],
                  k_k=rk_k_k[i], k_a=rk_k_a[i], r_k=rk_r_k[i], lnx_w=rk_lnx_w[i], lnx_b=rk_lnx_b[i])
        rhat, yhat, gt, ht, gate, bonus = _rwkv_chunks(rw, rk, bd)
        y0, y1 = _rwkv_scan(rhat, yhat, gt, ht, n_lat // CHUNK)
        xs, h, aff_t = _out_projection(x_lat, x_ctx, ctx_tile, oa, y0, y1, bonus, gate, on, rk, bd, w_out[i],
                                       g_post_mix[i], g_pre_ffn[i], w_router[i], mod4, n_lat_tiles, n_rows)
        f_lat, f_ctx = _moe(h, aff_t, weights, i, n_lat, n_ctx if need_ctx else 0)
        x_lat = x_ctx = _ffn_residual(xs, f_lat, f_ctx, g_post_ffn[i], mod4, n_lat_tiles)
        ctx_tile = n_lat_tiles
    return x_lat
```

```python
import functools

import numpy as np
import jax
import jax.numpy as jnp
from jax import lax
from jax.experimental import pallas as pl
from jax.experimental.pallas import tpu as pltpu

F32 = jnp.float32
BF16 = jnp.bfloat16

HEAD_DIM = 64
GRID_W = 64
A_HEADS = 8
A_KV_HEADS = 2
B_HEADS = 4
C_HEADS = 4
A_WIDTH = A_HEADS * HEAD_DIM
A_KV_WIDTH = A_KV_HEADS * HEAD_DIM
B_WIDTH = B_HEADS * HEAD_DIM
C_WIDTH = C_HEADS * HEAD_DIM
DECAY_LORA = 64
AAA_LORA = 64
GATE_LORA = 128
RW_WIDTH = 3 * B_WIDTH + 2 * DECAY_LORA + 2 * AAA_LORA + GATE_LORA
ROPE_BASE = 10000.0
ROPE_HALF = HEAD_DIM // 4
WIN_ROWS = 8
WIN_COLS = 16
N_EXPERTS = 16
CAPACITY_FACTOR = 2
NORM_EPS = 1e-6
LNX_EPS = 64e-5
ATTN_SCALE = HEAD_DIM ** -0.5
V_EXT = 2 * HEAD_DIM
MASK_VALUE = -1e30

LANES = 128
HEAD_SUM_TILE = 256
ROW_TILE = 256
CHUNK = 64
RWKV_CHUNKS_PER_STEP = 6
NATTEN_ROWS_PER_ITER = 16
MOD_ROWS = 16
VMEM_LIMIT = 56 * 1024 * 1024


def _cparams(*sem):
    return pltpu.CompilerParams(dimension_semantics=sem, vmem_limit_bytes=VMEM_LIMIT)


def _dot(a, b):
    return jnp.dot(a, b, preferred_element_type=F32)


def _dot_nt(a, b):
    return lax.dot_general(a, b, (((1,), (1,)), ((), ())), preferred_element_type=F32)


def _dot_tn(a, b):
    return lax.dot_general(a, b, (((0,), (0,)), ((), ())), preferred_element_type=F32)


def _split2(x):
    hi = x.astype(BF16)
    lo = (x - hi.astype(F32)).astype(BF16)
    return hi, lo


def _split3(x):
    hi = x.astype(BF16)
    r1 = x - hi.astype(F32)
    mid = r1.astype(BF16)
    lo = (r1 - mid.astype(F32)).astype(BF16)
    return hi, mid, lo


def _head_sum(x, bd):
    rows, width = x.shape
    tile = bd.shape[0]
    outs = []
    for c in range(0, width, tile):
        w = min(tile, width - c)
        hi, lo = _split2(x[:, c:c + w])
        p = _dot(jnp.concatenate([hi, lo], axis=0), bd[:w, :w])
        outs.append(p[:rows] + p[rows:])
    return outs[0] if len(outs) == 1 else jnp.concatenate(outs, axis=1)


def _tile_lanes(t, width):
    reps = width // t.shape[1]
    return t if reps == 1 else jnp.concatenate([t] * reps, axis=1)


def _rms(x, g):
    return x * lax.rsqrt(jnp.mean(x * x, axis=-1, keepdims=True) + NORM_EPS) * g


def _sigmoid(x):
    return 1.0 / (1.0 + jnp.exp(-x))


def _segment_specs(width, n_lat_tiles, ctx_tile):
    return (pl.BlockSpec((None, ROW_TILE, width), lambda b, t: (b, jnp.minimum(t, n_lat_tiles - 1), 0)),
            pl.BlockSpec((None, ROW_TILE, width), lambda b, t: (b, ctx_tile, 0)))


def _segment_rows(lat_ref, ctx_ref, n_lat_tiles):
    return jnp.where(pl.program_id(1) >= n_lat_tiles, ctx_ref[...], lat_ref[...])


def _mod_kernel(s_ref, w_ref, b_ref, o_ref):
    s = s_ref[...]
    s = s * _sigmoid(s)
    o_ref[...] = _dot(s, w_ref[...]) + b_ref[...]


def _modulation(cond, w_mod, b_mod):
    L, D, N = w_mod.shape
    tn = 512
    return pl.pallas_call(
        _mod_kernel,
        grid=(L, N // tn),
        in_specs=[pl.BlockSpec((MOD_ROWS, D), lambda l, j: (0, 0)),
                  pl.BlockSpec((None, D, tn), lambda l, j: (l, 0, j)),
                  pl.BlockSpec((None, 1, tn), lambda l, j: (l, 0, j))],
        out_specs=pl.BlockSpec((None, MOD_ROWS, tn), lambda l, j: (l, 0, j)),
        out_shape=jax.ShapeDtypeStruct((L, MOD_ROWS, N), F32),
        compiler_params=_cparams("parallel", "parallel"),
        name="modulation",
    )(cond, w_mod, b_mod.reshape(L, 1, N))


def _rope(x, cos, sin):
    w = x.shape[1]
    lane = lax.broadcasted_iota(jnp.int32, x.shape, 1)
    upper = (lane // ROPE_HALF) % 2 == 1
    partner = jnp.where(upper, pltpu.roll(x, ROPE_HALF, 1), pltpu.roll(x, w - ROPE_HALF, 1))
    return x * _tile_lanes(cos, w) + partner * _tile_lanes(sin, w)


def _inproj_kernel(xl_ref, xc_ref, g_ref, shift_ref, scale_ref, wqa_ref, wka_ref, wva_ref, wrw_ref, wn_ref,
                   qg_ref, kg_ref, cos_ref, sin_ref, bd_ref, vone_ref,
                   qa_ref, ka_ref, va_ref, rw_ref, qn_ref, kn_ref, vn_ref, *, n_lat_tiles):
    x = _segment_rows(xl_ref, xc_ref, n_lat_tiles)
    h = _rms(x, g_ref[...]) * (1.0 + scale_ref[...]) + shift_ref[...]
    hb = h.astype(BF16)
    bd = bd_ref[...]
    cos, sin = cos_ref[...], sin_ref[...]

    def normed(w_ref, gain_ref):
        y = _dot(hb, w_ref[...])
        ms = _head_sum(y * y, bd) * (1.0 / HEAD_DIM)
        return _rope(y * lax.rsqrt(ms + NORM_EPS) * gain_ref[...], cos, sin)

    def put_heads(o_ref, y, width=HEAD_DIM):
        for hd in range(y.shape[1] // width):
            o_ref[hd] = y[:, hd * width:(hd + 1) * width].astype(o_ref.dtype)

    put_heads(qa_ref, normed(wqa_ref, qg_ref) * ATTN_SCALE)
    put_heads(ka_ref, normed(wka_ref, kg_ref))
    vone = vone_ref[...]
    put_heads(va_ref, _dot(hb, wva_ref[...]) + vone[:, :A_KV_HEADS * V_EXT], V_EXT)
    rw_ref[...] = _dot(hb, wrw_ref[...])
    yn = _dot(hb, wn_ref[...])
    put_heads(qn_ref, yn[:, :C_WIDTH] * ATTN_SCALE)
    put_heads(kn_ref, yn[:, C_WIDTH:2 * C_WIDTH])
    put_heads(vn_ref, yn[:, 2 * C_WIDTH:] + vone, V_EXT)


def _extend_values(w, heads):
    D = w.shape[0]
    w = jnp.pad(w.reshape(D, heads, HEAD_DIM), ((0, 0), (0, 0), (0, V_EXT - HEAD_DIM)))
    return w.reshape(D, heads * V_EXT)


def _in_projection(x_lat, x_ctx, ctx_tile, mod4, g_pre, w_in, q_gain, k_gain, cos, sin, bd, n_lat_tiles):
    B, _, D = x_lat.shape
    tm = ROW_TILE
    T = (n_lat_tiles + 1) * tm
    o = np.cumsum([0, A_WIDTH, A_KV_WIDTH, A_KV_WIDTH, RW_WIDTH, 2 * C_WIDTH, C_WIDTH])
    wb = w_in.astype(BF16)
    wqa, wka, wva, wrw, wqk, wvn = [wb[:, o[i]:o[i + 1]] for i in range(6)]
    ws = [wqa, wka, _extend_values(wva, A_KV_HEADS), wrw,
          jnp.concatenate([wqk, _extend_values(wvn, C_HEADS)], axis=1)]
    vone = jnp.asarray(np.arange(C_HEADS * V_EXT) % V_EXT == HEAD_DIM, F32).reshape(1, C_HEADS * V_EXT)
    const = lambda shape: pl.BlockSpec(shape, lambda b, t: (0,) * len(shape))
    mod_spec = lambda k: pl.BlockSpec(
        (None, None, 1, D), lambda b, t: (jnp.where(t >= n_lat_tiles, B, b), k, 0, 0))
    heads = lambda nh, width=HEAD_DIM: pl.BlockSpec((None, nh, tm, width), lambda b, t: (b, 0, t, 0))
    hshape = lambda nh, width=HEAD_DIM: jax.ShapeDtypeStruct((B, nh, T, width), BF16)
    return pl.pallas_call(
        functools.partial(_inproj_kernel, n_lat_tiles=n_lat_tiles),
        grid=(B, T // tm),
        in_specs=[*_segment_specs(D, n_lat_tiles, ctx_tile), const((1, D)), mod_spec(0), mod_spec(1)]
                 + [const(w.shape) for w in ws]
                 + [const((1, A_WIDTH)), const((1, A_KV_WIDTH)),
                    pl.BlockSpec((tm, LANES), lambda b, t: (t, 0)),
                    pl.BlockSpec((tm, LANES), lambda b, t: (t, 0)),
                    const(bd.shape), const(vone.shape)],
        out_specs=[heads(A_HEADS), heads(A_KV_HEADS), heads(A_KV_HEADS, V_EXT),
                   pl.BlockSpec((None, tm, RW_WIDTH), lambda b, t: (b, t, 0)),
                   heads(C_HEADS), heads(C_HEADS), heads(C_HEADS, V_EXT)],
        out_shape=[hshape(A_HEADS), hshape(A_KV_HEADS), hshape(A_KV_HEADS, V_EXT),
                   jax.ShapeDtypeStruct((B, T, RW_WIDTH), F32),
                   hshape(C_HEADS), hshape(C_HEADS), hshape(C_HEADS, V_EXT)],
        compiler_params=_cparams("parallel", "parallel"),
        name="in_projection",
    )(x_lat, x_ctx, g_pre.reshape(1, D), mod4, mod4, *ws,
      jnp.tile(q_gain, A_HEADS).reshape(1, A_WIDTH), jnp.tile(k_gain, A_KV_HEADS).reshape(1, A_KV_WIDTH),
      cos, sin, bd, vone)


def _softmax_weights(s, m):
    return jnp.exp((s - m).astype(BF16))


def _normalised(ov):
    return ov[:, :HEAD_DIM] / ov[:, HEAD_DIM:HEAD_DIM + 1]


def _gqa_kernel(q_ref, k_ref, v_ref, o_ref, *, n_lat, n_lat_tiles):
    group = q_ref.shape[0]

    def attend(k, v):
        scores = [_dot_nt(q_ref[hd], k) for hd in range(group)]
        probs = [_softmax_weights(s, jnp.max(s, axis=-1, keepdims=True)) for s in scores]
        for hd, p in enumerate(probs):
            o_ref[:, hd * HEAD_DIM:(hd + 1) * HEAD_DIM] = _normalised(_dot(p, v)).astype(o_ref.dtype)

    @pl.when(pl.program_id(2) < n_lat_tiles)
    def _():
        attend(k_ref[...], v_ref[...])

    @pl.when(pl.program_id(2) >= n_lat_tiles)
    def _():
        attend(k_ref[n_lat:, :], v_ref[n_lat:, :])


def _gqa_attention(q, k, v, n_lat, n_rows):
    B, _, T, _ = q.shape
    tq = ROW_TILE
    group = A_HEADS // A_KV_HEADS
    kv_spec = lambda width: pl.BlockSpec((None, None, T, width), lambda b, g, t: (b, g, 0, 0))
    return pl.pallas_call(
        functools.partial(_gqa_kernel, n_lat=n_lat, n_lat_tiles=n_lat // tq),
        grid=(B, A_KV_HEADS, n_rows // tq),
        in_specs=[pl.BlockSpec((None, group, tq, HEAD_DIM), lambda b, g, t: (b, g, t, 0)),
                  kv_spec(HEAD_DIM), kv_spec(V_EXT)],
        out_specs=pl.BlockSpec((None, tq, group * HEAD_DIM), lambda b, g, t: (b, t, g)),
        out_shape=jax.ShapeDtypeStruct((B, n_rows, A_WIDTH), BF16),
        compiler_params=_cparams("parallel", "parallel", "parallel"),
        name="gqa_attention",
    )(q, k, v)


def _natten_kernel(q_ref, k_ref, v_ref, bias_ref, o_ref, *, n_lat, rows, need_ctx):
    win = WIN_ROWS * GRID_W
    for hd in range(C_HEADS):
        kc = k_ref[hd, n_lat:, :]
        vc = v_ref[hd, n_lat:, :]

        def row_group(g, carry, hd=hd, kc=kc, vc=vc):
            group_rows = NATTEN_ROWS_PER_ITER * GRID_W
            q_all = q_ref[hd, pl.ds(pl.multiple_of(g * group_rows, group_rows), group_rows), :]
            s_c_all = _dot_nt(q_all, kc)
            scores = []
            for u in range(NATTEN_ROWS_PER_ITER):
                r = g * NATTEN_ROWS_PER_ITER + u
                rs = jnp.clip(r - WIN_ROWS // 2, 0, rows - WIN_ROWS)
                kw = k_ref[hd, pl.ds(pl.multiple_of(rs * GRID_W, GRID_W), win), :]
                scores.append((r, rs, _dot_nt(q_all[u * GRID_W:(u + 1) * GRID_W], kw) + bias_ref[r - rs, hd]))
            probs = []
            for u, (r, rs, s_w) in enumerate(scores):
                s_c = s_c_all[u * GRID_W:(u + 1) * GRID_W]
                m = jnp.maximum(jnp.max(s_w, axis=-1, keepdims=True), jnp.max(s_c, axis=-1, keepdims=True))
                probs.append((r, rs, _softmax_weights(s_w, m), _softmax_weights(s_c, m)))
            o_c_all = _dot(jnp.concatenate([p_c for _, _, _, p_c in probs], axis=0), vc)
            for u, (r, rs, p_w, _) in enumerate(probs):
                vw = v_ref[hd, pl.ds(pl.multiple_of(rs * GRID_W, GRID_W), win), :]
                o = _normalised(_dot(p_w, vw) + o_c_all[u * GRID_W:(u + 1) * GRID_W])
                o_ref[pl.ds(pl.multiple_of(r * GRID_W, GRID_W), GRID_W),
                      hd * HEAD_DIM:(hd + 1) * HEAD_DIM] = o.astype(o_ref.dtype)
            return carry

        lax.fori_loop(0, rows // NATTEN_ROWS_PER_ITER, row_group, 0)
        if need_ctx:
            s = _dot_nt(q_ref[hd, n_lat:, :], kc)
            o = _normalised(_dot(_softmax_weights(s, jnp.max(s, axis=-1, keepdims=True)), vc))
            o_ref[n_lat:, hd * HEAD_DIM:(hd + 1) * HEAD_DIM] = o.astype(o_ref.dtype)


def _natten_bias(rpb, rows):
    off = np.arange(WIN_ROWS)[:, None, None]
    jr = np.arange(WIN_ROWS)[None, :, None]
    row_sel = (np.arange(2 * WIN_ROWS - 1)[None, None, :] == jr - off + WIN_ROWS - 1)
    qc = np.arange(GRID_W)[:, None, None]
    kc = np.arange(GRID_W)[None, :, None]
    col_start = np.clip(qc - WIN_COLS // 2, 0, GRID_W - WIN_COLS)
    valid = (kc >= col_start) & (kc < col_start + WIN_COLS)
    col_sel = (np.arange(2 * WIN_COLS - 1)[None, None, :] == kc - qc + WIN_COLS - 1) & valid
    bias = jnp.einsum("hrc,ojr,qkc->ohqjk", rpb.astype(F32), jnp.asarray(row_sel, F32), jnp.asarray(col_sel, F32),
                      precision=lax.Precision.HIGHEST)
    bias = bias + jnp.asarray(np.where(valid[None, None, :, None, :, 0], 0.0, MASK_VALUE), F32)
    return bias.reshape(WIN_ROWS, C_HEADS, GRID_W, WIN_ROWS * GRID_W)


def _neighbourhood_attention(q, k, v, bias, n_lat, need_ctx):
    B, _, T, _ = q.shape
    n_rows = T if need_ctx else n_lat
    full = lambda width: pl.BlockSpec((None, C_HEADS, T, width), lambda b: (b, 0, 0, 0))
    return pl.pallas_call(
        functools.partial(_natten_kernel, n_lat=n_lat, rows=n_lat // GRID_W, need_ctx=need_ctx),
        grid=(B,),
        in_specs=[full(HEAD_DIM), full(HEAD_DIM), full(V_EXT), pl.BlockSpec(bias.shape, lambda b: (0, 0, 0, 0))],
        out_specs=pl.BlockSpec((None, n_rows, C_WIDTH), lambda b: (b, 0, 0)),
        out_shape=jax.ShapeDtypeStruct((B, n_rows, C_WIDTH), BF16),
        compiler_params=_cparams("parallel"),
        name="neighbourhood_attention",
    )(q, k, v, bias)


def _rwkv_chunk_kernel(rw_ref, w0_ref, a0_ref, lora_ref, kk_ref, ka_ref, rk_ref, bdw_ref,
                       rhat_ref, yhat_ref, gt_ref, ht_ref, gate_ref, bonus_ref):
    C, W, NH = CHUNK, B_WIDTH, B_HEADS
    S = NH * C
    R = rw_ref.shape[0]
    rw = rw_ref[...]
    r, k, v = rw[:, :W], rw[:, W:2 * W], rw[:, 2 * W:3 * W]
    wl = rw[:, 3 * W:3 * W + 2 * DECAY_LORA]
    al = rw[:, 3 * W + 2 * DECAY_LORA:3 * W + 2 * DECAY_LORA + 2 * AAA_LORA]
    gl = rw[:, 3 * W + 2 * DECAY_LORA + 2 * AAA_LORA:]
    bdw = bdw_ref[...]
    kkv = k * kk_ref[...]
    kkv = kkv / jnp.maximum(jnp.sqrt(_head_sum(kkv * kkv, bdw)), 1e-12)
    lora = _dot(jnp.concatenate([jnp.tanh(wl), al, _sigmoid(gl)], axis=1), lora_ref[...])
    gate_ref[...] = lora[:, 4 * W:]
    lws, asigs = [], []
    for d in range(2):
        w = w0_ref[d] + lora[:, d * W:(d + 1) * W]
        w = -(jnp.maximum(-w, 0.0) + jnp.log(1.0 + jnp.exp(-jnp.abs(w)))) - 0.5
        lws.append(-jnp.exp(w))
        asigs.append(_sigmoid(a0_ref[d] + lora[:, (2 + d) * W:(3 + d) * W]))

    prow = lax.broadcasted_iota(jnp.int32, (R, R), 0)
    pcol = lax.broadcasted_iota(jnp.int32, (R, R), 1)
    tri = ((prow >= pcol) & (prow // C == pcol // C)).astype(BF16)
    cs = _dot(tri, jnp.concatenate([p for lw in lws for p in _split3(lw)], axis=1))
    prefix = [cs[:, 3 * d * W:(3 * d + 1) * W] + cs[:, (3 * d + 1) * W:(3 * d + 2) * W]
              + cs[:, (3 * d + 2) * W:(3 * d + 3) * W] for d in range(2)]

    srow = lax.broadcasted_iota(jnp.int32, (S, W), 0)
    scol = lax.broadcasted_iota(jnp.int32, (S, W), 1)
    same = (srow // C) == (scol // HEAD_DIM)
    crow = lax.broadcasted_iota(jnp.int32, (C, W), 0)
    cpos = lax.broadcasted_iota(jnp.int32, (C, W), 1) % HEAD_DIM
    eye_c = crow == cpos
    eye_b = eye_c.astype(BF16)

    def half_block(s):
        return (crow // (2 * s) == cpos // (2 * s)) & (crow // s != cpos // s)

    def stack(x):
        return jnp.where(same, jnp.concatenate([x] * NH, axis=0), 0.0).astype(BF16)

    kds = [k * (1.0 + (asig - 1.0) * ka_ref[...]) for asig in asigs]
    bonus_ref[...] = _head_sum(r * (kds[0] + kds[1]) * rk_ref[...], bdw) * v

    class Chain:
        pass

    chains = []
    for ci in range(R // C):
        v_s = stack(v[ci * C:(ci + 1) * C])
        for d in range(2):
            ch = Chain()
            ch.d, ch.ci, ch.rows, ch.v_s = d, ci, slice(ci * C, (ci + 1) * C), v_s
            chains.append(ch)

    for ch in chains:
        d, rows = ch.d, ch.rows
        lw, kd, b, a = lws[d][rows], kds[d][rows], (kkv * asigs[d])[rows], -kkv[rows]
        pre = prefix[d][rows]
        ch.tot = tot = pre[C - 1:C, :]
        cum_incl = pre if d == 0 else tot - pre + lw
        cum_excl = cum_incl - lw
        rho = 0.5 * tot
        e_in = jnp.exp(rho - cum_incl)
        e_end = jnp.exp(tot - cum_incl)
        ch.e_rho = jnp.exp(rho)
        ch.a_rho = a * jnp.exp(cum_excl - rho)
        ch.r_rho = r[rows] * jnp.exp(cum_incl - rho)
        ch.ends = jnp.concatenate([stack(b * e_end), stack(kd * e_end)], axis=0)
        ch.m = _dot_nt(jnp.concatenate([ch.a_rho, ch.r_rho], axis=0).astype(BF16),
                       jnp.concatenate([stack(b * e_in), stack(kd * e_in)], axis=0))

    for ch in chains:
        before, upto = (crow > cpos, crow >= cpos) if ch.d == 0 else (crow < cpos, crow <= cpos)
        m = ch.m
        l_ab = jnp.where(before, m[:C, :W], 0.0)
        ch.a_rb = jnp.where(upto, m[C:, :W], 0.0).astype(BF16)
        ch.kv = jnp.concatenate([jnp.where(before, m[:C, W:], 0.0), jnp.where(upto, m[C:, W:], 0.0)],
                                axis=0).astype(BF16)
        ch.l_ab = l_ab
        ch.tinv = jnp.where(eye_c, 1.0, jnp.where(half_block(1), l_ab, 0.0))
    for s in [2 ** i for i in range(1, int(np.log2(C)))]:
        for ch in chains:
            ch.p = _dot(jnp.where(half_block(s), ch.l_ab, 0.0).astype(BF16), stack(ch.tinv))
        for ch in chains:
            ch.tinv = ch.tinv + _dot(ch.tinv.astype(BF16), stack(ch.p))
    for ch in chains:
        ch.wv = _dot(ch.kv, ch.v_s)
    for ch in chains:
        ch.z = _dot(ch.tinv.astype(BF16),
                    jnp.concatenate([stack(ch.a_rho), stack(ch.wv[:C])], axis=1))
    for ch in chains:
        ch.ahat = ch.z[:, :W] * ch.e_rho
        ch.av = jnp.concatenate([stack(ch.z[:, :W]), stack(ch.z[:, W:])], axis=1)
        rz = _dot(ch.a_rb, ch.av)
        rhat_ref[ch.d, ch.rows] = (ch.r_rho + rz[:, :W]) * ch.e_rho
        yhat_ref[ch.d, ch.rows] = rz[:, W:] + ch.wv[C:]
    for ch in chains:
        ch.ends_t = _dot_nt(eye_b, ch.ends).astype(BF16)
    for ch in chains:
        gh = _dot(ch.ends_t[:, :S], jnp.concatenate([stack(ch.ahat), ch.av[:, W:]], axis=1))
        gt_ref[ch.d, ch.ci] = gh[:, :W] + jnp.where(eye_c, jnp.exp(ch.tot), 0.0)
        ht_ref[ch.d, ch.ci] = gh[:, W:] + _dot(ch.ends_t[:, S:], ch.v_s)


def _rwkv_chunks(rw, p, bdw):
    B, T, _ = rw.shape
    C, W = CHUNK, B_WIDTH
    nc = T // C
    per = max(d for d in range(1, RWKV_CHUNKS_PER_STEP + 1) if nc % d == 0)
    assert C == HEAD_DIM and nc % per == 0
    const = lambda shape: pl.BlockSpec(shape, lambda b, c: (0,) * len(shape))
    rows = pl.BlockSpec((None, 2, per * C, W), lambda b, c: (b, 0, c, 0))
    mats = pl.BlockSpec((None, 2, per, C, W), lambda b, c: (b, 0, c, 0, 0))
    flat = pl.BlockSpec((None, per * C, W), lambda b, c: (b, c, 0))
    vec = lambda a: a.reshape(1, W)
    lora_w = jnp.zeros((2 * DECAY_LORA + 2 * AAA_LORA + GATE_LORA, 5 * W), F32)
    for d in range(2):
        lora_w = lora_w.at[d * DECAY_LORA:(d + 1) * DECAY_LORA, d * W:(d + 1) * W].set(p["w_up"][d])
        lora_w = lora_w.at[2 * DECAY_LORA + d * AAA_LORA:2 * DECAY_LORA + (d + 1) * AAA_LORA,
                           (2 + d) * W:(3 + d) * W].set(p["a_up"][d])
    lora_w = lora_w.at[2 * DECAY_LORA + 2 * AAA_LORA:, 4 * W:].set(p["g_up"])
    return pl.pallas_call(
        _rwkv_chunk_kernel,
        grid=(B, nc // per),
        in_specs=[pl.BlockSpec((None, per * C, RW_WIDTH), lambda b, c: (b, c, 0)),
                  const((2, 1, W)), const((2, 1, W)), const(lora_w.shape),
                  const((1, W)), const((1, W)), const((1, W)), const(bdw.shape)],
        out_specs=[rows, rows, mats, mats, flat, flat],
        out_shape=[jax.ShapeDtypeStruct((B, 2, T, W), F32), jax.ShapeDtypeStruct((B, 2, T, W), F32),
                   jax.ShapeDtypeStruct((B, 2, nc, C, W), F32), jax.ShapeDtypeStruct((B, 2, nc, C, W), F32),
                   jax.ShapeDtypeStruct((B, T, W), F32), jax.ShapeDtypeStruct((B, T, W), F32)],
        compiler_params=_cparams("parallel", "parallel"),
        name="rwkv_chunks",
    )(rw, p["w0"].reshape(2, 1, W), p["a0"].reshape(2, 1, W), lora_w,
      vec(p["k_k"]), vec(p["k_a"]), vec(p["r_k"]), bdw)


def _rwkv_scan_kernel(rhat0_ref, yhat0_ref, gt0_ref, ht0_ref, rhat1_ref, yhat1_ref, gt1_ref, ht1_ref,
                      y0_ref, y1_ref, st_ref):
    @pl.when(pl.program_id(0) == 0)
    def _():
        st_ref[...] = jnp.zeros_like(st_ref)

    W = B_WIDTH
    same = (lax.broadcasted_iota(jnp.int32, (W, W), 0) // HEAD_DIM
            == lax.broadcasted_iota(jnp.int32, (W, W), 1) // HEAD_DIM)
    expand = lambda x: jnp.where(same, jnp.concatenate([x] * B_HEADS, axis=0), 0.0)
    dirs = ((rhat0_ref, yhat0_ref, gt0_ref, ht0_ref, y0_ref), (rhat1_ref, yhat1_ref, gt1_ref, ht1_ref, y1_ref))
    for b in range(st_ref.shape[1]):
        for d, (rhat_ref, yhat_ref, gt_ref, ht_ref, y_ref) in enumerate(dirs):
            stb = st_ref[d, b].astype(BF16)
            y_ref[b] = _dot(rhat_ref[b].astype(BF16), stb) + yhat_ref[b]
            st_ref[d, b] = _dot(expand(gt_ref[b]).astype(BF16), stb) + expand(ht_ref[b])


def _rwkv_scan(rhat, yhat, gt, ht, n_lat_chunks):
    B, _, T, W = rhat.shape
    C = CHUNK
    nc = T // C
    chunk = (lambda s: (s + n_lat_chunks) % nc, lambda s: nc - 1 - s)
    rows = lambda d: pl.BlockSpec((B, None, C, W), lambda s: (0, d, chunk[d](s), 0))
    mats = lambda d: pl.BlockSpec((B, None, None, C, W), lambda s: (0, d, chunk[d](s), 0, 0))
    outs = lambda d: pl.BlockSpec((B, C, W), lambda s: (0, chunk[d](s), 0))
    return pl.pallas_call(
        _rwkv_scan_kernel,
        grid=(nc,),
        in_specs=[rows(0), rows(0), mats(0), mats(0), rows(1), rows(1), mats(1), mats(1)],
        out_specs=[outs(0), outs(1)],
        out_shape=[jax.ShapeDtypeStruct((B, T, W), F32)] * 2,
        scratch_shapes=[pltpu.VMEM((2, B, W, W), F32)],
        compiler_params=_cparams("arbitrary"),
        name="rwkv_scan",
    )(rhat, yhat, gt, ht, rhat, yhat, gt, ht)


def _outproj_kernel(xl_ref, xc_ref, oa_ref, y0_ref, y1_ref, bonus_ref, gate_ref, on_ref, lnw_ref, lnb_ref, bd_ref,
                    wa_ref, wb_ref, wn_ref, g_ref, mgate_ref, gf_ref, shiftf_ref, scalef_ref, wr_ref,
                    o_ref, h_ref, aff_ref, *, n_lat_tiles):
    y = y0_ref[...] + y1_ref[...]
    bd = bd_ref[...]
    mu = _head_sum(y, bd) * (1.0 / HEAD_DIM)
    yc = y - mu
    var = _head_sum(yc * yc, bd) * (1.0 / HEAD_DIM)
    ob = (yc * lax.rsqrt(var + LNX_EPS) * lnw_ref[...] + lnb_ref[...] + bonus_ref[...]) * gate_ref[...]
    out = _dot(oa_ref[...], wa_ref[...]) + _dot(ob.astype(BF16), wb_ref[...]) + _dot(on_ref[...], wn_ref[...])
    x = _segment_rows(xl_ref, xc_ref, n_lat_tiles) + mgate_ref[...] * _rms(out, g_ref[...])
    o_ref[...] = x
    h = _rms(x, gf_ref[...]) * (1.0 + scalef_ref[...]) + shiftf_ref[...]
    h_ref[...] = h.astype(h_ref.dtype)
    logits = _dot_nt(wr_ref[...], h)
    p = jnp.exp(logits - jnp.max(logits, axis=0, keepdims=True))
    aff_ref[...] = p / jnp.sum(p, axis=0, keepdims=True)


def _out_projection(x_lat, x_ctx, ctx_tile, oa, y0, y1, bonus, gate, on, p, bd, w_out, g_post, g_pre_ffn, w_router,
                    mod4, n_lat_tiles, n_rows):
    B, _, D = x_lat.shape
    tm = ROW_TILE
    W = B_WIDTH
    E = w_router.shape[1]
    wb = w_out.astype(BF16)
    const = lambda shape: pl.BlockSpec(shape, lambda b, t: (0,) * len(shape))
    rows = lambda width: pl.BlockSpec((None, tm, width), lambda b, t: (b, t, 0))
    mod_spec = lambda k: pl.BlockSpec(
        (None, None, 1, D), lambda b, t: (jnp.where(t >= n_lat_tiles, B, b), k, 0, 0))
    return pl.pallas_call(
        functools.partial(_outproj_kernel, n_lat_tiles=n_lat_tiles),
        grid=(B, n_rows // tm),
        in_specs=[*_segment_specs(D, n_lat_tiles, ctx_tile),
                  rows(A_WIDTH), rows(W), rows(W), rows(W), rows(W), rows(C_WIDTH),
                  const((1, W)), const((1, W)), const(bd.shape),
                  const((A_WIDTH, D)), const((W, D)), const((C_WIDTH, D)), const((1, D)), mod_spec(2),
                  const((1, D)), mod_spec(3), mod_spec(4), const((E, D))],
        out_specs=[rows(D), rows(D), pl.BlockSpec((None, E, tm), lambda b, t: (b, 0, t))],
        out_shape=[jax.ShapeDtypeStruct((B, n_rows, D), F32), jax.ShapeDtypeStruct((B, n_rows, D), BF16),
                   jax.ShapeDtypeStruct((B, E, n_rows), F32)],
        compiler_params=_cparams("parallel", "parallel"),
        name="out_projection",
    )(x_lat, x_ctx, oa, y0, y1, bonus, gate, on, p["lnx_w"].reshape(1, W), p["lnx_b"].reshape(1, W), bd,
      wb[:A_WIDTH], wb[A_WIDTH:A_WIDTH + W], wb[A_WIDTH + W:], g_post.reshape(1, D), mod4,
      g_pre_ffn.reshape(1, D), mod4, mod4, w_router.T)


def _column(x, e):
    lane = lax.broadcasted_iota(jnp.int32, x.shape, 1)
    return jnp.sum(jnp.where(lane == e, x, 0.0), axis=1, keepdims=True)


RANK_BLOCK = 256


def _rank_partials(a_row, a_col):
    n = a_row.shape[-1]
    tj = min(n, RANK_BLOCK)
    as_count = lambda mask: mask.astype(F32).astype(BF16)

    def partial(jc):
        lo, hi = jc * tj, (jc + 1) * tj
        aj = a_col[lo:hi, :]
        ai = a_row[:, lo:hi]
        j_first = (lax.broadcasted_iota(jnp.int32, (tj, tj), 0) < lax.broadcasted_iota(jnp.int32, (tj, tj), 1))
        parts = [as_count(aj > a_row[:, :lo])] if lo else []
        parts.append(as_count((aj > ai) | ((aj == ai) & j_first)))
        if hi < n:
            parts.append(as_count(aj >= a_row[:, hi:]))
        return _dot(jnp.ones((8, tj), BF16), parts[0] if len(parts) == 1 else jnp.concatenate(parts, axis=1))

    return [functools.partial(partial, jc) for jc in range(n // tj)]


def _token_ranks(a_row, a_col):
    cnt = None
    for thunk in _rank_partials(a_row, a_col):
        cnt = thunk() if cnt is None else cnt + thunk()
    return cnt[0:1, :]


def _propose_threshold(aff, cap):
    keys = lax.bitcast_convert_type(aff, jnp.int32)
    t = jnp.zeros((aff.shape[0], 1), jnp.int32)
    for bit in range(30, -1, -1):
        cand = t | (1 << bit)
        cnt = jnp.sum((keys >= cand).astype(F32), axis=1, keepdims=True)
        t = jnp.where(cnt >= cap, cand, t)
    return lax.bitcast_convert_type(t, F32)


def _prefix_count(mask):
    n = mask.shape[1]
    blk = min(n, RANK_BLOCK)
    tri = (lax.broadcasted_iota(jnp.int32, (blk, blk), 0) <= lax.broadcasted_iota(jnp.int32, (blk, blk), 1))
    tri = tri.astype(F32).astype(BF16)
    ones = mask.astype(F32).astype(BF16)
    outs = []
    for c in range(n // blk):
        p = _dot(ones[:, c * blk:(c + 1) * blk], tri)
        outs.append(p + outs[-1][:, blk - 1:blk] if outs else p)
    return outs[0] if len(outs) == 1 else jnp.concatenate(outs, axis=1)


def _slots_from_threshold(aff, tau, cap):
    above, equal = aff > tau, aff == tau
    n_above = jnp.sum(above.astype(F32), axis=1, keepdims=True)
    n_equal = jnp.sum(equal.astype(F32), axis=1, keepdims=True)
    ok = (n_above < cap) & (n_above + n_equal >= cap)
    take = above | (equal & (_prefix_count(equal) <= cap - n_above))
    return jnp.where(take, _prefix_count(take) - 1.0, -1.0), ok


def _expert_kernel(h_ref, aff_ref, affc_ref, wg_ref, wu_ref, wd_ref, f_ref, yet_ref, ye_ref, ft_ref, rank_ref,
                   *, cap, groups):
    n = h_ref.shape[0]
    E = aff_ref.shape[0]
    e = pl.program_id(1)

    @pl.when(e == 0)
    def _():
        seg, gcap = n // groups, cap // groups
        bad = 0.0
        for g in range(groups):
            cols = slice(g * seg, (g + 1) * seg)
            aff = aff_ref[:, cols]
            slots, ok = _slots_from_threshold(aff, _propose_threshold(aff, gcap), gcap)
            rank_ref[:, cols] = jnp.where(slots >= 0.0, slots + g * gcap, -1.0)
            bad = bad + jnp.sum(jnp.where(ok, 0.0, 1.0))

        @pl.when(bad > 0.0)
        def _():
            for g in range(groups):
                cols = slice(g * seg, (g + 1) * seg)

                def rerank(x, carry, g=g, cols=cols):
                    r = _token_ranks(aff_ref[pl.ds(x, 1), cols], _column(affc_ref[cols, :], x))
                    rank_ref[pl.ds(x, 1), cols] = jnp.where(r < gcap, r + g * gcap, -1.0)
                    return carry
                lax.fori_loop(0, E, rerank, 0)

    slot = lax.broadcasted_iota(jnp.int32, (cap, n), 0).astype(F32)
    pick = (rank_ref[pl.ds(e, 1), :] == slot).astype(F32).astype(BF16)
    xe = _dot(pick, h_ref[...]).astype(BF16)
    gte = _dot(xe, wg_ref[...])
    hid = (gte * _sigmoid(gte) * _dot(xe, wu_ref[...])).astype(BF16)
    ye_ref[...] = _dot(hid, wd_ref[...])
    yet_ref[:, pl.ds(pl.multiple_of(e * cap, cap), cap)] = ye_ref[...].T.astype(BF16)

    @pl.when(e == E - 1)
    def _():
        tn = min(n, ROW_TILE)
        slot_c = lax.broadcasted_iota(jnp.int32, (cap, tn), 0).astype(F32)
        for c in range(n // tn):
            cols = slice(c * tn, (c + 1) * tn)
            put = jnp.concatenate(
                [jnp.where(rank_ref[x:x + 1, cols] == slot_c, aff_ref[x:x + 1, cols], 0.0).astype(BF16)
                 for x in range(E)], axis=0)
            ft_ref[...] = _dot(yet_ref[...], put)
            f_ref[cols, :] = ft_ref[...].T


def _experts(h, aff, weights, layer, cap, groups):
    S, E, n = aff.shape
    D = h.shape[-1]
    F = weights[0].shape[-1]
    assert cap % LANES == 0 and (n // groups) % LANES == 0
    weight = lambda a, b: pl.BlockSpec((None, None, a, b), lambda s, e: (layer, e, 0, 0))
    return pl.pallas_call(
        functools.partial(_expert_kernel, cap=cap, groups=groups),
        grid=(S, E),
        in_specs=[pl.BlockSpec((None, n, D), lambda s, e: (s, 0, 0)),
                  pl.BlockSpec((None, E, n), lambda s, e: (s, 0, 0)),
                  pl.BlockSpec((None, n, E), lambda s, e: (s, 0, 0)),
                  weight(D, F), weight(D, F), weight(F, D)],
        out_specs=pl.BlockSpec((None, n, D), lambda s, e: (s, 0, 0)),
        out_shape=jax.ShapeDtypeStruct((S, n, D), F32),
        scratch_shapes=[pltpu.VMEM((D, E * cap), BF16), pltpu.VMEM((cap, D), F32),
                        pltpu.VMEM((D, min(n, ROW_TILE)), F32), pltpu.VMEM((E, n), F32)],
        compiler_params=_cparams("parallel", "arbitrary"),
        name="moe_experts",
    )(h, aff, jnp.swapaxes(aff, 1, 2), *weights)


def _ffn_residual_kernel(x_ref, fl_ref, fc_ref, g_ref, mgate_ref, o_ref, *, n_lat_tiles):
    f = _segment_rows(fl_ref, fc_ref, n_lat_tiles)
    o_ref[...] = x_ref[...] + mgate_ref[...] * _rms(f, g_ref[...])


def _ffn_residual(xs, f_lat, f_ctx, g_post, mod4, n_lat_tiles):
    B, R, D = xs.shape
    tm = ROW_TILE
    rows = pl.BlockSpec((None, tm, D), lambda b, t: (b, t, 0))
    return pl.pallas_call(
        functools.partial(_ffn_residual_kernel, n_lat_tiles=n_lat_tiles),
        grid=(B, R // tm),
        in_specs=[rows, *_segment_specs(D, n_lat_tiles, 0), pl.BlockSpec((1, D), lambda b, t: (0, 0)),
                  pl.BlockSpec((None, None, 1, D), lambda b, t: (jnp.where(t >= n_lat_tiles, B, b), 5, 0, 0))],
        out_specs=rows,
        out_shape=jax.ShapeDtypeStruct((B, R, D), F32),
        compiler_params=_cparams("parallel", "parallel"),
        name="ffn_residual",
    )(xs, f_lat, f_lat if f_ctx is None else f_ctx, g_post.reshape(1, D), mod4)


def _moe(h, aff_t, weights, layer, n_lat, n_ctx):
    B, _, D = h.shape
    E = aff_t.shape[1]
    f_l = _experts(h, aff_t[:, :, :n_lat], weights, layer, CAPACITY_FACTOR * n_lat // E, 1)
    if not n_ctx:
        return f_l, None
    a_c = jnp.swapaxes(aff_t[:, :, n_lat:], 0, 1).reshape(1, E, B * n_ctx)
    f_c = _experts(h[:, n_lat:].reshape(1, B * n_ctx, D), a_c, weights, layer,
                   B * (CAPACITY_FACTOR * n_ctx // E), B)
    return f_l, f_c.reshape(B, n_ctx, D)


def _rope_tables(n_lat, n_ctx):
    inv = ROPE_BASE ** (-jnp.arange(ROPE_HALF, dtype=F32) / ROPE_HALF)
    pos = jnp.arange(n_lat)
    ang_r = (pos // GRID_W).astype(F32)[:, None] * inv[None, :]
    ang_c = (pos % GRID_W).astype(F32)[:, None] * inv[None, :]
    cos = jnp.concatenate([jnp.cos(ang_r)] * 2 + [jnp.cos(ang_c)] * 2, axis=1)
    sin = jnp.concatenate([-jnp.sin(ang_r), jnp.sin(ang_r), -jnp.sin(ang_c), jnp.sin(ang_c)], axis=1)
    cos = jnp.concatenate([jnp.tile(cos, (1, LANES // HEAD_DIM)), jnp.ones((n_ctx, LANES), F32)], axis=0)
    sin = jnp.concatenate([jnp.tile(sin, (1, LANES // HEAD_DIM)), jnp.zeros((n_ctx, LANES), F32)], axis=0)
    return cos, sin


def kernel(x, c, ctx, c_ctx, w_mod, b_mod, g_pre_mix, g_post_mix, g_pre_ffn, g_post_ffn, w_in, w_out, q_gain, k_gain, rpb, rk_w0, rk_w_up, rk_a0, rk_a_up, rk_g_up, rk_k_k, rk_k_a, rk_r_k, rk_lnx_w, rk_lnx_b, w_router, w_e_gate, w_e_up, w_e_down):
    B, n_lat, D = x.shape
    n_ctx = ctx.shape[1]
    T = n_lat + n_ctx
    depth = w_mod.shape[0]
    assert n_lat % ROW_TILE == 0 and n_ctx == ROW_TILE and n_lat % n_ctx == 0 and B < MOD_ROWS
    n_lat_tiles = n_lat // ROW_TILE

    cond = jnp.zeros((MOD_ROWS, D), F32).at[:B].set(c).at[B].set(c_ctx)
    mod = _modulation(cond, w_mod, b_mod)
    cos, sin = _rope_tables(n_lat, n_ctx)
    lane = np.arange(HEAD_SUM_TILE)
    bd = jnp.asarray(lane[:, None] // HEAD_DIM == lane[None, :] // HEAD_DIM, BF16)

    weights = (w_e_gate.astype(BF16), w_e_up.astype(BF16), w_e_down.astype(BF16))
    x_lat, x_ctx, ctx_tile = x, ctx, 0
    for i in range(depth):
        need_ctx = i < depth - 1
        n_rows = T if need_ctx else n_lat
        mod4 = mod[i].reshape(MOD_ROWS, 6, 1, D)
        qa, ka, va, rw, qn, kn, vn = _in_projection(
            x_lat, x_ctx, ctx_tile, mod4, g_pre_mix[i], w_in[i], q_gain[i], k_gain[i], cos, sin, bd, n_lat_tiles)
        oa = _gqa_attention(qa, ka, va, n_lat, n_rows)
        on = _neighbourhood_attention(qn, kn, vn, _natten_bias(rpb[i], n_lat // GRID_W), n_lat, need_ctx)
        rk = dict(w0=rk_w0[i], w_up=rk_w_up[i], a0=rk_a0[i], a_up=rk_a_up[i], g_up=rk_g_up[i],
                  k_k=rk_k_k[i], k_a=rk_k_a[i], r_k=rk_r_k[i], lnx_w=rk_lnx_w[i], lnx_b=rk_lnx_b[i])
        rhat, yhat, gt, ht, gate, bonus = _rwkv_chunks(rw, rk, bd)
        y0, y1 = _rwkv_scan(rhat, yhat, gt, ht, n_lat // CHUNK)
        xs, h, aff_t = _out_projection(x_lat, x_ctx, ctx_tile, oa, y0, y1, bonus, gate, on, rk, bd, w_out[i],
                                       g_post_mix[i], g_pre_ffn[i], w_router[i], mod4, n_lat_tiles, n_rows)
        f_lat, f_ctx = _moe(h, aff_t, weights, i, n_lat, n_ctx if need_ctx else 0)
        x_lat = x_ctx = _ffn_residual(xs, f_lat, f_ctx, g_post_ffn[i], mod4, n_lat_tiles)
        ctx_tile = n_lat_tiles
    return x_lat
```
